```python
import jax
import jax.numpy as jnp
from jax import lax
import numpy as np

D_MODEL = 2048
BATCH = 2
SEQ = 8192
DEPTH = 4

N_MIXERS = 2
SC_KERNEL = 3
DN_HEAD_DIM = 128
DN_KEY_HEADS = D_MODEL // DN_HEAD_DIM
DN_VALUE_HEADS = 2 * DN_KEY_HEADS
DN_KEY_DIM = DN_KEY_HEADS * DN_HEAD_DIM
DN_VALUE_DIM = DN_VALUE_HEADS * DN_HEAD_DIM
DN_QKV_DIM = 2 * DN_KEY_DIM + DN_VALUE_DIM
DN_PROJ_DIM = DN_QKV_DIM + DN_VALUE_DIM + 2 * DN_VALUE_HEADS
DN_CONV_KERNEL = 4
DN_CHUNK = 64
FFN_HIDDEN = -(-8 * D_MODEL // (3 * 256)) * 256
DEEPNORM_ALPHA = (2 * DEPTH) ** 0.25
DEEPNORM_BETA = (8 * DEPTH) ** -0.25
LN_EPS = 1e-5
RMS_EPS = 1e-6
N_SC_LAYERS = (DEPTH + 1) // 2
N_DN_LAYERS = DEPTH // 2

kernel_name = 'hybrid_shortconv_gdn_deepnorm'


def layer_norm(x, gain, bias):
    xf = x.astype(jnp.float32)
    mu = jnp.mean(xf, -1, keepdims=True)
    var = jnp.mean(jnp.square(xf - mu), -1, keepdims=True)
    return ((xf - mu) * lax.rsqrt(var + LN_EPS) * gain + bias).astype(x.dtype)


def l2_normalize(t):
    return t * lax.rsqrt(jnp.sum(t * t, -1, keepdims=True) + RMS_EPS)


def causal_depthwise_conv(u, w):
    k = w.shape[0]
    s = u.shape[1]
    u_pad = jnp.pad(u, ((0, 0), (k - 1, 0), (0, 0)))
    out = u_pad[:, 0:s] * w[0]
    for j in range(1, k):
        out = out + u_pad[:, j:j + s] * w[j]
    return out


def short_conv_mixer(x, w_in, conv_w, w_out):
    proj = x @ w_in
    gate_b = proj[..., :D_MODEL]
    gate_c = proj[..., D_MODEL:2 * D_MODEL]
    h = proj[..., 2 * D_MODEL:]
    y = causal_depthwise_conv(gate_c * h, conv_w)
    return (gate_b * y) @ w_out


def chunk_gated_delta_rule(q, k, v, beta, g):
    b, s, h, dk = q.shape
    dv = v.shape[-1]
    c = DN_CHUNK
    n = s // c

    def blocks(t):
        return t.reshape(b, n, c, h, -1).transpose(0, 3, 1, 2, 4)

    q, k, v = blocks(q), blocks(k), blocks(v)
    beta = blocks(beta[..., None])[..., 0]
    g = lax.cumsum(blocks(g[..., None])[..., 0], axis=3)
    causal = jnp.tril(jnp.ones((c, c), dtype=bool))
    strict = jnp.tril(jnp.ones((c, c), dtype=bool), -1)
    decay = jnp.exp(jnp.where(causal, g[..., :, None] - g[..., None, :], -jnp.inf))
    k_beta = k * beta[..., None]
    v_beta = v * beta[..., None]
    a_kk = jnp.where(strict, jnp.einsum('bhncd,bhnmd->bhncm', k_beta, k) * decay, 0.0) + jnp.eye(c, dtype=q.dtype)
    rhs = jnp.concatenate([v_beta, k_beta * jnp.exp(g)[..., None]], axis=-1)
    sol = lax.linalg.triangular_solve(a_kk, rhs, left_side=True, lower=True)
    u = sol[..., :dv]
    w = sol[..., dv:]
    a_qk = jnp.einsum('bhncd,bhnmd->bhncm', q, k) * decay
    q_dec = q * jnp.exp(g)[..., None]
    k_tail = k * jnp.exp(g[..., -1:] - g)[..., None]
    chunk_decay = jnp.exp(g[..., -1])

    def step(state, inp):
        qd, kt, uc, wc, aqk, cd = inp
        v_new = uc - jnp.einsum('bhck,bhkv->bhcv', wc, state)
        o = jnp.einsum('bhck,bhkv->bhcv', qd, state) + jnp.einsum('bhcm,bhmv->bhcv', aqk, v_new)
        state = state * cd[..., None, None] + jnp.einsum('bhck,bhcv->bhkv', kt, v_new)
        return state, o

    xs = (jnp.moveaxis(q_dec, 2, 0), jnp.moveaxis(k_tail, 2, 0), jnp.moveaxis(u, 2, 0),
          jnp.moveaxis(w, 2, 0), jnp.moveaxis(a_qk, 2, 0), jnp.moveaxis(chunk_decay, 2, 0))
    _, o = lax.scan(step, jnp.zeros((b, h, dk, dv), q.dtype), xs)
    return o.transpose(1, 0, 3, 2, 4).reshape(b, s, h, dv)


def gated_deltanet(x, w_in, conv_w, a_log, dt_bias, norm_w, w_out):
    b, s, _ = x.shape
    proj = x @ w_in
    o1 = DN_QKV_DIM
    o2 = o1 + DN_VALUE_DIM
    o3 = o2 + DN_VALUE_HEADS
    qkv = proj[..., :o1]
    z = proj[..., o1:o2]
    beta_raw = proj[..., o2:o3]
    a_raw = proj[..., o3:]
    qkv = jax.nn.silu(causal_depthwise_conv(qkv, conv_w)).astype(jnp.float32)
    q = qkv[..., :DN_KEY_DIM].reshape(b, s, DN_KEY_HEADS, DN_HEAD_DIM)
    k = qkv[..., DN_KEY_DIM:2 * DN_KEY_DIM].reshape(b, s, DN_KEY_HEADS, DN_HEAD_DIM)
    v = qkv[..., 2 * DN_KEY_DIM:].reshape(b, s, DN_VALUE_HEADS, DN_HEAD_DIM)
    rep = DN_VALUE_HEADS // DN_KEY_HEADS
    q = jnp.repeat(l2_normalize(q) * (DN_HEAD_DIM ** -0.5), rep, axis=2)
    k = jnp.repeat(l2_normalize(k), rep, axis=2)
    beta = jax.nn.sigmoid(beta_raw.astype(jnp.float32))
    g = -jnp.exp(a_log.astype(jnp.float32)) * jax.nn.softplus(
        a_raw.astype(jnp.float32) + dt_bias.astype(jnp.float32))
    o = chunk_gated_delta_rule(q, k, v, beta, g)
    o = o * lax.rsqrt(jnp.mean(jnp.square(o), -1, keepdims=True) + RMS_EPS) * norm_w.astype(jnp.float32)
    o = o * jax.nn.silu(z.astype(jnp.float32).reshape(b, s, DN_VALUE_HEADS, DN_HEAD_DIM))
    return o.reshape(b, s, DN_VALUE_DIM).astype(x.dtype) @ w_out


def swiglu(x, w_gate_up, w_down):
    gu = x @ w_gate_up
    return (jax.nn.silu(gu[..., :FFN_HIDDEN]) * gu[..., FFN_HIDDEN:]) @ w_down


def setup_inputs(seed: int = 0) -> dict:
    key = jax.random.key(seed)
    ks = jax.random.split(key, 16)

    def normal(k, shape, scale):
        return jax.random.normal(k, shape, jnp.float32) * scale

    dt = jnp.exp(jax.random.uniform(ks[7], (N_DN_LAYERS, DN_VALUE_HEADS), jnp.float32,
                                    np.log(1e-3), np.log(1e-1)))
    return {
        'x': normal(ks[0], (BATCH, SEQ, D_MODEL), 1.0),
        'sc_w_in': normal(ks[1], (N_SC_LAYERS, D_MODEL, 3 * D_MODEL), D_MODEL ** -0.5),
        'sc_conv_w': normal(ks[2], (N_SC_LAYERS, SC_KERNEL, D_MODEL), SC_KERNEL ** -0.5),
        'sc_w_out': normal(ks[3], (N_SC_LAYERS, D_MODEL, D_MODEL), DEEPNORM_BETA * D_MODEL ** -0.5),
        'dn_w_in': normal(ks[4], (N_DN_LAYERS, D_MODEL, DN_PROJ_DIM), D_MODEL ** -0.5),
        'dn_conv_w': normal(ks[5], (N_DN_LAYERS, DN_CONV_KERNEL, DN_QKV_DIM), DN_CONV_KERNEL ** -0.5),
        'dn_a_log': jnp.log(jax.random.uniform(ks[6], (N_DN_LAYERS, DN_VALUE_HEADS), jnp.float32, 1.0, 16.0)),
        'dn_dt_bias': dt + jnp.log(-jnp.expm1(-dt)),
        'dn_norm_w': 1.0 + normal(ks[8], (N_DN_LAYERS, DN_HEAD_DIM), 0.01),
        'dn_w_out': normal(ks[9], (N_DN_LAYERS, DN_VALUE_DIM, D_MODEL), DEEPNORM_BETA * DN_VALUE_DIM ** -0.5),
        'ffn_w_gate_up': normal(ks[10], (DEPTH, D_MODEL, 2 * FFN_HIDDEN), D_MODEL ** -0.5),
        'ffn_w_down': normal(ks[11], (DEPTH, FFN_HIDDEN, D_MODEL), DEEPNORM_BETA * FFN_HIDDEN ** -0.5),
        'ln_gain': 1.0 + normal(ks[12], (DEPTH, 2, D_MODEL), 0.02),
        'ln_bias': normal(ks[13], (DEPTH, 2, D_MODEL), 0.02),
    }


def reference(x, sc_w_in, sc_conv_w, sc_w_out, dn_w_in, dn_conv_w, dn_a_log, dn_dt_bias,
              dn_norm_w, dn_w_out, ffn_w_gate_up, ffn_w_down, ln_gain, ln_bias):
    for i in range(DEPTH):
        j = i // N_MIXERS
        if i % N_MIXERS == 0:
            h = short_conv_mixer(x, sc_w_in[j], sc_conv_w[j], sc_w_out[j])
        else:
            h = gated_deltanet(x, dn_w_in[j], dn_conv_w[j], dn_a_log[j], dn_dt_bias[j],
                               dn_norm_w[j], dn_w_out[j])
        x = layer_norm(DEEPNORM_ALPHA * x + h, ln_gain[i, 0], ln_bias[i, 0])
        x = layer_norm(DEEPNORM_ALPHA * x + swiglu(x, ffn_w_gate_up[i], ffn_w_down[i]),
                       ln_gain[i, 1], ln_bias[i, 1])
    return x
```

```python
import functools

import jax
import jax.numpy as jnp
from jax import lax
from jax.experimental import pallas as pl
from jax.experimental.pallas import tpu as pltpu

HEAD_DIM = 128
CHUNK = 64
SUBLANES = 8
LANES = 128
LN_EPS = 1e-5
RMS_EPS = 1e-6
NEG_BIG = -1e30
VMEM_LIMIT_BYTES = 60 * 1024 * 1024

F32 = jnp.float32
BF16 = jnp.bfloat16


def _cparams(n_axes):
    return pltpu.CompilerParams(
        dimension_semantics=("arbitrary",) * n_axes,
        vmem_limit_bytes=VMEM_LIMIT_BYTES)


def _dot(a, b):
    return jnp.dot(a, b, preferred_element_type=F32)


def _dot_hi(a, b):
    return jnp.dot(a, b, preferred_element_type=F32, precision=lax.Precision.HIGHEST)


def _dot_nt(a, b):
    return lax.dot_general(a, b, (((1,), (1,)), ((), ())), preferred_element_type=F32)


def _dot_tn(a, b):
    return lax.dot_general(a, b, (((0,), (0,)), ((), ())), preferred_element_type=F32)


def _layer_norm(y, gain, bias):
    mu = jnp.mean(y, axis=-1, keepdims=True)
    d = y - mu
    var = jnp.mean(d * d, axis=-1, keepdims=True)
    return d * lax.rsqrt(var + LN_EPS) * gain + bias


def _silu(t):
    return t * jax.nn.sigmoid(t)


def _causal_conv(u, prev, w_ref):
    ksize = w_ref.shape[0]
    out = u * w_ref[ksize - 1:ksize, :]
    head = jnp.concatenate([prev, u[0:SUBLANES]], axis=0)
    for d in range(1, ksize):
        shifted = pltpu.roll(u, d, axis=0)
        top = pltpu.roll(head, d, axis=0)[SUBLANES:2 * SUBLANES]
        shifted = jnp.concatenate([top, shifted[SUBLANES:]], axis=0)
        out = out + shifted * w_ref[ksize - 1 - d:ksize - d, :]
    return out


def _conv_history(carry_ref, j, i, tiles_per_seq):
    prev = carry_ref[j]
    return jnp.where(i % tiles_per_seq == 0, jnp.zeros_like(prev), prev)


def _sc_in_kernel(x_ref, wb_ref, wc_ref, wh_ref, cw_ref, o_ref, xb_ref, carry_ref, *, tiles_per_seq):
    i, j = pl.program_id(0), pl.program_id(1)

    @pl.when(j == 0)
    def _():
        xb_ref[...] = x_ref[...].astype(BF16)

    xb = xb_ref[...]
    u = _dot(xb, wc_ref[...]) * _dot(xb, wh_ref[...])
    prev = _conv_history(carry_ref, j, i, tiles_per_seq)
    y = _causal_conv(u, prev, cw_ref)
    carry_ref[j] = u[u.shape[0] - SUBLANES:]
    o_ref[...] = (_dot(xb, wb_ref[...]) * y).astype(o_ref.dtype)


def _sc_in(x, w_in, conv_w, seq, tm, tn):
    m, d = x.shape
    nd = d // tn
    return pl.pallas_call(
        functools.partial(_sc_in_kernel, tiles_per_seq=seq // tm),
        grid=(m // tm, nd),
        in_specs=[
            pl.BlockSpec((tm, d), lambda i, j: (i, 0)),
            pl.BlockSpec((d, tn), lambda i, j: (0, j)),
            pl.BlockSpec((d, tn), lambda i, j: (0, nd + j)),
            pl.BlockSpec((d, tn), lambda i, j: (0, 2 * nd + j)),
            pl.BlockSpec((conv_w.shape[0], tn), lambda i, j: (0, j)),
        ],
        out_specs=pl.BlockSpec((tm, tn), lambda i, j: (i, j)),
        out_shape=jax.ShapeDtypeStruct((m, d), BF16),
        scratch_shapes=[pltpu.VMEM((tm, d), BF16), pltpu.VMEM((nd, SUBLANES, tn), F32)],
        compiler_params=_cparams(2),
        name="sc_in",
    )(x, w_in, w_in, w_in, conv_w)


def _proj_ln_kernel(a_ref, w_ref, x_ref, gain_ref, bias_ref, o_ref, *, alpha):
    y = alpha * x_ref[...] + _dot(a_ref[...], w_ref[...])
    o_ref[...] = _layer_norm(y, gain_ref[...], bias_ref[...])


def _proj_ln(a, w, x, gain, bias, alpha, tm):
    m, d = x.shape
    k = a.shape[1]
    return pl.pallas_call(
        functools.partial(_proj_ln_kernel, alpha=alpha),
        grid=(m // tm,),
        in_specs=[
            pl.BlockSpec((tm, k), lambda i: (i, 0)),
            pl.BlockSpec((k, d), lambda i: (0, 0), pipeline_mode=pl.Buffered(1)),
            pl.BlockSpec((tm, d), lambda i: (i, 0)),
            pl.BlockSpec((1, d), lambda i: (0, 0)),
            pl.BlockSpec((1, d), lambda i: (0, 0)),
        ],
        out_specs=pl.BlockSpec((tm, d), lambda i: (i, 0)),
        out_shape=jax.ShapeDtypeStruct((m, d), F32),
        compiler_params=_cparams(1),
        name="proj_ln",
    )(a, w, x, gain, bias)


def _ffn_kernel(x_ref, wg_ref, wu_ref, wd_ref, gain_ref, bias_ref, o_ref, xb_ref, acc_ref, *, alpha):
    j = pl.program_id(1)

    @pl.when(j == 0)
    def _():
        xb_ref[...] = x_ref[...].astype(BF16)
        acc_ref[...] = jnp.zeros_like(acc_ref)

    xb = xb_ref[...]
    act = _silu(_dot(xb, wg_ref[...])) * _dot(xb, wu_ref[...])
    acc_ref[...] += _dot(act.astype(BF16), wd_ref[...])

    @pl.when(j == pl.num_programs(1) - 1)
    def _():
        y = alpha * x_ref[...] + acc_ref[...]
        o_ref[...] = _layer_norm(y, gain_ref[...], bias_ref[...])


def _ffn(x, w_gate_up, w_down, gain, bias, alpha, tm, th):
    m, d = x.shape
    hidden = w_down.shape[0]
    nh = hidden // th
    return pl.pallas_call(
        functools.partial(_ffn_kernel, alpha=alpha),
        grid=(m // tm, nh),
        in_specs=[
            pl.BlockSpec((tm, d), lambda i, j: (i, 0)),
            pl.BlockSpec((d, th), lambda i, j: (0, j)),
            pl.BlockSpec((d, th), lambda i, j: (0, nh + j)),
            pl.BlockSpec((th, d), lambda i, j: (j, 0)),
            pl.BlockSpec((1, d), lambda i, j: (0, 0)),
            pl.BlockSpec((1, d), lambda i, j: (0, 0)),
        ],
        out_specs=pl.BlockSpec((tm, d), lambda i, j: (i, 0)),
        out_shape=jax.ShapeDtypeStruct((m, d), F32),
        scratch_shapes=[pltpu.VMEM((tm, d), BF16), pltpu.VMEM((tm, d), F32)],
        compiler_params=_cparams(2),
        name="ffn",
    )(x, w_gate_up, w_gate_up, w_down, gain, bias)


def _dn_in_kernel(x_ref, w_ref, cw_ref, o_ref, xb_ref, carry_ref, *, tiles_per_seq, n_q, n_qk, n_qkv):
    i, j = pl.program_id(0), pl.program_id(1)

    @pl.when(j == 0)
    def _():
        xb_ref[...] = x_ref[...].astype(BF16)

    acc = _dot(xb_ref[...], w_ref[...])

    def conv_silu():
        jc = jnp.minimum(j, n_qkv - 1)
        prev = _conv_history(carry_ref, jc, i, tiles_per_seq)
        carry_ref[jc] = acc[acc.shape[0] - SUBLANES:]
        return _silu(_causal_conv(acc, prev, cw_ref))

    @pl.when(j < n_qk)
    def _():
        t = conv_silu()
        scale = jnp.where(j < n_q, HEAD_DIM ** -0.5, 1.0).astype(F32)
        for s in range(t.shape[1] // HEAD_DIM):
            seg = t[:, s * HEAD_DIM:(s + 1) * HEAD_DIM]
            inv = lax.rsqrt(jnp.sum(seg * seg, axis=-1, keepdims=True) + RMS_EPS) * scale
            o_ref[:, s * HEAD_DIM:(s + 1) * HEAD_DIM] = (seg * inv).astype(o_ref.dtype)

    @pl.when(jnp.logical_and(j >= n_qk, j < n_qkv))
    def _():
        o_ref[...] = conv_silu().astype(o_ref.dtype)

    @pl.when(j >= n_qkv)
    def _():
        o_ref[...] = acc.astype(o_ref.dtype)


def _dn_in(x, w_qkvz, conv_w, seq, key_dim, qkv_dim, tm, tn):
    m, d = x.shape
    n = w_qkvz.shape[1]
    n_qkv = qkv_dim // tn
    return pl.pallas_call(
        functools.partial(_dn_in_kernel, tiles_per_seq=seq // tm, n_q=key_dim // tn,
                          n_qk=2 * key_dim // tn, n_qkv=n_qkv),
        grid=(m // tm, n // tn),
        in_specs=[
            pl.BlockSpec((tm, d), lambda i, j: (i, 0)),
            pl.BlockSpec((d, tn), lambda i, j: (0, j)),
            pl.BlockSpec((conv_w.shape[0], tn), lambda i, j: (0, jnp.minimum(j, n_qkv - 1))),
        ],
        out_specs=pl.BlockSpec((tm, tn), lambda i, j: (i, j)),
        out_shape=jax.ShapeDtypeStruct((m, n), BF16),
        scratch_shapes=[pltpu.VMEM((tm, d), BF16), pltpu.VMEM((n_qkv, SUBLANES, tn), F32)],
        compiler_params=_cparams(2),
        name="dn_in",
    )(x, w_qkvz, conv_w)


def _softplus(t):
    return jnp.maximum(t, 0.0) + jnp.log1p(jnp.exp(-jnp.abs(t)))


def _chunk_cumsum(g, axis):
    pos = lax.broadcasted_iota(jnp.int32, g.shape, axis) % CHUNK
    s = 1
    while s < CHUNK:
        g = g + jnp.where(pos >= s, pltpu.roll(g, s, axis=axis), 0.0)
        s *= 2
    return g


def _gates(raw, a_log, dt_bias, slot_axis, token_axis):
    c = lax.broadcasted_iota(jnp.int32, raw.shape, slot_axis) % SUBLANES
    beta = jax.nn.sigmoid(raw)
    g = -jnp.exp(a_log) * _softplus(raw + dt_bias)
    g = _chunk_cumsum(jnp.where(jnp.logical_and(c >= 2, c < 4), g, 0.0), token_axis)
    return jnp.where(c < 2, beta, g)


def _dn_gate_kernel(x_ref, w_ref, wt_ref, al_ref, dt_ref, alt_ref, dtt_ref, gc_ref, gt_ref):
    xb = x_ref[...].astype(BF16)
    gc_ref[...] = _gates(_dot(xb, w_ref[...]), al_ref[...], dt_ref[...], 1, 0)
    gt_ref[...] = _gates(_dot_nt(wt_ref[...], xb), alt_ref[...], dtt_ref[...], 0, 1)


def _dn_gates(x, w_slot, a_log_slot, dt_slot, tm):
    m, d = x.shape
    ns = w_slot.shape[1]
    return pl.pallas_call(
        _dn_gate_kernel,
        grid=(m // tm,),
        in_specs=[
            pl.BlockSpec((tm, d), lambda i: (i, 0)),
            pl.BlockSpec((d, ns), lambda i: (0, 0)),
            pl.BlockSpec((ns, d), lambda i: (0, 0)),
            pl.BlockSpec((1, ns), lambda i: (0, 0)),
            pl.BlockSpec((1, ns), lambda i: (0, 0)),
            pl.BlockSpec((ns, 1), lambda i: (0, 0)),
            pl.BlockSpec((ns, 1), lambda i: (0, 0)),
        ],
        out_specs=[pl.BlockSpec((tm, ns), lambda i: (i, 0)),
                   pl.BlockSpec((ns, tm), lambda i: (0, i))],
        out_shape=[jax.ShapeDtypeStruct((m, ns), F32), jax.ShapeDtypeStruct((ns, m), F32)],
        compiler_params=_cparams(1),
        name="dn_gates",
    )(x, w_slot, w_slot.T, a_log_slot[None, :], dt_slot[None, :], a_log_slot[:, None], dt_slot[:, None])


def _dn_chunk_kernel(q_ref, k_ref, v_ref, z_ref, gc_ref, gt_ref, nw_ref, o_ref, state_ref, gsel_ref):
    kh, t = pl.program_id(1), pl.program_id(2)
    n_chunks = q_ref.shape[0] // CHUNK

    @pl.when(t == 0)
    def _():
        state_ref[...] = jnp.zeros_like(state_ref)

    gsel_ref[...] = pltpu.roll(gc_ref[...], (LANES - SUBLANES * kh) % LANES, axis=1)

    row = lax.broadcasted_iota(jnp.int32, (CHUNK, LANES), 0)
    lane = lax.broadcasted_iota(jnp.int32, (CHUNK, LANES), 1)
    col = lane % CHUNK
    causal = col <= row
    strict = col < row
    upper_half = lane >= CHUNK
    eye_hi = jnp.where(jnp.logical_and(upper_half, col == row), 1.0, 0.0).astype(F32)
    norm_w = nw_ref[...]

    def chunk_body(c, carry):
        r0 = pl.multiple_of(c * CHUNK, CHUNK)
        rows = pl.ds(r0, CHUNK)
        qc = q_ref[rows, :]
        kc = k_ref[rows, :]
        gsel = gsel_ref[rows, :]
        grow = gt_ref[c]
        qkk = _dot_nt(jnp.concatenate([qc, kc], axis=0), jnp.concatenate([kc, kc], axis=0))
        qk2, kk2 = qkk[:CHUNK], qkk[CHUNK:]
        for e in range(2):
            beta_col = jnp.broadcast_to(gsel[:, e:e + 1], (CHUNK, LANES))
            g_col = jnp.broadcast_to(gsel[:, 2 + e:3 + e], (CHUNK, LANES))
            g_row = grow[2 + e:3 + e, :]
            g_last = g_row[:, CHUNK - 1:CHUNK]
            decay = jnp.exp(jnp.where(causal, g_col - g_row, NEG_BIG))
            x_mat = jnp.where(strict, -(kk2 * beta_col * decay), 0.0)
            w_mat = jnp.where(upper_half, x_mat + eye_hi, _dot_hi(x_mat[:, :CHUNK], x_mat))
            for _ in range(4):
                w_mat = _dot_hi(w_mat[:, :CHUNK], w_mat) + jnp.where(upper_half, w_mat, 0.0)
            t_fac = w_mat[:, CHUNK:]
            t_mat = t_fac + _dot_hi(w_mat[:, :CHUNK], t_fac)
            exp_g = jnp.exp(g_col)
            kf = kc.astype(F32)
            ve = v_ref[rows, e * HEAD_DIM:(e + 1) * HEAD_DIM].astype(F32)
            rhs = jnp.concatenate([ve * beta_col, kf * (beta_col * exp_g)], axis=1)
            sol = _dot(t_mat.astype(BF16), rhs.astype(BF16))
            u_mat, w_dec = sol[:, :HEAD_DIM], sol[:, HEAD_DIM:]
            a_qk = (qk2 * decay)[:, :CHUNK]
            q_dec = qc.astype(F32) * exp_g
            k_tail = kf * jnp.exp(g_last - g_col)
            state = state_ref[e]
            state_b = state.astype(BF16)
            v_new = u_mat - _dot(w_dec.astype(BF16), state_b)
            v_new_b = v_new.astype(BF16)
            o = _dot(q_dec.astype(BF16), state_b) + _dot(a_qk.astype(BF16), v_new_b)
            state_ref[e] = state * jnp.exp(g_last) + _dot_tn(k_tail.astype(BF16), v_new_b)
            o = o * lax.rsqrt(jnp.mean(o * o, axis=-1, keepdims=True) + RMS_EPS) * norm_w
            ze = z_ref[rows, e * HEAD_DIM:(e + 1) * HEAD_DIM].astype(F32)
            o_ref[rows, e * HEAD_DIM:(e + 1) * HEAD_DIM] = (o * _silu(ze)).astype(o_ref.dtype)
        return carry

    lax.fori_loop(0, n_chunks, chunk_body, 0)


def _dn_chunk(qkvz, gate_c, gate_t, norm_w, batch, seq, n_kh, tc):
    m = qkvz.shape[0]
    nt = seq // tc
    cpt = tc // CHUNK
    v_off = n_kh
    z_off = 2 * n_kh
    return pl.pallas_call(
        _dn_chunk_kernel,
        grid=(batch, n_kh, nt),
        in_specs=[
            pl.BlockSpec((tc, HEAD_DIM), lambda b, h, t: (b * nt + t, h)),
            pl.BlockSpec((tc, HEAD_DIM), lambda b, h, t: (b * nt + t, n_kh + h)),
            pl.BlockSpec((tc, 2 * HEAD_DIM), lambda b, h, t: (b * nt + t, v_off + h)),
            pl.BlockSpec((tc, 2 * HEAD_DIM), lambda b, h, t: (b * nt + t, z_off + h)),
            pl.BlockSpec((tc, gate_c.shape[1]), lambda b, h, t: (b * nt + t, 0)),
            pl.BlockSpec((None, cpt, SUBLANES, LANES), lambda b, h, t: (h, b * nt + t, 0, 0)),
            pl.BlockSpec((1, HEAD_DIM), lambda b, h, t: (0, 0)),
        ],
        out_specs=pl.BlockSpec((tc, 2 * HEAD_DIM), lambda b, h, t: (b * nt + t, h)),
        out_shape=jax.ShapeDtypeStruct((m, 2 * n_kh * HEAD_DIM), BF16),
        scratch_shapes=[pltpu.VMEM((2, HEAD_DIM, HEAD_DIM), F32),
                        pltpu.VMEM((tc, gate_c.shape[1]), F32)],
        compiler_params=_cparams(3),
        name="dn_chunk",
    )(qkvz, qkvz, qkvz, qkvz, gate_c, gate_t, norm_w)


def _tile(n, want):
    if n <= want:
        return n
    t = want - want % LANES
    while n % t:
        t -= LANES
    return t


def _gate_slots(w_beta_a, a_log, dt_bias, n_vh):
    n_kh = n_vh // 2
    d = w_beta_a.shape[0]
    wb = w_beta_a[:, :n_vh].reshape(d, n_kh, 2)
    wa = w_beta_a[:, n_vh:].reshape(d, n_kh, 2)
    w_slot = jnp.concatenate([wb, wa, jnp.zeros((d, n_kh, SUBLANES - 4), w_beta_a.dtype)], axis=2)
    zeros2 = jnp.zeros((n_kh, 2), F32)
    zeros4 = jnp.zeros((n_kh, SUBLANES - 4), F32)
    al = jnp.concatenate([zeros2, a_log.astype(F32).reshape(n_kh, 2), zeros4], axis=1)
    dt = jnp.concatenate([zeros2, dt_bias.astype(F32).reshape(n_kh, 2), zeros4], axis=1)
    pad = (-n_kh * SUBLANES) % LANES
    w_slot = jnp.pad(w_slot.reshape(d, n_kh * SUBLANES), ((0, 0), (0, pad)))
    return w_slot, jnp.pad(al.reshape(-1), (0, pad)), jnp.pad(dt.reshape(-1), (0, pad))


def kernel(x, sc_w_in, sc_conv_w, sc_w_out, dn_w_in, dn_conv_w, dn_a_log, dn_dt_bias, dn_norm_w,
           dn_w_out, ffn_w_gate_up, ffn_w_down, ln_gain, ln_bias):
    batch, seq, d = x.shape
    depth = ln_gain.shape[0]
    alpha = float((2 * depth) ** 0.25)
    m = batch * seq
    n_vh = dn_a_log.shape[1]
    n_kh = n_vh // 2
    key_dim = n_kh * HEAD_DIM
    val_dim = n_vh * HEAD_DIM
    qkv_dim = 2 * key_dim + val_dim
    hidden = ffn_w_down.shape[1]
    assert seq % CHUNK == 0 and d % LANES == 0 and dn_norm_w.shape[1] == HEAD_DIM

    tm_in = _tile(seq, 1024)
    tm_out = _tile(seq, 512)
    tn = _tile(d, 512)
    th = _tile(hidden, 512)
    tc = _tile(seq, 1024)

    h = x.reshape(m, d)
    for i in range(depth):
        l = i // 2
        g0, b0 = ln_gain[i, 0][None, :], ln_bias[i, 0][None, :]
        g1, b1 = ln_gain[i, 1][None, :], ln_bias[i, 1][None, :]
        if i % 2 == 0:
            a = _sc_in(h, sc_w_in[l].astype(BF16), sc_conv_w[l], seq, tm_in, tn)
            h = _proj_ln(a, sc_w_out[l].astype(BF16), h, g0, b0, alpha, tm_out)
        else:
            w_in = dn_w_in[l]
            qkvz = _dn_in(h, w_in[:, :qkv_dim + val_dim].astype(BF16), dn_conv_w[l], seq,
                          key_dim, qkv_dim, tm_in, tn)
            w_slot, al_slot, dt_slot = _gate_slots(w_in[:, qkv_dim + val_dim:], dn_a_log[l],
                                                   dn_dt_bias[l], n_vh)
            gate_c, gate_t = _dn_gates(h, w_slot.astype(BF16), al_slot, dt_slot, tm_out)
            gate_t = gate_t[:n_kh * SUBLANES].reshape(n_kh, SUBLANES, m // CHUNK, CHUNK)
            gate_t = jnp.transpose(gate_t, (0, 2, 1, 3))
            gate_t = jnp.concatenate([gate_t, gate_t], axis=-1)
            o = _dn_chunk(qkvz, gate_c, gate_t, dn_norm_w[l][None, :], batch, seq, n_kh, tc)
            h = _proj_ln(o, dn_w_out[l].astype(BF16), h, g0, b0, alpha, tm_out)
        h = _ffn(h, ffn_w_gate_up[i].astype(BF16), ffn_w_down[i].astype(BF16), g1, b1, alpha, tm_out, th)
    return h.reshape(batch, seq, d)
```

```python
import functools

import jax
import jax.numpy as jnp
from jax import lax
from jax.experimental import pallas as pl
from jax.experimental.pallas import tpu as pltpu

HEAD_DIM = 128
CHUNK = 64
SUBLANES = 8
LANES = 128
LN_EPS = 1e-5
RMS_EPS = 1e-6
NEG_BIG = -1e30
VMEM_LIMIT_BYTES = 60 * 1024 * 1024

F32 = jnp.float32
BF16 = jnp.bfloat16


def _cparams(n_axes):
    return pltpu.CompilerParams(
        dimension_semantics=("arbitrary",) * n_axes,
        vmem_limit_bytes=VMEM_LIMIT_BYTES)


def _dot(a, b):
    return jnp.dot(a, b, preferred_element_type=F32)


def _dot_nt(a, b):
    return lax.dot_general(a, b, (((1,), (1,)), ((), ())), preferred_element_type=F32)


def _dot_tn(a, b):
    return lax.dot_general(a, b, (((0,), (0,)), ((), ())), preferred_element_type=F32)


def _layer_norm(y, gain, bias):
    mu = jnp.mean(y, axis=-1, keepdims=True)
    d = y - mu
    var = jnp.mean(d * d, axis=-1, keepdims=True)
    return d * lax.rsqrt(var + LN_EPS) * gain + bias


def _silu(t):
    return t * jax.nn.sigmoid(t)


def _causal_conv(u, prev, w_ref):
    ksize = w_ref.shape[0]
    out = u * w_ref[ksize - 1:ksize, :]
    head = jnp.concatenate([prev, u[0:SUBLANES]], axis=0)
    for d in range(1, ksize):
        shifted = pltpu.roll(u, d, axis=0)
        top = pltpu.roll(head, d, axis=0)[SUBLANES:2 * SUBLANES]
        shifted = jnp.concatenate([top, shifted[SUBLANES:]], axis=0)
        out = out + shifted * w_ref[ksize - 1 - d:ksize - d, :]
    return out


def _conv_history(carry_ref, j, i, tiles_per_seq):
    prev = carry_ref[j]
    return jnp.where(i % tiles_per_seq == 0, jnp.zeros_like(prev), prev)


def _sc_in_kernel(x_ref, wb_ref, wc_ref, wh_ref, cw_ref, o_ref, xb_ref, carry_ref, *, tiles_per_seq):
    i, j = pl.program_id(0), pl.program_id(1)

    @pl.when(j == 0)
    def _():
        xb_ref[...] = x_ref[...].astype(BF16)

    xb = xb_ref[...]
    u = _dot(xb, wc_ref[...]) * _dot(xb, wh_ref[...])
    prev = _conv_history(carry_ref, j, i, tiles_per_seq)
    y = _causal_conv(u, prev, cw_ref)
    carry_ref[j] = u[u.shape[0] - SUBLANES:]
    o_ref[...] = (_dot(xb, wb_ref[...]) * y).astype(o_ref.dtype)


def _sc_in(x, w_in, conv_w, seq, tm, tn):
    m, d = x.shape
    nd = d // tn
    return pl.pallas_call(
        functools.partial(_sc_in_kernel, tiles_per_seq=seq // tm),
        grid=(m // tm, nd),
        in_specs=[
            pl.BlockSpec((tm, d), lambda i, j: (i, 0)),
            pl.BlockSpec((d, tn), lambda i, j: (0, j)),
            pl.BlockSpec((d, tn), lambda i, j: (0, nd + j)),
            pl.BlockSpec((d, tn), lambda i, j: (0, 2 * nd + j)),
            pl.BlockSpec((conv_w.shape[0], tn), lambda i, j: (0, j)),
        ],
        out_specs=pl.BlockSpec((tm, tn), lambda i, j: (i, j)),
        out_shape=jax.ShapeDtypeStruct((m, d), BF16),
        scratch_shapes=[pltpu.VMEM((tm, d), BF16), pltpu.VMEM((nd, SUBLANES, tn), F32)],
        compiler_params=_cparams(2),
        name="sc_in",
    )(x, w_in, w_in, w_in, conv_w)


def _proj_ln_kernel(a_ref, w_ref, x_ref, gain_ref, bias_ref, o_ref, *, alpha):
    y = alpha * x_ref[...] + _dot(a_ref[...], w_ref[...])
    o_ref[...] = _layer_norm(y, gain_ref[...], bias_ref[...])


def _proj_ln(a, w, x, gain, bias, alpha, tm):
    m, d = x.shape
    k = a.shape[1]
    return pl.pallas_call(
        functools.partial(_proj_ln_kernel, alpha=alpha),
        grid=(m // tm,),
        in_specs=[
            pl.BlockSpec((tm, k), lambda i: (i, 0)),
            pl.BlockSpec((k, d), lambda i: (0, 0), pipeline_mode=pl.Buffered(1)),
            pl.BlockSpec((tm, d), lambda i: (i, 0)),
            pl.BlockSpec((1, d), lambda i: (0, 0)),
            pl.BlockSpec((1, d), lambda i: (0, 0)),
        ],
        out_specs=pl.BlockSpec((tm, d), lambda i: (i, 0)),
        out_shape=jax.ShapeDtypeStruct((m, d), F32),
        compiler_params=_cparams(1),
        name="proj_ln",
    )(a, w, x, gain, bias)


def _ffn_kernel(x_ref, wg_ref, wu_ref, wd_ref, gain_ref, bias_ref, o_ref, xb_ref, acc_ref, *, alpha):
    j = pl.program_id(1)

    @pl.when(j == 0)
    def _():
        xb_ref[...] = x_ref[...].astype(BF16)
        acc_ref[...] = jnp.zeros_like(acc_ref)

    xb = xb_ref[...]
    act = _silu(_dot(xb, wg_ref[...])) * _dot(xb, wu_ref[...])
    acc_ref[...] += _dot(act.astype(BF16), wd_ref[...])

    @pl.when(j == pl.num_programs(1) - 1)
    def _():
        y = alpha * x_ref[...] + acc_ref[...]
        o_ref[...] = _layer_norm(y, gain_ref[...], bias_ref[...])


def _ffn(x, w_gate_up, w_down, gain, bias, alpha, tm, th):
    m, d = x.shape
    hidden = w_down.shape[0]
    nh = hidden // th
    return pl.pallas_call(
        functools.partial(_ffn_kernel, alpha=alpha),
        grid=(m // tm, nh),
        in_specs=[
            pl.BlockSpec((tm, d), lambda i, j: (i, 0)),
            pl.BlockSpec((d, th), lambda i, j: (0, j)),
            pl.BlockSpec((d, th), lambda i, j: (0, nh + j)),
            pl.BlockSpec((th, d), lambda i, j: (j, 0)),
            pl.BlockSpec((1, d), lambda i, j: (0, 0)),
            pl.BlockSpec((1, d), lambda i, j: (0, 0)),
        ],
        out_specs=pl.BlockSpec((tm, d), lambda i, j: (i, 0)),
        out_shape=jax.ShapeDtypeStruct((m, d), F32),
        scratch_shapes=[pltpu.VMEM((tm, d), BF16), pltpu.VMEM((tm, d), F32)],
        compiler_params=_cparams(2),
        name="ffn",
    )(x, w_gate_up, w_gate_up, w_down, gain, bias)


def _dn_in_kernel(x_ref, w_ref, cw_ref, o_ref, xb_ref, carry_ref, *, tiles_per_seq, n_q, n_qk, n_qkv):
    i, j = pl.program_id(0), pl.program_id(1)

    @pl.when(j == 0)
    def _():
        xb_ref[...] = x_ref[...].astype(BF16)

    acc = _dot(xb_ref[...], w_ref[...])

    def conv_silu():
        jc = jnp.minimum(j, n_qkv - 1)
        prev = _conv_history(carry_ref, jc, i, tiles_per_seq)
        carry_ref[jc] = acc[acc.shape[0] - SUBLANES:]
        return _silu(_causal_conv(acc, prev, cw_ref))

    @pl.when(j < n_qk)
    def _():
        t = conv_silu()
        scale = jnp.where(j < n_q, HEAD_DIM ** -0.5, 1.0).astype(F32)
        for s in range(t.shape[1] // HEAD_DIM):
            seg = t[:, s * HEAD_DIM:(s + 1) * HEAD_DIM]
            inv = lax.rsqrt(jnp.sum(seg * seg, axis=-1, keepdims=True) + RMS_EPS) * scale
            o_ref[:, s * HEAD_DIM:(s + 1) * HEAD_DIM] = (seg * inv).astype(o_ref.dtype)

    @pl.when(jnp.logical_and(j >= n_qk, j < n_qkv))
    def _():
        o_ref[...] = conv_silu().astype(o_ref.dtype)

    @pl.when(j >= n_qkv)
    def _():
        o_ref[...] = acc.astype(o_ref.dtype)


def _dn_in(x, w_qkvz, conv_w, seq, key_dim, qkv_dim, tm, tn):
    m, d = x.shape
    n = w_qkvz.shape[1]
    n_qkv = qkv_dim // tn
    return pl.pallas_call(
        functools.partial(_dn_in_kernel, tiles_per_seq=seq // tm, n_q=key_dim // tn,
                          n_qk=2 * key_dim // tn, n_qkv=n_qkv),
        grid=(m // tm, n // tn),
        in_specs=[
            pl.BlockSpec((tm, d), lambda i, j: (i, 0)),
            pl.BlockSpec((d, tn), lambda i, j: (0, j)),
            pl.BlockSpec((conv_w.shape[0], tn), lambda i, j: (0, jnp.minimum(j, n_qkv - 1))),
        ],
        out_specs=pl.BlockSpec((tm, tn), lambda i, j: (i, j)),
        out_shape=jax.ShapeDtypeStruct((m, n), BF16),
        scratch_shapes=[pltpu.VMEM((tm, d), BF16), pltpu.VMEM((n_qkv, SUBLANES, tn), F32)],
        compiler_params=_cparams(2),
        name="dn_in",
    )(x, w_qkvz, conv_w)


def _softplus(t):
    return jnp.maximum(t, 0.0) + jnp.log1p(jnp.exp(-jnp.abs(t)))


def _chunk_cumsum(g, axis):
    pos = lax.broadcasted_iota(jnp.int32, g.shape, axis) % CHUNK
    s = 1
    while s < CHUNK:
        g = g + jnp.where(pos >= s, pltpu.roll(g, s, axis=axis), 0.0)
        s *= 2
    return g


def _gates(raw, a_log, dt_bias, slot_axis, token_axis):
    c = lax.broadcasted_iota(jnp.int32, raw.shape, slot_axis) % SUBLANES
    beta = jax.nn.sigmoid(raw)
    g = -jnp.exp(a_log) * _softplus(raw + dt_bias)
    g = _chunk_cumsum(jnp.where(jnp.logical_and(c >= 2, c < 4), g, 0.0), token_axis)
    return jnp.where(c < 2, beta, g)


def _dn_gate_kernel(x_ref, w_ref, wt_ref, al_ref, dt_ref, alt_ref, dtt_ref, gc_ref, gt_ref):
    xb = x_ref[...].astype(BF16)
    gc_ref[...] = _gates(_dot(xb, w_ref[...]), al_ref[...], dt_ref[...], 1, 0)
    gt_ref[...] = _gates(_dot_nt(wt_ref[...], xb), alt_ref[...], dtt_ref[...], 0, 1)


def _dn_gates(x, w_slot, a_log_slot, dt_slot, tm):
    m, d = x.shape
    ns = w_slot.shape[1]
    return pl.pallas_call(
        _dn_gate_kernel,
        grid=(m // tm,),
        in_specs=[
            pl.BlockSpec((tm, d), lambda i: (i, 0)),
            pl.BlockSpec((d, ns), lambda i: (0, 0)),
            pl.BlockSpec((ns, d), lambda i: (0, 0)),
            pl.BlockSpec((1, ns), lambda i: (0, 0)),
            pl.BlockSpec((1, ns), lambda i: (0, 0)),
            pl.BlockSpec((ns, 1), lambda i: (0, 0)),
            pl.BlockSpec((ns, 1), lambda i: (0, 0)),
        ],
        out_specs=[pl.BlockSpec((tm, ns), lambda i: (i, 0)),
                   pl.BlockSpec((ns, tm), lambda i: (0, i))],
        out_shape=[jax.ShapeDtypeStruct((m, ns), F32), jax.ShapeDtypeStruct((ns, m), F32)],
        compiler_params=_cparams(1),
        name="dn_gates",
    )(x, w_slot, w_slot.T, a_log_slot[None, :], dt_slot[None, :], a_log_slot[:, None], dt_slot[:, None])


def _dn_chunk_kernel(q_ref, k_ref, v_ref, z_ref, gc_ref, gt_ref, nw_ref, o_ref,
                     state_ref, qm_ref, on_ref, cd_ref):
    kh, s = pl.program_id(1), pl.program_id(2)
    n_chunks = q_ref.shape[0] // CHUNK
    slot_a = s % 2
    slot_b = 1 - slot_a

    @pl.when(s == 0)
    def _():
        state_ref[...] = jnp.zeros_like(state_ref)
        qm_ref[1] = jnp.zeros(qm_ref.shape[1:], qm_ref.dtype)
        on_ref[1] = jnp.zeros(on_ref.shape[1:], on_ref.dtype)
        cd_ref[1] = jnp.zeros(cd_ref.shape[1:], cd_ref.dtype)

    row = lax.broadcasted_iota(jnp.int32, (CHUNK, LANES), 0)
    lane = lax.broadcasted_iota(jnp.int32, (CHUNK, LANES), 1)
    col = lane % CHUNK
    causal = col <= row
    strict = col < row
    upper_half = lane >= CHUNK
    eye_hi = jnp.where(jnp.logical_and(upper_half, col == row), 1.0, 0.0).astype(F32)
    norm_w = nw_ref[...]
    states = [state_ref[0], state_ref[1]]
    chains = [(c, e) for c in range(n_chunks) for e in range(2)]

    def recurrence(c):
        rows = slice(c * CHUNK, (c + 1) * CHUNK)
        for e in range(2):
            qms = _dot(qm_ref[slot_b, c, e], states[e].astype(BF16))
            o_n = on_ref[slot_b, c, e]
            o = qms[:CHUNK] + o_n[:CHUNK]
            states[e] = states[e] * cd_ref[slot_b, c, e][:1, :] + o_n[CHUNK:] - qms[CHUNK:]
            o = o * lax.rsqrt(jnp.mean(o * o, axis=-1, keepdims=True) + RMS_EPS) * norm_w
            ze = z_ref[rows, e * HEAD_DIM:(e + 1) * HEAD_DIM].astype(F32)
            o_ref[rows, e * HEAD_DIM:(e + 1) * HEAD_DIM] = (o * _silu(ze)).astype(o_ref.dtype)

    gsel_all = pltpu.roll(gc_ref[...], (LANES - SUBLANES * kh) % LANES, axis=1)
    kf, qf, k_t, qk2, kk2 = [], [], [], [], []
    for c in range(n_chunks):
        rows = slice(c * CHUNK, (c + 1) * CHUNK)
        qc, kc = q_ref[rows, :], k_ref[rows, :]
        qf.append(qc.astype(F32))
        kf.append(kc.astype(F32))
        k_t.append(jnp.transpose(kf[c]).astype(BF16))
        qkk = _dot_nt(jnp.concatenate([qc, kc], axis=0), jnp.concatenate([kc, kc], axis=0))
        qk2.append(qkk[:CHUNK])
        kk2.append(qkk[CHUNK:])
    recurrence(0)

    beta_col, g_col, g_last, decay, w_mat = {}, {}, {}, {}, {}
    for (c, e) in chains:
        gsel = gsel_all[c * CHUNK:(c + 1) * CHUNK]
        g_row = gt_ref[c][2 + e:3 + e, :]
        beta_col[c, e] = jnp.broadcast_to(gsel[:, e:e + 1], (CHUNK, LANES))
        g_col[c, e] = jnp.broadcast_to(gsel[:, 2 + e:3 + e], (CHUNK, LANES))
        g_last[c, e] = g_row[:, CHUNK - 1:CHUNK]
        decay[c, e] = jnp.exp(jnp.where(causal, g_col[c, e] - g_row, NEG_BIG))
        x_mat = jnp.where(strict, -(kk2[c] * beta_col[c, e] * decay[c, e]), 0.0)
        x_b = x_mat.astype(BF16)
        w_mat[c, e] = jnp.where(upper_half, x_mat + eye_hi, _dot(x_b[:, :CHUNK], x_b))
    recurrence(1)
    for level in range(4):
        for ce in chains:
            w_b = w_mat[ce].astype(BF16)
            w_mat[ce] = _dot(w_b[:, :CHUNK], w_b) + jnp.where(upper_half, w_mat[ce], 0.0)
        recurrence(2 + level)
    sol = {}
    for (c, e) in chains:
        t_fac = w_mat[c, e][:, CHUNK:]
        t_mat = t_fac + _dot(w_mat[c, e][:, :CHUNK].astype(BF16), t_fac.astype(BF16))
        ve = v_ref[c * CHUNK:(c + 1) * CHUNK, e * HEAD_DIM:(e + 1) * HEAD_DIM].astype(F32)
        exp_g = jnp.exp(g_col[c, e])
        rhs = jnp.concatenate([kf[c] * (beta_col[c, e] * exp_g), ve * beta_col[c, e]], axis=1)
        w_mat[c, e] = (t_mat, rhs, exp_g)
    recurrence(6)
    for ce in chains:
        t_mat, rhs, _ = w_mat[ce]
        sol[ce] = _dot(t_mat.astype(BF16), rhs.astype(BF16))
    recurrence(7)
    for (c, e) in chains:
        a_qk = (qk2[c] * decay[c, e])[:, :CHUNK].astype(BF16)
        qo = _dot(a_qk, sol[c, e].astype(BF16))
        q_eff = qf[c] * w_mat[c, e][2] - qo[:, :HEAD_DIM]
        k_tail = jnp.exp(g_last[c, e] - g_col[c, e])[:, :1]
        mn = _dot(k_t[c], (sol[c, e] * k_tail).astype(BF16))
        qm_ref[slot_a, c, e] = jnp.concatenate([q_eff, mn[:, :HEAD_DIM]], axis=0).astype(BF16)
        on_ref[slot_a, c, e] = jnp.concatenate([qo[:, HEAD_DIM:], mn[:, HEAD_DIM:]], axis=0)
        cd_ref[slot_a, c, e] = jnp.broadcast_to(jnp.exp(g_last[c, e]), (SUBLANES, LANES))

    state_ref[0] = states[0]
    state_ref[1] = states[1]


def _dn_chunk(qkvz, gate_c, gate_t, norm_w, batch, seq, n_kh, tc):
    m = qkvz.shape[0]
    nt = seq // tc
    cpt = tc // CHUNK
    assert cpt == 8, "the recurrence steps are interleaved with 8 preparation levels"
    v_off = n_kh
    z_off = 2 * n_kh

    def prep(b, s):
        return b * nt + jnp.minimum(s, nt - 1)

    def rec(b, s):
        return b * nt + jnp.maximum(s - 1, 0)

    return pl.pallas_call(
        _dn_chunk_kernel,
        grid=(batch, n_kh, nt + 1),
        in_specs=[
            pl.BlockSpec((tc, HEAD_DIM), lambda b, h, s: (prep(b, s), h)),
            pl.BlockSpec((tc, HEAD_DIM), lambda b, h, s: (prep(b, s), n_kh + h)),
            pl.BlockSpec((tc, 2 * HEAD_DIM), lambda b, h, s: (prep(b, s), v_off + h)),
            pl.BlockSpec((tc, 2 * HEAD_DIM), lambda b, h, s: (rec(b, s), z_off + h)),
            pl.BlockSpec((tc, gate_c.shape[1]), lambda b, h, s: (prep(b, s), 0)),
            pl.BlockSpec((None, cpt, SUBLANES, LANES), lambda b, h, s: (h, prep(b, s), 0, 0)),
            pl.BlockSpec((1, HEAD_DIM), lambda b, h, s: (0, 0)),
        ],
        out_specs=pl.BlockSpec((tc, 2 * HEAD_DIM), lambda b, h, s: (rec(b, s), h)),
        out_shape=jax.ShapeDtypeStruct((m, 2 * n_kh * HEAD_DIM), BF16),
        scratch_shapes=[pltpu.VMEM((2, HEAD_DIM, HEAD_DIM), F32),
                        pltpu.VMEM((2, cpt, 2, CHUNK + HEAD_DIM, HEAD_DIM), BF16),
                        pltpu.VMEM((2, cpt, 2, CHUNK + HEAD_DIM, HEAD_DIM), F32),
                        pltpu.VMEM((2, cpt, 2, SUBLANES, LANES), F32)],
        compiler_params=_cparams(3),
        name="dn_chunk",
    )(qkvz, qkvz, qkvz, qkvz, gate_c, gate_t, norm_w)


def _tile(n, want):
    if n <= want:
        return n
    t = want - want % LANES
    while n % t:
        t -= LANES
    return t


def _gate_slots(w_beta_a, a_log, dt_bias, n_vh):
    n_kh = n_vh // 2
    d = w_beta_a.shape[0]
    wb = w_beta_a[:, :n_vh].reshape(d, n_kh, 2)
    wa = w_beta_a[:, n_vh:].reshape(d, n_kh, 2)
    w_slot = jnp.concatenate([wb, wa, jnp.zeros((d, n_kh, SUBLANES - 4), w_beta_a.dtype)], axis=2)
    zeros2 = jnp.zeros((n_kh, 2), F32)
    zeros4 = jnp.zeros((n_kh, SUBLANES - 4), F32)
    al = jnp.concatenate([zeros2, a_log.astype(F32).reshape(n_kh, 2), zeros4], axis=1)
    dt = jnp.concatenate([zeros2, dt_bias.astype(F32).reshape(n_kh, 2), zeros4], axis=1)
    pad = (-n_kh * SUBLANES) % LANES
    w_slot = jnp.pad(w_slot.reshape(d, n_kh * SUBLANES), ((0, 0), (0, pad)))
    return w_slot, jnp.pad(al.reshape(-1), (0, pad)), jnp.pad(dt.reshape(-1), (0, pad))


def kernel(x, sc_w_in, sc_conv_w, sc_w_out, dn_w_in, dn_conv_w, dn_a_log, dn_dt_bias, dn_norm_w,
           dn_w_out, ffn_w_gate_up, ffn_w_down, ln_gain, ln_bias):
    batch, seq, d = x.shape
    depth = ln_gain.shape[0]
    alpha = float((2 * depth) ** 0.25)
    m = batch * seq
    n_vh = dn_a_log.shape[1]
    n_kh = n_vh // 2
    key_dim = n_kh * HEAD_DIM
    val_dim = n_vh * HEAD_DIM
    qkv_dim = 2 * key_dim + val_dim
    hidden = ffn_w_down.shape[1]
    assert seq % CHUNK == 0 and d % LANES == 0 and dn_norm_w.shape[1] == HEAD_DIM

    tm_in = _tile(seq, 1024)
    tm_out = _tile(seq, 512)
    tn = _tile(d, 512)
    th = _tile(hidden, 512)
    tc = _tile(seq, 512)

    h = x.reshape(m, d)
    for i in range(depth):
        l = i // 2
        g0, b0 = ln_gain[i, 0][None, :], ln_bias[i, 0][None, :]
        g1, b1 = ln_gain[i, 1][None, :], ln_bias[i, 1][None, :]
        if i % 2 == 0:
            a = _sc_in(h, sc_w_in[l].astype(BF16), sc_conv_w[l], seq, tm_in, tn)
            h = _proj_ln(a, sc_w_out[l].astype(BF16), h, g0, b0, alpha, tm_out)
        else:
            w_in = dn_w_in[l]
            qkvz = _dn_in(h, w_in[:, :qkv_dim + val_dim].astype(BF16), dn_conv_w[l], seq,
                          key_dim, qkv_dim, tm_in, tn)
            w_slot, al_slot, dt_slot = _gate_slots(w_in[:, qkv_dim + val_dim:], dn_a_log[l],
                                                   dn_dt_bias[l], n_vh)
            gate_c, gate_t = _dn_gates(h, w_slot.astype(BF16), al_slot, dt_slot, tm_out)
            gate_t = gate_t[:n_kh * SUBLANES].reshape(n_kh, SUBLANES, m // CHUNK, CHUNK)
            gate_t = jnp.transpose(gate_t, (0, 2, 1, 3))
            gate_t = jnp.concatenate([gate_t, gate_t], axis=-1)
            o = _dn_chunk(qkvz, gate_c, gate_t, dn_norm_w[l][None, :], batch, seq, n_kh, tc)
            h = _proj_ln(o, dn_w_out[l].astype(BF16), h, g0, b0, alpha, tm_out)
        h = _ffn(h, ffn_w_gate_up[i].astype(BF16), ffn_w_down[i].astype(BF16), g1, b1, alpha, tm_out, th)
    return h.reshape(batch, seq, d)
```

```python
import functools

import jax
import jax.numpy as jnp
from jax import lax
from jax.experimental import pallas as pl
from jax.experimental.pallas import tpu as pltpu

HEAD_DIM = 128
CHUNK = 64
SUBLANES = 8
LANES = 128
LN_EPS = 1e-5
RMS_EPS = 1e-6
NEG_BIG = -1e30
VMEM_LIMIT_BYTES = 60 * 1024 * 1024

F32 = jnp.float32
BF16 = jnp.bfloat16


def _cparams(n_axes):
    return pltpu.CompilerParams(
        dimension_semantics=("arbitrary",) * n_axes,
        vmem_limit_bytes=VMEM_LIMIT_BYTES)


def _dot(a, b):
    return jnp.dot(a, b, preferred_element_type=F32)


def _dot_nt(a, b):
    return lax.dot_general(a, b, (((1,), (1,)), ((), ())), preferred_element_type=F32)


def _dot_tn(a, b):
    return lax.dot_general(a, b, (((0,), (0,)), ((), ())), preferred_element_type=F32)


def _layer_norm(y, gain, bias):
    mu = jnp.mean(y, axis=-1, keepdims=True)
    d = y - mu
    var = jnp.mean(d * d, axis=-1, keepdims=True)
    return d * lax.rsqrt(var + LN_EPS) * gain + bias


def _silu(t):
    return t * jax.nn.sigmoid(t)


def _causal_conv(u, prev, w_ref):
    ksize = w_ref.shape[0]
    out = u * w_ref[ksize - 1:ksize, :]
    head = jnp.concatenate([prev, u[0:SUBLANES]], axis=0)
    for d in range(1, ksize):
        shifted = pltpu.roll(u, d, axis=0)
        top = pltpu.roll(head, d, axis=0)[SUBLANES:2 * SUBLANES]
        shifted = jnp.concatenate([top, shifted[SUBLANES:]], axis=0)
        out = out + shifted * w_ref[ksize - 1 - d:ksize - d, :]
    return out


def _conv_history(carry_ref, j, i, tiles_per_seq):
    prev = carry_ref[j]
    return jnp.where(i % tiles_per_seq == 0, jnp.zeros_like(prev), prev)


def _sc_in_kernel(x_ref, wb_ref, wc_ref, wh_ref, cw_ref, o_ref, xb_ref, carry_ref, *, tiles_per_seq):
    i, j = pl.program_id(0), pl.program_id(1)

    @pl.when(j == 0)
    def _():
        xb_ref[...] = x_ref[...].astype(BF16)

    xb = xb_ref[...]
    u = _dot(xb, wc_ref[...]) * _dot(xb, wh_ref[...])
    prev = _conv_history(carry_ref, j, i, tiles_per_seq)
    y = _causal_conv(u, prev, cw_ref)
    carry_ref[j] = u[u.shape[0] - SUBLANES:]
    o_ref[...] = (_dot(xb, wb_ref[...]) * y).astype(o_ref.dtype)


def _sc_in(x, w_in, conv_w, seq, tm, tn):
    m, d = x.shape
    nd = d // tn
    return pl.pallas_call(
        functools.partial(_sc_in_kernel, tiles_per_seq=seq // tm),
        grid=(m // tm, nd),
        in_specs=[
            pl.BlockSpec((tm, d), lambda i, j: (i, 0)),
            pl.BlockSpec((d, tn), lambda i, j: (0, j)),
            pl.BlockSpec((d, tn), lambda i, j: (0, nd + j)),
            pl.BlockSpec((d, tn), lambda i, j: (0, 2 * nd + j)),
            pl.BlockSpec((conv_w.shape[0], tn), lambda i, j: (0, j)),
        ],
        out_specs=pl.BlockSpec((tm, tn), lambda i, j: (i, j)),
        out_shape=jax.ShapeDtypeStruct((m, d), BF16),
        scratch_shapes=[pltpu.VMEM((tm, d), BF16), pltpu.VMEM((nd, SUBLANES, tn), F32)],
        compiler_params=_cparams(2),
        name="sc_in",
    )(x, w_in, w_in, w_in, conv_w)


def _proj_ln_kernel(a_ref, w_ref, x_ref, gain_ref, bias_ref, o_ref, *, alpha):
    y = alpha * x_ref[...] + _dot(a_ref[...], w_ref[...])
    o_ref[...] = _layer_norm(y, gain_ref[...], bias_ref[...])


def _proj_ln(a, w, x, gain, bias, alpha, tm):
    m, d = x.shape
    k = a.shape[1]
    return pl.pallas_call(
        functools.partial(_proj_ln_kernel, alpha=alpha),
        grid=(m // tm,),
        in_specs=[
            pl.BlockSpec((tm, k), lambda i: (i, 0)),
            pl.BlockSpec((k, d), lambda i: (0, 0), pipeline_mode=pl.Buffered(1)),
            pl.BlockSpec((tm, d), lambda i: (i, 0)),
            pl.BlockSpec((1, d), lambda i: (0, 0)),
            pl.BlockSpec((1, d), lambda i: (0, 0)),
        ],
        out_specs=pl.BlockSpec((tm, d), lambda i: (i, 0)),
        out_shape=jax.ShapeDtypeStruct((m, d), F32),
        compiler_params=_cparams(1),
        name="proj_ln",
    )(a, w, x, gain, bias)


def _ffn_kernel(x_ref, wg_ref, wu_ref, wd_ref, gain_ref, bias_ref, o_ref, xb_ref, acc_ref, *, alpha):
    j = pl.program_id(1)

    @pl.when(j == 0)
    def _():
        xb_ref[...] = x_ref[...].astype(BF16)
        acc_ref[...] = jnp.zeros_like(acc_ref)

    xb = xb_ref[...]
    act = _silu(_dot(xb, wg_ref[...])) * _dot(xb, wu_ref[...])
    acc_ref[...] += _dot(act.astype(BF16), wd_ref[...])

    @pl.when(j == pl.num_programs(1) - 1)
    def _():
        y = alpha * x_ref[...] + acc_ref[...]
        o_ref[...] = _layer_norm(y, gain_ref[...], bias_ref[...])


def _ffn(x, w_gate_up, w_down, gain, bias, alpha, tm, th):
    m, d = x.shape
    hidden = w_down.shape[0]
    nh = hidden // th
    return pl.pallas_call(
        functools.partial(_ffn_kernel, alpha=alpha),
        grid=(m // tm, nh),
        in_specs=[
            pl.BlockSpec((tm, d), lambda i, j: (i, 0)),
            pl.BlockSpec((d, th), lambda i, j: (0, j)),
            pl.BlockSpec((d, th), lambda i, j: (0, nh + j)),
            pl.BlockSpec((th, d), lambda i, j: (j, 0)),
            pl.BlockSpec((1, d), lambda i, j: (0, 0)),
            pl.BlockSpec((1, d), lambda i, j: (0, 0)),
        ],
        out_specs=pl.BlockSpec((tm, d), lambda i, j: (i, 0)),
        out_shape=jax.ShapeDtypeStruct((m, d), F32),
        scratch_shapes=[pltpu.VMEM((tm, d), BF16), pltpu.VMEM((tm, d), F32)],
        compiler_params=_cparams(2),
        name="ffn",
    )(x, w_gate_up, w_gate_up, w_down, gain, bias)


def _dn_in_kernel(x_ref, w_ref, cw_ref, o_ref, xb_ref, acc_ref, carry_ref, *,
                  tiles_per_seq, n_q, n_qk, n_qkv, n_sub):
    i, j = pl.program_id(0), pl.program_id(1)
    tm = o_ref.shape[0]
    rs = tm // n_sub
    ksize = cw_ref.shape[0]

    @pl.when(j == 0)
    def _():
        xb_ref[...] = x_ref[...].astype(BF16)

    def block_dot(b):
        return _dot(xb_ref[b * rs:(b + 1) * rs, :], w_ref[...])

    def conv_tile(l2norm):
        @pl.when(i % tiles_per_seq == 0)
        def _():
            acc_ref[0:SUBLANES, :] = jnp.zeros((SUBLANES, acc_ref.shape[1]), F32)

        @pl.when(i % tiles_per_seq != 0)
        def _():
            acc_ref[0:SUBLANES, :] = carry_ref[j]

        scale = jnp.where(j < n_q, HEAD_DIM ** -0.5, 1.0).astype(F32)

        def epilogue(b):
            r0 = SUBLANES + b * rs
            t = acc_ref[r0:r0 + rs, :] * cw_ref[ksize - 1:ksize, :]
            for d in range(1, ksize):
                t = t + acc_ref[r0 - d:r0 - d + rs, :] * cw_ref[ksize - 1 - d:ksize - d, :]
            t = _silu(t)
            if not l2norm:
                o_ref[b * rs:(b + 1) * rs, :] = t.astype(o_ref.dtype)
                return
            for s in range(t.shape[1] // HEAD_DIM):
                seg = t[:, s * HEAD_DIM:(s + 1) * HEAD_DIM]
                inv = lax.rsqrt(jnp.sum(seg * seg, axis=-1, keepdims=True) + RMS_EPS) * scale
                o_ref[b * rs:(b + 1) * rs, s * HEAD_DIM:(s + 1) * HEAD_DIM] = (seg * inv).astype(o_ref.dtype)

        for b in range(n_sub):
            acc_ref[SUBLANES + b * rs:SUBLANES + (b + 1) * rs, :] = block_dot(b)
            if b > 0:
                epilogue(b - 1)
        epilogue(n_sub - 1)
        carry_ref[j] = acc_ref[tm:tm + SUBLANES, :]

    @pl.when(j < n_qk)
    def _():
        conv_tile(True)

    @pl.when(jnp.logical_and(j >= n_qk, j < n_qkv))
    def _():
        conv_tile(False)

    @pl.when(j >= n_qkv)
    def _():
        for b in range(n_sub):
            o_ref[b * rs:(b + 1) * rs, :] = block_dot(b).astype(o_ref.dtype)


def _dn_in(x, w_qkvz, conv_w, seq, key_dim, qkv_dim, tm, tn):
    m, d = x.shape
    n = w_qkvz.shape[1]
    n_qkv = qkv_dim // tn
    return pl.pallas_call(
        functools.partial(_dn_in_kernel, tiles_per_seq=seq // tm, n_q=key_dim // tn,
                          n_qk=2 * key_dim // tn, n_qkv=n_qkv, n_sub=max(1, tm // 128)),
        grid=(m // tm, n // tn),
        in_specs=[
            pl.BlockSpec((tm, d), lambda i, j: (i, 0)),
            pl.BlockSpec((d, tn), lambda i, j: (0, j)),
            pl.BlockSpec((conv_w.shape[0], tn), lambda i, j: (0, jnp.minimum(j, n_qkv - 1))),
        ],
        out_specs=pl.BlockSpec((tm, tn), lambda i, j: (i, j)),
        out_shape=jax.ShapeDtypeStruct((m, n), BF16),
        scratch_shapes=[pltpu.VMEM((tm, d), BF16), pltpu.VMEM((tm + SUBLANES, tn), F32),
                        pltpu.VMEM((n_qkv, SUBLANES, tn), F32)],
        compiler_params=_cparams(2),
        name="dn_in",
    )(x, w_qkvz, conv_w)


def _softplus(t):
    return jnp.maximum(t, 0.0) + jnp.log1p(jnp.exp(-jnp.abs(t)))


def _chunk_cumsum(g, axis):
    pos = lax.broadcasted_iota(jnp.int32, g.shape, axis) % CHUNK
    s = 1
    while s < CHUNK:
        g = g + jnp.where(pos >= s, pltpu.roll(g, s, axis=axis), 0.0)
        s *= 2
    return g


def _gates(raw, a_log, dt_bias, slot_axis, token_axis):
    c = lax.broadcasted_iota(jnp.int32, raw.shape, slot_axis) % SUBLANES
    beta = jax.nn.sigmoid(raw)
    g = -jnp.exp(a_log) * _softplus(raw + dt_bias)
    g = _chunk_cumsum(jnp.where(jnp.logical_and(c >= 2, c < 4), g, 0.0), token_axis)
    return jnp.where(c < 2, beta, g)


def _dn_gate_kernel(x_ref, w_ref, wt_ref, al_ref, dt_ref, alt_ref, dtt_ref, gc_ref, gt_ref):
    xb = x_ref[...].astype(BF16)
    gc_ref[...] = _gates(_dot(xb, w_ref[...]), al_ref[...], dt_ref[...], 1, 0)
    gt_ref[...] = _gates(_dot_nt(wt_ref[...], xb), alt_ref[...], dtt_ref[...], 0, 1)


def _dn_gates(x, w_slot, a_log_slot, dt_slot, tm):
    m, d = x.shape
    ns = w_slot.shape[1]
    return pl.pallas_call(
        _dn_gate_kernel,
        grid=(m // tm,),
        in_specs=[
            pl.BlockSpec((tm, d), lambda i: (i, 0)),
            pl.BlockSpec((d, ns), lambda i: (0, 0)),
            pl.BlockSpec((ns, d), lambda i: (0, 0)),
            pl.BlockSpec((1, ns), lambda i: (0, 0)),
            pl.BlockSpec((1, ns), lambda i: (0, 0)),
            pl.BlockSpec((ns, 1), lambda i: (0, 0)),
            pl.BlockSpec((ns, 1), lambda i: (0, 0)),
        ],
        out_specs=[pl.BlockSpec((tm, ns), lambda i: (i, 0)),
                   pl.BlockSpec((ns, tm), lambda i: (0, i))],
        out_shape=[jax.ShapeDtypeStruct((m, ns), F32), jax.ShapeDtypeStruct((ns, m), F32)],
        compiler_params=_cparams(1),
        name="dn_gates",
    )(x, w_slot, w_slot.T, a_log_slot[None, :], dt_slot[None, :], a_log_slot[:, None], dt_slot[:, None])


def _dn_chunk_kernel(q_ref, k_ref, v_ref, z_ref, gc_ref, gt_ref, nw_ref, o_ref,
                     state_ref, qm_ref, on_ref, cd_ref):
    kh, s = pl.program_id(1), pl.program_id(2)
    n_chunks = q_ref.shape[0] // CHUNK
    slot_a = s % 2
    slot_b = 1 - slot_a

    @pl.when(s == 0)
    def _():
        state_ref[...] = jnp.zeros_like(state_ref)
        qm_ref[1] = jnp.zeros(qm_ref.shape[1:], qm_ref.dtype)
        on_ref[1] = jnp.zeros(on_ref.shape[1:], on_ref.dtype)
        cd_ref[1] = jnp.zeros(cd_ref.shape[1:], cd_ref.dtype)

    row = lax.broadcasted_iota(jnp.int32, (CHUNK, LANES), 0)
    lane = lax.broadcasted_iota(jnp.int32, (CHUNK, LANES), 1)
    col = lane % CHUNK
    causal = col <= row
    strict = col < row
    upper_half = lane >= CHUNK
    eye_hi = jnp.where(jnp.logical_and(upper_half, col == row), 1.0, 0.0).astype(F32)
    norm_w = nw_ref[...]
    states = [state_ref[0], state_ref[1]]
    chains = [(c, e) for c in range(n_chunks) for e in range(2)]

    def recurrence(c):
        rows = slice(c * CHUNK, (c + 1) * CHUNK)
        for e in range(2):
            qms = _dot(qm_ref[slot_b, c, e], states[e].astype(BF16))
            o_n = on_ref[slot_b, c, e]
            o = qms[:CHUNK] + o_n[:CHUNK]
            states[e] = states[e] * cd_ref[slot_b, c, e][:1, :] + o_n[CHUNK:] - qms[CHUNK:]
            o = o * lax.rsqrt(jnp.mean(o * o, axis=-1, keepdims=True) + RMS_EPS) * norm_w
            ze = z_ref[rows, e * HEAD_DIM:(e + 1) * HEAD_DIM].astype(F32)
            o_ref[rows, e * HEAD_DIM:(e + 1) * HEAD_DIM] = (o * _silu(ze)).astype(o_ref.dtype)

    gsel_all = pltpu.roll(gc_ref[...], (LANES - SUBLANES * kh) % LANES, axis=1)
    kf, qf, k_t, qk2, kk2 = [], [], [], [], []
    for c in range(n_chunks):
        rows = slice(c * CHUNK, (c + 1) * CHUNK)
        qc, kc = q_ref[rows, :], k_ref[rows, :]
        qf.append(qc.astype(F32))
        kf.append(kc.astype(F32))
        k_t.append(jnp.transpose(kf[c]).astype(BF16))
        qkk = _dot_nt(jnp.concatenate([qc, kc], axis=0), jnp.concatenate([kc, kc], axis=0))
        qk2.append(qkk[:CHUNK])
        kk2.append(qkk[CHUNK:])
    recurrence(0)

    beta_col, g_col, g_last, decay, w_mat = {}, {}, {}, {}, {}
    for (c, e) in chains:
        gsel = gsel_all[c * CHUNK:(c + 1) * CHUNK]
        g_row = gt_ref[c][2 + e:3 + e, :]
        beta_col[c, e] = jnp.broadcast_to(gsel[:, e:e + 1], (CHUNK, LANES))
        g_col[c, e] = jnp.broadcast_to(gsel[:, 2 + e:3 + e], (CHUNK, LANES))
        g_last[c, e] = g_row[:, CHUNK - 1:CHUNK]
        decay[c, e] = jnp.exp(jnp.where(causal, g_col[c, e] - g_row, NEG_BIG))
        x_mat = jnp.where(strict, -(kk2[c] * beta_col[c, e] * decay[c, e]), 0.0)
        x_b = x_mat.astype(BF16)
        w_mat[c, e] = jnp.where(upper_half, x_mat + eye_hi, _dot(x_b[:, :CHUNK], x_b))
    recurrence(1)
    for level in range(4):
        for ce in chains:
            w_b = w_mat[ce].astype(BF16)
            w_mat[ce] = _dot(w_b[:, :CHUNK], w_b) + jnp.where(upper_half, w_mat[ce], 0.0)
        recurrence(2 + level)
    sol = {}
    for (c, e) in chains:
        t_fac = w_mat[c, e][:, CHUNK:]
        t_mat = t_fac + _dot(w_mat[c, e][:, :CHUNK].astype(BF16), t_fac.astype(BF16))
        ve = v_ref[c * CHUNK:(c + 1) * CHUNK, e * HEAD_DIM:(e + 1) * HEAD_DIM].astype(F32)
        exp_g = jnp.exp(g_col[c, e])
        rhs = jnp.concatenate([kf[c] * (beta_col[c, e] * exp_g), ve * beta_col[c, e]], axis=1)
        w_mat[c, e] = (t_mat, rhs, exp_g)
    recurrence(6)
    for ce in chains:
        t_mat, rhs, _ = w_mat[ce]
        sol[ce] = _dot(t_mat.astype(BF16), rhs.astype(BF16))
    recurrence(7)
    for (c, e) in chains:
        a_qk = (qk2[c] * decay[c, e])[:, :CHUNK].astype(BF16)
        qo = _dot(a_qk, sol[c, e].astype(BF16))
        q_eff = qf[c] * w_mat[c, e][2] - qo[:, :HEAD_DIM]
        k_tail = jnp.exp(g_last[c, e] - g_col[c, e])[:, :1]
        mn = _dot(k_t[c], (sol[c, e] * k_tail).astype(BF16))
        qm_ref[slot_a, c, e] = jnp.concatenate([q_eff, mn[:, :HEAD_DIM]], axis=0).astype(BF16)
        on_ref[slot_a, c, e] = jnp.concatenate([qo[:, HEAD_DIM:], mn[:, HEAD_DIM:]], axis=0)
        cd_ref[slot_a, c, e] = jnp.broadcast_to(jnp.exp(g_last[c, e]), (SUBLANES, LANES))

    state_ref[0] = states[0]
    state_ref[1] = states[1]


def _dn_chunk(qkvz, gate_c, gate_t, norm_w, batch, seq, n_kh, tc):
    m = qkvz.shape[0]
    nt = seq // tc
    cpt = tc // CHUNK
    assert cpt == 8, "the recurrence steps are interleaved with 8 preparation levels"
    v_off = n_kh
    z_off = 2 * n_kh

    def prep(b, s):
        return b * nt + jnp.minimum(s, nt - 1)

    def rec(b, s):
        return b * nt + jnp.maximum(s - 1, 0)

    return pl.pallas_call(
        _dn_chunk_kernel,
        grid=(batch, n_kh, nt + 1),
        in_specs=[
            pl.BlockSpec((tc, HEAD_DIM), lambda b, h, s: (prep(b, s), h)),
            pl.BlockSpec((tc, HEAD_DIM), lambda b, h, s: (prep(b, s), n_kh + h)),
            pl.BlockSpec((tc, 2 * HEAD_DIM), lambda b, h, s: (prep(b, s), v_off + h)),
            pl.BlockSpec((tc, 2 * HEAD_DIM), lambda b, h, s: (rec(b, s), z_off + h)),
            pl.BlockSpec((tc, gate_c.shape[1]), lambda b, h, s: (prep(b, s), 0)),
            pl.BlockSpec((None, cpt, SUBLANES, LANES), lambda b, h, s: (h, prep(b, s), 0, 0)),
            pl.BlockSpec((1, HEAD_DIM), lambda b, h, s: (0, 0)),
        ],
        out_specs=pl.BlockSpec((tc, 2 * HEAD_DIM), lambda b, h, s: (rec(b, s), h)),
        out_shape=jax.ShapeDtypeStruct((m, 2 * n_kh * HEAD_DIM), BF16),
        scratch_shapes=[pltpu.VMEM((2, HEAD_DIM, HEAD_DIM), F32),
                        pltpu.VMEM((2, cpt, 2, CHUNK + HEAD_DIM, HEAD_DIM), BF16),
                        pltpu.VMEM((2, cpt, 2, CHUNK + HEAD_DIM, HEAD_DIM), F32),
                        pltpu.VMEM((2, cpt, 2, SUBLANES, LANES), F32)],
        compiler_params=_cparams(3),
        name="dn_chunk",
    )(qkvz, qkvz, qkvz, qkvz, gate_c, gate_t, norm_w)


def _tile(n, want):
    if n <= want:
        return n
    t = want - want % LANES
    while n % t:
        t -= LANES
    return t


def _gate_slots(w_beta_a, a_log, dt_bias, n_vh):
    n_kh = n_vh // 2
    d = w_beta_a.shape[0]
    wb = w_beta_a[:, :n_vh].reshape(d, n_kh, 2)
    wa = w_beta_a[:, n_vh:].reshape(d, n_kh, 2)
    w_slot = jnp.concatenate([wb, wa, jnp.zeros((d, n_kh, SUBLANES - 4), w_beta_a.dtype)], axis=2)
    zeros2 = jnp.zeros((n_kh, 2), F32)
    zeros4 = jnp.zeros((n_kh, SUBLANES - 4), F32)
    al = jnp.concatenate([zeros2, a_log.astype(F32).reshape(n_kh, 2), zeros4], axis=1)
    dt = jnp.concatenate([zeros2, dt_bias.astype(F32).reshape(n_kh, 2), zeros4], axis=1)
    pad = (-n_kh * SUBLANES) % LANES
    w_slot = jnp.pad(w_slot.reshape(d, n_kh * SUBLANES), ((0, 0), (0, pad)))
    return w_slot, jnp.pad(al.reshape(-1), (0, pad)), jnp.pad(dt.reshape(-1), (0, pad))


def kernel(x, sc_w_in, sc_conv_w, sc_w_out, dn_w_in, dn_conv_w, dn_a_log, dn_dt_bias, dn_norm_w,
           dn_w_out, ffn_w_gate_up, ffn_w_down, ln_gain, ln_bias):
    batch, seq, d = x.shape
    depth = ln_gain.shape[0]
    alpha = float((2 * depth) ** 0.25)
    m = batch * seq
    n_vh = dn_a_log.shape[1]
    n_kh = n_vh // 2
    key_dim = n_kh * HEAD_DIM
    val_dim = n_vh * HEAD_DIM
    qkv_dim = 2 * key_dim + val_dim
    hidden = ffn_w_down.shape[1]
    assert seq % CHUNK == 0 and d % LANES == 0 and dn_norm_w.shape[1] == HEAD_DIM

    tm_in = _tile(seq, 1024)
    tm_out = _tile(seq, 512)
    tn = _tile(d, 512)
    th = _tile(hidden, 512)
    tc = _tile(seq, 512)

    h = x.reshape(m, d)
    for i in range(depth):
        l = i // 2
        g0, b0 = ln_gain[i, 0][None, :], ln_bias[i, 0][None, :]
        g1, b1 = ln_gain[i, 1][None, :], ln_bias[i, 1][None, :]
        if i % 2 == 0:
            a = _sc_in(h, sc_w_in[l].astype(BF16), sc_conv_w[l], seq, tm_in, tn)
            h = _proj_ln(a, sc_w_out[l].astype(BF16), h, g0, b0, alpha, tm_out)
        else:
            w_in = dn_w_in[l]
            qkvz = _dn_in(h, w_in[:, :qkv_dim + val_dim].astype(BF16), dn_conv_w[l], seq,
                          key_dim, qkv_dim, tm_in, tn)
            w_slot, al_slot, dt_slot = _gate_slots(w_in[:, qkv_dim + val_dim:], dn_a_log[l],
                                                   dn_dt_bias[l], n_vh)
            gate_c, gate_t = _dn_gates(h, w_slot.astype(BF16), al_slot, dt_slot, tm_out)
            gate_t = gate_t[:n_kh * SUBLANES].reshape(n_kh, SUBLANES, m // CHUNK, CHUNK)
            gate_t = jnp.transpose(gate_t, (0, 2, 1, 3))
            gate_t = jnp.concatenate([gate_t, gate_t], axis=-1)
            o = _dn_chunk(qkvz, gate_c, gate_t, dn_norm_w[l][None, :], batch, seq, n_kh, tc)
            h = _proj_ln(o, dn_w_out[l].astype(BF16), h, g0, b0, alpha, tm_out)
        h = _ffn(h, ffn_w_gate_up[i].astype(BF16), ffn_w_down[i].astype(BF16), g1, b1, alpha, tm_out, th)
    return h.reshape(batch, seq, d)
```

```python
import functools

import jax
import jax.numpy as jnp
from jax import lax
from jax.experimental import pallas as pl
from jax.experimental.pallas import tpu as pltpu

HEAD_DIM = 128
CHUNK = 64
SUBLANES = 8
LANES = 128
LN_EPS = 1e-5
RMS_EPS = 1e-6
NEG_BIG = -1e30
VMEM_LIMIT_BYTES = 60 * 1024 * 1024

F32 = jnp.float32
BF16 = jnp.bfloat16


def _cparams(n_axes):
    return pltpu.CompilerParams(
        dimension_semantics=("arbitrary",) * n_axes,
        vmem_limit_bytes=VMEM_LIMIT_BYTES)


def _dot(a, b):
    return jnp.dot(a, b, preferred_element_type=F32)


def _dot_nt(a, b):
    return lax.dot_general(a, b, (((1,), (1,)), ((), ())), preferred_element_type=F32)


def _dot_tn(a, b):
    return lax.dot_general(a, b, (((0,), (0,)), ((), ())), preferred_element_type=F32)


def _layer_norm(y, gain, bias):
    mu = jnp.mean(y, axis=-1, keepdims=True)
    d = y - mu
    var = jnp.mean(d * d, axis=-1, keepdims=True)
    return d * lax.rsqrt(var + LN_EPS) * gain + bias


def _silu(t):
    return t * jax.nn.sigmoid(t)


def _causal_conv(u, prev, w_ref):
    ksize = w_ref.shape[0]
    out = u * w_ref[ksize - 1:ksize, :]
    head = jnp.concatenate([prev, u[0:SUBLANES]], axis=0)
    for d in range(1, ksize):
        shifted = pltpu.roll(u, d, axis=0)
        top = pltpu.roll(head, d, axis=0)[SUBLANES:2 * SUBLANES]
        shifted = jnp.concatenate([top, shifted[SUBLANES:]], axis=0)
        out = out + shifted * w_ref[ksize - 1 - d:ksize - d, :]
    return out


def _conv_history(carry_ref, j, i, tiles_per_seq):
    prev = carry_ref[j]
    return jnp.where(i % tiles_per_seq == 0, jnp.zeros_like(prev), prev)


def _sc_in_kernel(xb_ref, wb_ref, wc_ref, wh_ref, cw_ref, o_ref, carry_ref, *, tiles_per_seq):
    i, j = pl.program_id(0), pl.program_id(1)
    xb = xb_ref[...]
    u = _dot(xb, wc_ref[...]) * _dot(xb, wh_ref[...])
    prev = _conv_history(carry_ref, j, i, tiles_per_seq)
    y = _causal_conv(u, prev, cw_ref)
    carry_ref[j] = u[u.shape[0] - SUBLANES:]
    o_ref[...] = (_dot(xb, wb_ref[...]) * y).astype(o_ref.dtype)


def _sc_in(xb, w_in, conv_w, seq, tm, tn):
    m, d = xb.shape
    nd = d // tn
    return pl.pallas_call(
        functools.partial(_sc_in_kernel, tiles_per_seq=seq // tm),
        grid=(m // tm, nd),
        in_specs=[
            pl.BlockSpec((tm, d), lambda i, j: (i, 0)),
            pl.BlockSpec((d, tn), lambda i, j: (0, j)),
            pl.BlockSpec((d, tn), lambda i, j: (0, nd + j)),
            pl.BlockSpec((d, tn), lambda i, j: (0, 2 * nd + j)),
            pl.BlockSpec((conv_w.shape[0], tn), lambda i, j: (0, j)),
        ],
        out_specs=pl.BlockSpec((tm, tn), lambda i, j: (i, j)),
        out_shape=jax.ShapeDtypeStruct((m, d), BF16),
        scratch_shapes=[pltpu.VMEM((nd, SUBLANES, tn), F32)],
        compiler_params=_cparams(2),
        name="sc_in",
    )(xb, w_in, w_in, w_in, conv_w)


def _proj_ln_kernel(a_ref, w_ref, x_ref, gain_ref, bias_ref, o_ref, ob_ref, *, alpha):
    y = alpha * x_ref[...] + _dot(a_ref[...], w_ref[...])
    out = _layer_norm(y, gain_ref[...], bias_ref[...])
    o_ref[...] = out
    ob_ref[...] = out.astype(BF16)


def _proj_ln(a, w, x, gain, bias, alpha, tm):
    m, d = x.shape
    k = a.shape[1]
    return pl.pallas_call(
        functools.partial(_proj_ln_kernel, alpha=alpha),
        grid=(m // tm,),
        in_specs=[
            pl.BlockSpec((tm, k), lambda i: (i, 0)),
            pl.BlockSpec((k, d), lambda i: (0, 0), pipeline_mode=pl.Buffered(1)),
            pl.BlockSpec((tm, d), lambda i: (i, 0)),
            pl.BlockSpec((1, d), lambda i: (0, 0)),
            pl.BlockSpec((1, d), lambda i: (0, 0)),
        ],
        out_specs=[pl.BlockSpec((tm, d), lambda i: (i, 0)), pl.BlockSpec((tm, d), lambda i: (i, 0))],
        out_shape=[jax.ShapeDtypeStruct((m, d), F32), jax.ShapeDtypeStruct((m, d), BF16)],
        compiler_params=_cparams(1),
        name="proj_ln",
    )(a, w, x, gain, bias)


def _ffn_kernel(x_ref, xb_ref, wg_ref, wu_ref, wd_ref, gain_ref, bias_ref, o_ref, ob_ref, acc_ref, *, alpha):
    j = pl.program_id(1)

    @pl.when(j == 0)
    def _():
        acc_ref[...] = jnp.zeros_like(acc_ref)

    xb = xb_ref[...]
    act = _silu(_dot(xb, wg_ref[...])) * _dot(xb, wu_ref[...])
    acc_ref[...] += _dot(act.astype(BF16), wd_ref[...])

    @pl.when(j == pl.num_programs(1) - 1)
    def _():
        y = alpha * x_ref[...] + acc_ref[...]
        out = _layer_norm(y, gain_ref[...], bias_ref[...])
        o_ref[...] = out
        ob_ref[...] = out.astype(BF16)


def _ffn(x, xb, w_gate_up, w_down, gain, bias, alpha, tm, th):
    m, d = x.shape
    hidden = w_down.shape[0]
    nh = hidden // th
    return pl.pallas_call(
        functools.partial(_ffn_kernel, alpha=alpha),
        grid=(m // tm, nh),
        in_specs=[
            pl.BlockSpec((tm, d), lambda i, j: (i, 0)),
            pl.BlockSpec((tm, d), lambda i, j: (i, 0)),
            pl.BlockSpec((d, th), lambda i, j: (0, j)),
            pl.BlockSpec((d, th), lambda i, j: (0, nh + j)),
            pl.BlockSpec((th, d), lambda i, j: (j, 0)),
            pl.BlockSpec((1, d), lambda i, j: (0, 0)),
            pl.BlockSpec((1, d), lambda i, j: (0, 0)),
        ],
        out_specs=[pl.BlockSpec((tm, d), lambda i, j: (i, 0)), pl.BlockSpec((tm, d), lambda i, j: (i, 0))],
        out_shape=[jax.ShapeDtypeStruct((m, d), F32), jax.ShapeDtypeStruct((m, d), BF16)],
        scratch_shapes=[pltpu.VMEM((tm, d), F32)],
        compiler_params=_cparams(2),
        name="ffn",
    )(x, xb, w_gate_up, w_gate_up, w_down, gain, bias)


def _dn_in_kernel(xb_ref, w_ref, cw_ref, o_ref, *scratch, tiles_per_seq, conv, l2norm, n_scaled, n_sub):
    i, j = pl.program_id(0), pl.program_id(1)
    tm = o_ref.shape[0]
    rs = tm // n_sub

    def block_dot(b):
        return _dot(xb_ref[b * rs:(b + 1) * rs, :], w_ref[...])

    if not conv:
        for b in range(n_sub):
            o_ref[b * rs:(b + 1) * rs, :] = block_dot(b).astype(o_ref.dtype)
        return

    acc_ref, carry_ref = scratch
    ksize = cw_ref.shape[0]

    @pl.when(i % tiles_per_seq == 0)
    def _():
        acc_ref[0:SUBLANES, :] = jnp.zeros((SUBLANES, acc_ref.shape[1]), F32)

    @pl.when(i % tiles_per_seq != 0)
    def _():
        acc_ref[0:SUBLANES, :] = carry_ref[j]

    scale = jnp.where(j < n_scaled, HEAD_DIM ** -0.5, 1.0).astype(F32)

    def epilogue(b):
        r0 = SUBLANES + b * rs
        t = acc_ref[r0:r0 + rs, :] * cw_ref[ksize - 1:ksize, :]
        for d in range(1, ksize):
            t = t + acc_ref[r0 - d:r0 - d + rs, :] * cw_ref[ksize - 1 - d:ksize - d, :]
        t = _silu(t)
        if not l2norm:
            o_ref[b * rs:(b + 1) * rs, :] = t.astype(o_ref.dtype)
            return
        for s in range(t.shape[1] // HEAD_DIM):
            seg = t[:, s * HEAD_DIM:(s + 1) * HEAD_DIM]
            inv = lax.rsqrt(jnp.sum(seg * seg, axis=-1, keepdims=True) + RMS_EPS) * scale
            o_ref[b * rs:(b + 1) * rs, s * HEAD_DIM:(s + 1) * HEAD_DIM] = (seg * inv).astype(o_ref.dtype)

    for b in range(n_sub):
        acc_ref[SUBLANES + b * rs:SUBLANES + (b + 1) * rs, :] = block_dot(b)
        if b > 0:
            epilogue(b - 1)
    epilogue(n_sub - 1)
    carry_ref[j] = acc_ref[tm:tm + SUBLANES, :]


def _dn_in(xb, w_in, conv_w, col0, width, seq, tm, tn, *, conv, l2norm=False, n_scaled=0, name):
    m, d = xb.shape
    nj = width // tn
    j0 = col0 // tn
    scratch = [pltpu.VMEM((tm + SUBLANES, tn), F32), pltpu.VMEM((nj, SUBLANES, tn), F32)] if conv else []
    cw_j0 = j0 if conv else 0
    return pl.pallas_call(
        functools.partial(_dn_in_kernel, tiles_per_seq=seq // tm, conv=conv, l2norm=l2norm,
                          n_scaled=n_scaled, n_sub=max(1, tm // 128)),
        grid=(m // tm, nj),
        in_specs=[
            pl.BlockSpec((tm, d), lambda i, j: (i, 0)),
            pl.BlockSpec((d, tn), lambda i, j: (0, j0 + j)),
            pl.BlockSpec((conv_w.shape[0], tn), lambda i, j: (0, cw_j0 + j if conv else 0)),
        ],
        out_specs=pl.BlockSpec((tm, tn), lambda i, j: (i, j)),
        out_shape=jax.ShapeDtypeStruct((m, width), BF16),
        scratch_shapes=scratch,
        compiler_params=_cparams(2),
        name=name,
    )(xb, w_in, conv_w)


def _softplus(t):
    return jnp.maximum(t, 0.0) + jnp.log1p(jnp.exp(-jnp.abs(t)))


def _chunk_cumsum(g, axis):
    pos = lax.broadcasted_iota(jnp.int32, g.shape, axis) % CHUNK
    s = 1
    while s < CHUNK:
        g = g + jnp.where(pos >= s, pltpu.roll(g, s, axis=axis), 0.0)
        s *= 2
    return g


def _gates(raw, a_log, dt_bias, slot_axis, token_axis):
    c = lax.broadcasted_iota(jnp.int32, raw.shape, slot_axis) % SUBLANES
    beta = jax.nn.sigmoid(raw)
    g = -jnp.exp(a_log) * _softplus(raw + dt_bias)
    g = _chunk_cumsum(jnp.where(jnp.logical_and(c >= 2, c < 4), g, 0.0), token_axis)
    return jnp.where(c < 2, beta, g)


def _dn_gate_kernel(x_ref, w_ref, wt_ref, al_ref, dt_ref, alt_ref, dtt_ref, gc_ref, gt_ref):
    xb = x_ref[...]
    gc_ref[...] = _gates(_dot(xb, w_ref[...]), al_ref[...], dt_ref[...], 1, 0)
    gt_ref[...] = _gates(_dot_nt(wt_ref[...], xb), alt_ref[...], dtt_ref[...], 0, 1)


def _dn_gates(x, w_slot, a_log_slot, dt_slot, tm):
    m, d = x.shape
    ns = w_slot.shape[1]
    return pl.pallas_call(
        _dn_gate_kernel,
        grid=(m // tm,),
        in_specs=[
            pl.BlockSpec((tm, d), lambda i: (i, 0)),
            pl.BlockSpec((d, ns), lambda i: (0, 0)),
            pl.BlockSpec((ns, d), lambda i: (0, 0)),
            pl.BlockSpec((1, ns), lambda i: (0, 0)),
            pl.BlockSpec((1, ns), lambda i: (0, 0)),
            pl.BlockSpec((ns, 1), lambda i: (0, 0)),
            pl.BlockSpec((ns, 1), lambda i: (0, 0)),
        ],
        out_specs=[pl.BlockSpec((tm, ns), lambda i: (i, 0)),
                   pl.BlockSpec((ns, tm), lambda i: (0, i))],
        out_shape=[jax.ShapeDtypeStruct((m, ns), F32), jax.ShapeDtypeStruct((ns, m), F32)],
        compiler_params=_cparams(1),
        name="dn_gates",
    )(x, w_slot, w_slot.T, a_log_slot[None, :], dt_slot[None, :], a_log_slot[:, None], dt_slot[:, None])


def _dn_chunk_kernel(q_ref, k_ref, v_ref, z_ref, gc_ref, gt_ref, nw_ref, o_ref,
                     state_ref, qm_ref, on_ref, cd_ref):
    kh, s = pl.program_id(1), pl.program_id(2)
    n_chunks = q_ref.shape[0] // CHUNK
    slot_a = s % 2
    slot_b = 1 - slot_a

    @pl.when(s == 0)
    def _():
        state_ref[...] = jnp.zeros_like(state_ref)
        qm_ref[1] = jnp.zeros(qm_ref.shape[1:], qm_ref.dtype)
        on_ref[1] = jnp.zeros(on_ref.shape[1:], on_ref.dtype)
        cd_ref[1] = jnp.zeros(cd_ref.shape[1:], cd_ref.dtype)

    row = lax.broadcasted_iota(jnp.int32, (CHUNK, LANES), 0)
    lane = lax.broadcasted_iota(jnp.int32, (CHUNK, LANES), 1)
    col = lane % CHUNK
    causal = col <= row
    strict = col < row
    upper_half = lane >= CHUNK
    eye_hi = jnp.where(jnp.logical_and(upper_half, col == row), 1.0, 0.0).astype(F32)
    norm_w = nw_ref[...]
    states = [state_ref[0], state_ref[1]]
    chains = [(c, e) for c in range(n_chunks) for e in range(2)]

    def recurrence(c):
        rows = slice(c * CHUNK, (c + 1) * CHUNK)
        for e in range(2):
            qms = _dot(qm_ref[slot_b, c, e], states[e].astype(BF16))
            o_n = on_ref[slot_b, c, e]
            o = qms[:CHUNK] + o_n[:CHUNK]
            states[e] = states[e] * cd_ref[slot_b, c, e][:1, :] + o_n[CHUNK:] - qms[CHUNK:]
            o = o * lax.rsqrt(jnp.mean(o * o, axis=-1, keepdims=True) + RMS_EPS) * norm_w
            ze = z_ref[rows, e * HEAD_DIM:(e + 1) * HEAD_DIM].astype(F32)
            o_ref[rows, e * HEAD_DIM:(e + 1) * HEAD_DIM] = (o * _silu(ze)).astype(o_ref.dtype)

    gsel_all = pltpu.roll(gc_ref[...], (LANES - SUBLANES * kh) % LANES, axis=1)
    kf, qf, k_t, qk2, kk2 = [], [], [], [], []
    for c in range(n_chunks):
        rows = slice(c * CHUNK, (c + 1) * CHUNK)
        qc, kc = q_ref[rows, :], k_ref[rows, :]
        qf.append(qc.astype(F32))
        kf.append(kc.astype(F32))
        k_t.append(jnp.transpose(kf[c]).astype(BF16))
        qkk = _dot_nt(jnp.concatenate([qc, kc], axis=0), jnp.concatenate([kc, kc], axis=0))
        qk2.append(qkk[:CHUNK])
        kk2.append(qkk[CHUNK:])
    recurrence(0)

    beta_col, g_col, g_last, decay, w_mat = {}, {}, {}, {}, {}
    for (c, e) in chains:
        gsel = gsel_all[c * CHUNK:(c + 1) * CHUNK]
        g_row = gt_ref[c][2 + e:3 + e, :]
        beta_col[c, e] = jnp.broadcast_to(gsel[:, e:e + 1], (CHUNK, LANES))
        g_col[c, e] = jnp.broadcast_to(gsel[:, 2 + e:3 + e], (CHUNK, LANES))
        g_last[c, e] = g_row[:, CHUNK - 1:CHUNK]
        decay[c, e] = jnp.exp(jnp.where(causal, g_col[c, e] - g_row, NEG_BIG))
        x_mat = jnp.where(strict, -(kk2[c] * beta_col[c, e] * decay[c, e]), 0.0)
        x_b = x_mat.astype(BF16)
        w_mat[c, e] = jnp.where(upper_half, x_mat + eye_hi, _dot(x_b[:, :CHUNK], x_b))
    recurrence(1)
    for level in range(4):
        for ce in chains:
            w_b = w_mat[ce].astype(BF16)
            w_mat[ce] = _dot(w_b[:, :CHUNK], w_b) + jnp.where(upper_half, w_mat[ce], 0.0)
        recurrence(2 + level)
    sol = {}
    for (c, e) in chains:
        t_fac = w_mat[c, e][:, CHUNK:]
        t_mat = t_fac + _dot(w_mat[c, e][:, :CHUNK].astype(BF16), t_fac.astype(BF16))
        ve = v_ref[c * CHUNK:(c + 1) * CHUNK, e * HEAD_DIM:(e + 1) * HEAD_DIM].astype(F32)
        exp_g = jnp.exp(g_col[c, e])
        rhs = jnp.concatenate([kf[c] * (beta_col[c, e] * exp_g), ve * beta_col[c, e]], axis=1)
        w_mat[c, e] = (t_mat, rhs, exp_g)
    recurrence(6)
    for ce in chains:
        t_mat, rhs, _ = w_mat[ce]
        sol[ce] = _dot(t_mat.astype(BF16), rhs.astype(BF16))
    recurrence(7)
    for (c, e) in chains:
        a_qk = (qk2[c] * decay[c, e])[:, :CHUNK].astype(BF16)
        qo = _dot(a_qk, sol[c, e].astype(BF16))
        q_eff = qf[c] * w_mat[c, e][2] - qo[:, :HEAD_DIM]
        k_tail = jnp.exp(g_last[c, e] - g_col[c, e])[:, :1]
        mn = _dot(k_t[c], (sol[c, e] * k_tail).astype(BF16))
        qm_ref[slot_a, c, e] = jnp.concatenate([q_eff, mn[:, :HEAD_DIM]], axis=0).astype(BF16)
        on_ref[slot_a, c, e] = jnp.concatenate([qo[:, HEAD_DIM:], mn[:, HEAD_DIM:]], axis=0)
        cd_ref[slot_a, c, e] = jnp.broadcast_to(jnp.exp(g_last[c, e]), (SUBLANES, LANES))

    state_ref[0] = states[0]
    state_ref[1] = states[1]


def _dn_chunk(qk, v, z, gate_c, gate_t, norm_w, batch, seq, n_kh, tc):
    m = qk.shape[0]
    nt = seq // tc
    cpt = tc // CHUNK
    assert cpt == 8, "the recurrence steps are interleaved with 8 preparation levels"

    def prep(b, s):
        return b * nt + jnp.minimum(s, nt - 1)

    def rec(b, s):
        return b * nt + jnp.maximum(s - 1, 0)

    return pl.pallas_call(
        _dn_chunk_kernel,
        grid=(batch, n_kh, nt + 1),
        in_specs=[
            pl.BlockSpec((tc, HEAD_DIM), lambda b, h, s: (prep(b, s), h)),
            pl.BlockSpec((tc, HEAD_DIM), lambda b, h, s: (prep(b, s), n_kh + h)),
            pl.BlockSpec((tc, 2 * HEAD_DIM), lambda b, h, s: (prep(b, s), h)),
            pl.BlockSpec((tc, 2 * HEAD_DIM), lambda b, h, s: (rec(b, s), h)),
            pl.BlockSpec((tc, gate_c.shape[1]), lambda b, h, s: (prep(b, s), 0)),
            pl.BlockSpec((None, cpt, SUBLANES, LANES), lambda b, h, s: (h, prep(b, s), 0, 0)),
            pl.BlockSpec((1, HEAD_DIM), lambda b, h, s: (0, 0)),
        ],
        out_specs=pl.BlockSpec((tc, 2 * HEAD_DIM), lambda b, h, s: (rec(b, s), h)),
        out_shape=jax.ShapeDtypeStruct((m, 2 * n_kh * HEAD_DIM), BF16),
        scratch_shapes=[pltpu.VMEM((2, HEAD_DIM, HEAD_DIM), F32),
                        pltpu.VMEM((2, cpt, 2, CHUNK + HEAD_DIM, HEAD_DIM), BF16),
                        pltpu.VMEM((2, cpt, 2, CHUNK + HEAD_DIM, HEAD_DIM), F32),
                        pltpu.VMEM((2, cpt, 2, SUBLANES, LANES), F32)],
        compiler_params=_cparams(3),
        name="dn_chunk",
    )(qk, qk, v, z, gate_c, gate_t, norm_w)


def _tile(n, want):
    if n <= want:
        return n
    t = want - want % LANES
    while n % t:
        t -= LANES
    return t


def _gate_slots(w_beta_a, a_log, dt_bias, n_vh):
    n_kh = n_vh // 2
    d = w_beta_a.shape[0]
    wb = w_beta_a[:, :n_vh].reshape(d, n_kh, 2)
    wa = w_beta_a[:, n_vh:].reshape(d, n_kh, 2)
    w_slot = jnp.concatenate([wb, wa, jnp.zeros((d, n_kh, SUBLANES - 4), w_beta_a.dtype)], axis=2)
    zeros2 = jnp.zeros((n_kh, 2), F32)
    zeros4 = jnp.zeros((n_kh, SUBLANES - 4), F32)
    al = jnp.concatenate([zeros2, a_log.astype(F32).reshape(n_kh, 2), zeros4], axis=1)
    dt = jnp.concatenate([zeros2, dt_bias.astype(F32).reshape(n_kh, 2), zeros4], axis=1)
    pad = (-n_kh * SUBLANES) % LANES
    w_slot = jnp.pad(w_slot.reshape(d, n_kh * SUBLANES), ((0, 0), (0, pad)))
    return w_slot, jnp.pad(al.reshape(-1), (0, pad)), jnp.pad(dt.reshape(-1), (0, pad))


def kernel(x, sc_w_in, sc_conv_w, sc_w_out, dn_w_in, dn_conv_w, dn_a_log, dn_dt_bias, dn_norm_w,
           dn_w_out, ffn_w_gate_up, ffn_w_down, ln_gain, ln_bias):
    batch, seq, d = x.shape
    depth = ln_gain.shape[0]
    alpha = float((2 * depth) ** 0.25)
    m = batch * seq
    n_vh = dn_a_log.shape[1]
    n_kh = n_vh // 2
    key_dim = n_kh * HEAD_DIM
    val_dim = n_vh * HEAD_DIM
    qkv_dim = 2 * key_dim + val_dim
    hidden = ffn_w_down.shape[1]
    assert seq % CHUNK == 0 and d % LANES == 0 and dn_norm_w.shape[1] == HEAD_DIM

    tm_in = _tile(seq, 1024)
    tm_out = _tile(seq, 512)
    tn = _tile(d, 512)
    th = _tile(hidden, 512)
    tc = _tile(seq, 512)

    h = x.reshape(m, d)
    hb = h.astype(BF16)
    for i in range(depth):
        l = i // 2
        g0, b0 = ln_gain[i, 0][None, :], ln_bias[i, 0][None, :]
        g1, b1 = ln_gain[i, 1][None, :], ln_bias[i, 1][None, :]
        if i % 2 == 0:
            a = _sc_in(hb, sc_w_in[l].astype(BF16), sc_conv_w[l], seq, tm_in, tn)
            h, hb = _proj_ln(a, sc_w_out[l].astype(BF16), h, g0, b0, alpha, tm_out)
        else:
            w_in = dn_w_in[l]
            w_qkvz = w_in[:, :qkv_dim + val_dim].astype(BF16)
            cw = dn_conv_w[l]
            qk = _dn_in(hb, w_qkvz, cw, 0, 2 * key_dim, seq, tm_in, tn, conv=True, l2norm=True,
                        n_scaled=key_dim // tn, name="dn_in_qk")
            v = _dn_in(hb, w_qkvz, cw, 2 * key_dim, val_dim, seq, tm_in, tn, conv=True, name="dn_in_v")
            z = _dn_in(hb, w_qkvz, cw, qkv_dim, val_dim, seq, tm_in, tn, conv=False, name="dn_in_z")
            w_slot, al_slot, dt_slot = _gate_slots(w_in[:, qkv_dim + val_dim:], dn_a_log[l],
                                                   dn_dt_bias[l], n_vh)
            gate_c, gate_t = _dn_gates(hb, w_slot.astype(BF16), al_slot, dt_slot, tm_out)
            gate_t = gate_t[:n_kh * SUBLANES].reshape(n_kh, SUBLANES, m // CHUNK, CHUNK)
            gate_t = jnp.transpose(gate_t, (0, 2, 1, 3))
            gate_t = jnp.concatenate([gate_t, gate_t], axis=-1)
            o = _dn_chunk(qk, v, z, gate_c, gate_t, dn_norm_w[l][None, :], batch, seq, n_kh, tc)
            h, hb = _proj_ln(o, dn_w_out[l].astype(BF16), h, g0, b0, alpha, tm_out)
        h, hb = _ffn(h, hb, ffn_w_gate_up[i].astype(BF16), ffn_w_down[i].astype(BF16), g1, b1, alpha,
                     tm_out, th)
    return h.reshape(batch, seq, d)
```

```python
import functools

import jax
import jax.numpy as jnp
from jax import lax
from jax.experimental import pallas as pl
from jax.experimental.pallas import tpu as pltpu

HEAD_DIM = 128
CHUNK = 64
SUBLANES = 8
LANES = 128
LN_EPS = 1e-5
RMS_EPS = 1e-6
NEG_BIG = -1e30
VMEM_LIMIT_BYTES = 60 * 1024 * 1024

F32 = jnp.float32
BF16 = jnp.bfloat16


def _cparams(n_axes):
    return pltpu.CompilerParams(
        dimension_semantics=("arbitrary",) * n_axes,
        vmem_limit_bytes=VMEM_LIMIT_BYTES)


def _dot(a, b):
    return jnp.dot(a, b, preferred_element_type=F32)


def _dot_nt(a, b):
    return lax.dot_general(a, b, (((1,), (1,)), ((), ())), preferred_element_type=F32)


def _dot_tn(a, b):
    return lax.dot_general(a, b, (((0,), (0,)), ((), ())), preferred_element_type=F32)


def _layer_norm(y, gain, bias):
    mu = jnp.mean(y, axis=-1, keepdims=True)
    d = y - mu
    var = jnp.mean(d * d, axis=-1, keepdims=True)
    return d * lax.rsqrt(var + LN_EPS) * gain + bias


def _silu(t):
    return t * jax.nn.sigmoid(t)


def _causal_conv(u, prev, w_ref):
    ksize = w_ref.shape[0]
    out = u * w_ref[ksize - 1:ksize, :]
    head = jnp.concatenate([prev, u[0:SUBLANES]], axis=0)
    for d in range(1, ksize):
        shifted = pltpu.roll(u, d, axis=0)
        top = pltpu.roll(head, d, axis=0)[SUBLANES:2 * SUBLANES]
        shifted = jnp.concatenate([top, shifted[SUBLANES:]], axis=0)
        out = out + shifted * w_ref[ksize - 1 - d:ksize - d, :]
    return out


def _causal_taps(ext, w_rows):
    k = len(w_rows)
    assert k <= SUBLANES
    s1 = pltpu.roll(ext, 1, axis=0)
    out = None
    for d0 in range(0, k, 2):
        term = ext * w_rows[k - 1 - d0]
        if d0 + 1 < k:
            term = term + s1 * w_rows[k - 2 - d0]
        if d0:
            term = pltpu.roll(term, d0, axis=0)
        out = term if out is None else out + term
    return out[SUBLANES:]


def _conv_history(carry_ref, j, i, tiles_per_seq):
    prev = carry_ref[j]
    return jnp.where(i % tiles_per_seq == 0, jnp.zeros_like(prev), prev)


def _sc_in_kernel(xb_ref, wb_ref, wc_ref, wh_ref, cw_ref, o_ref, carry_ref, *, tiles_per_seq):
    i, j = pl.program_id(0), pl.program_id(1)
    xb = xb_ref[...]
    u = _dot(xb, wc_ref[...]) * _dot(xb, wh_ref[...])
    prev = _conv_history(carry_ref, j, i, tiles_per_seq)
    y = _causal_conv(u, prev, cw_ref)
    carry_ref[j] = u[u.shape[0] - SUBLANES:]
    o_ref[...] = (_dot(xb, wb_ref[...]) * y).astype(o_ref.dtype)


def _sc_in(xb, w_in, conv_w, seq, tm, tn):
    m, d = xb.shape
    nd = d // tn
    return pl.pallas_call(
        functools.partial(_sc_in_kernel, tiles_per_seq=seq // tm),
        grid=(m // tm, nd),
        in_specs=[
            pl.BlockSpec((tm, d), lambda i, j: (i, 0)),
            pl.BlockSpec((d, tn), lambda i, j: (0, j)),
            pl.BlockSpec((d, tn), lambda i, j: (0, nd + j)),
            pl.BlockSpec((d, tn), lambda i, j: (0, 2 * nd + j)),
            pl.BlockSpec((conv_w.shape[0], tn), lambda i, j: (0, j)),
        ],
        out_specs=pl.BlockSpec((tm, tn), lambda i, j: (i, j)),
        out_shape=jax.ShapeDtypeStruct((m, d), BF16),
        scratch_shapes=[pltpu.VMEM((nd, SUBLANES, tn), F32)],
        compiler_params=_cparams(2),
        name="sc_in",
    )(xb, w_in, w_in, w_in, conv_w)


def _proj_ln_kernel(a_ref, w_ref, x_ref, gb_ref, o_ref, ob_ref, *, alpha):
    y = alpha * x_ref[...] + _dot(a_ref[...], w_ref[...])
    out = _layer_norm(y, gb_ref[0:1, :], gb_ref[1:2, :])
    o_ref[...] = out
    ob_ref[...] = out.astype(BF16)


def _proj_ln(a, w, x, gain_bias, alpha, tm):
    m, d = x.shape
    k = a.shape[1]
    return pl.pallas_call(
        functools.partial(_proj_ln_kernel, alpha=alpha),
        grid=(m // tm,),
        in_specs=[
            pl.BlockSpec((tm, k), lambda i: (i, 0)),
            pl.BlockSpec((k, d), lambda i: (0, 0), pipeline_mode=pl.Buffered(1)),
            pl.BlockSpec((tm, d), lambda i: (i, 0)),
            pl.BlockSpec((2, d), lambda i: (0, 0)),
        ],
        out_specs=[pl.BlockSpec((tm, d), lambda i: (i, 0)), pl.BlockSpec((tm, d), lambda i: (i, 0))],
        out_shape=[jax.ShapeDtypeStruct((m, d), F32), jax.ShapeDtypeStruct((m, d), BF16)],
        compiler_params=_cparams(1),
        name="proj_ln",
    )(a, w, x, gain_bias)


def _ffn_kernel(x_ref, wgu_ref, wd_ref, gb_ref, o_ref, ob_ref, xb_ref, acc_ref, *, alpha):
    j = pl.program_id(1)
    th = wd_ref.shape[0]

    @pl.when(j == 0)
    def _():
        xb_ref[...] = x_ref[...].astype(BF16)
        acc_ref[...] = jnp.zeros_like(acc_ref)

    gu = _dot(xb_ref[...], wgu_ref[...])
    act = _silu(gu[:, :th]) * gu[:, th:]
    acc_ref[...] += _dot(act.astype(BF16), wd_ref[...])

    @pl.when(j == pl.num_programs(1) - 1)
    def _():
        y = alpha * x_ref[...] + acc_ref[...]
        out = _layer_norm(y, gb_ref[0:1, :], gb_ref[1:2, :])
        o_ref[...] = out
        ob_ref[...] = out.astype(BF16)


def _ffn(x, w_gate_up_tiles, w_down, gain_bias, alpha, tm):
    m, d = x.shape
    nh, _, th2 = w_gate_up_tiles.shape
    th = th2 // 2
    return pl.pallas_call(
        functools.partial(_ffn_kernel, alpha=alpha),
        grid=(m // tm, nh),
        in_specs=[
            pl.BlockSpec((tm, d), lambda i, j: (i, 0)),
            pl.BlockSpec((None, d, th2), lambda i, j: (j, 0, 0)),
            pl.BlockSpec((th, d), lambda i, j: (j, 0)),
            pl.BlockSpec((2, d), lambda i, j: (0, 0)),
        ],
        out_specs=[pl.BlockSpec((tm, d), lambda i, j: (i, 0)), pl.BlockSpec((tm, d), lambda i, j: (i, 0))],
        out_shape=[jax.ShapeDtypeStruct((m, d), F32), jax.ShapeDtypeStruct((m, d), BF16)],
        scratch_shapes=[pltpu.VMEM((tm, d), BF16), pltpu.VMEM((tm, d), F32)],
        compiler_params=_cparams(2),
        name="ffn",
    )(x, w_gate_up_tiles, w_down, gain_bias)


def _dn_in_kernel(xb_ref, w_ref, cw_ref, o_ref, *scratch, tiles_per_seq, conv, l2norm, n_scaled, n_sub):
    i, j = pl.program_id(0), pl.program_id(1)
    tm = o_ref.shape[0]
    rs = tm // n_sub

    def block_dot(b):
        return _dot(xb_ref[b * rs:(b + 1) * rs, :], w_ref[...])

    if not conv:
        for b in range(n_sub):
            o_ref[b * rs:(b + 1) * rs, :] = block_dot(b).astype(o_ref.dtype)
        return

    acc0_ref, acc1_ref, carry_ref = scratch
    acc = (acc0_ref, acc1_ref)
    ksize = cw_ref.shape[0]

    @pl.when(i % tiles_per_seq == 0)
    def _():
        acc0_ref[0:SUBLANES, :] = jnp.zeros((SUBLANES, acc0_ref.shape[1]), F32)

    @pl.when(i % tiles_per_seq != 0)
    def _():
        acc0_ref[0:SUBLANES, :] = carry_ref[j]

    scale = jnp.where(j < n_scaled, HEAD_DIM ** -0.5, 1.0).astype(F32)

    def epilogue(b):
        src = acc[b % 2]
        for s in range(o_ref.shape[1] // HEAD_DIM):
            lanes = slice(s * HEAD_DIM, (s + 1) * HEAD_DIM)
            t = _silu(_causal_taps(src[:, lanes], [cw_ref[k:k + 1, lanes] for k in range(ksize)]))
            if l2norm:
                t = t * (lax.rsqrt(jnp.sum(t * t, axis=-1, keepdims=True) + RMS_EPS) * scale)
            o_ref[b * rs:(b + 1) * rs, lanes] = t.astype(o_ref.dtype)

    tail = None
    for b in range(n_sub):
        res = block_dot(b)
        acc[b % 2][SUBLANES:SUBLANES + rs, :] = res
        if b > 0:
            epilogue(b - 1)
            acc[b % 2][0:SUBLANES, :] = tail
        tail = res[rs - SUBLANES:, :]
    epilogue(n_sub - 1)
    carry_ref[j] = tail


def _dn_in(xb, w_in, conv_w, col0, width, seq, tm, tn, *, conv, l2norm=False, n_scaled=0, name):
    m, d = xb.shape
    nj = width // tn
    j0 = col0 // tn
    n_sub = max(1, tm // 128)
    blk = pltpu.VMEM((tm // n_sub + SUBLANES, tn), F32)
    scratch = [blk, blk, pltpu.VMEM((nj, SUBLANES, tn), F32)] if conv else []
    cw_j0 = j0 if conv else 0
    return pl.pallas_call(
        functools.partial(_dn_in_kernel, tiles_per_seq=seq // tm, conv=conv, l2norm=l2norm,
                          n_scaled=n_scaled, n_sub=n_sub),
        grid=(m // tm, nj),
        in_specs=[
            pl.BlockSpec((tm, d), lambda i, j: (i, 0)),
            pl.BlockSpec((d, tn), lambda i, j: (0, j0 + j)),
            pl.BlockSpec((conv_w.shape[0], tn), lambda i, j: (0, cw_j0 + j if conv else 0)),
        ],
        out_specs=pl.BlockSpec((tm, tn), lambda i, j: (i, j)),
        out_shape=jax.ShapeDtypeStruct((m, width), BF16),
        scratch_shapes=scratch,
        compiler_params=_cparams(2),
        name=name,
    )(xb, w_in, conv_w)


def _softplus(t):
    return jnp.maximum(t, 0.0) + jnp.log1p(jnp.exp(-jnp.abs(t)))


def _chunk_cumsum(g, axis):
    pos = lax.broadcasted_iota(jnp.int32, g.shape, axis) % CHUNK
    s = 1
    while s < CHUNK:
        g = g + jnp.where(pos >= s, pltpu.roll(g, s, axis=axis), 0.0)
        s *= 2
    return g


def _gates(raw, a_log, dt_bias, slot_axis, token_axis):
    c = lax.broadcasted_iota(jnp.int32, raw.shape, slot_axis) % SUBLANES
    beta = jax.nn.sigmoid(raw)
    g = -jnp.exp(a_log) * _softplus(raw + dt_bias)
    g = _chunk_cumsum(jnp.where(jnp.logical_and(c >= 2, c < 4), g, 0.0), token_axis)
    return jnp.where(c < 2, beta, g)


def _dn_gate_kernel(x_ref, w_ref, wt_ref, al_ref, dt_ref, alt_ref, dtt_ref, gc_ref, gt_ref):
    xb = x_ref[...]
    gc_ref[...] = _gates(_dot(xb, w_ref[...]), al_ref[...], dt_ref[...], 1, 0)
    gt_ref[...] = _gates(_dot_nt(wt_ref[...], xb), alt_ref[...], dtt_ref[...], 0, 1)


def _dn_gates(x, w_slot, a_log_slot, dt_slot, tm):
    m, d = x.shape
    ns = w_slot.shape[1]
    return pl.pallas_call(
        _dn_gate_kernel,
        grid=(m // tm,),
        in_specs=[
            pl.BlockSpec((tm, d), lambda i: (i, 0)),
            pl.BlockSpec((d, ns), lambda i: (0, 0)),
            pl.BlockSpec((ns, d), lambda i: (0, 0)),
            pl.BlockSpec((1, ns), lambda i: (0, 0)),
            pl.BlockSpec((1, ns), lambda i: (0, 0)),
            pl.BlockSpec((ns, 1), lambda i: (0, 0)),
            pl.BlockSpec((ns, 1), lambda i: (0, 0)),
        ],
        out_specs=[pl.BlockSpec((tm, ns), lambda i: (i, 0)),
                   pl.BlockSpec((ns, tm), lambda i: (0, i))],
        out_shape=[jax.ShapeDtypeStruct((m, ns), F32), jax.ShapeDtypeStruct((ns, m), F32)],
        compiler_params=_cparams(1),
        name="dn_gates",
    )(x, w_slot, w_slot.T, a_log_slot[None, :], dt_slot[None, :], a_log_slot[:, None], dt_slot[:, None])


def _dn_chunk_kernel(q_ref, k_ref, v_ref, z_ref, gc_ref, gt_ref, nw_ref, o_ref,
                     state_ref, qm_ref, on_ref, cd_ref):
    kh, s = pl.program_id(1), pl.program_id(2)
    n_chunks = q_ref.shape[0] // CHUNK
    slot_a = s % 2
    slot_b = 1 - slot_a

    @pl.when(s == 0)
    def _():
        state_ref[...] = jnp.zeros_like(state_ref)
        qm_ref[1] = jnp.zeros(qm_ref.shape[1:], qm_ref.dtype)
        on_ref[1] = jnp.zeros(on_ref.shape[1:], on_ref.dtype)
        cd_ref[1] = jnp.zeros(cd_ref.shape[1:], cd_ref.dtype)

    row = lax.broadcasted_iota(jnp.int32, (CHUNK, LANES), 0)
    lane = lax.broadcasted_iota(jnp.int32, (CHUNK, LANES), 1)
    col = lane % CHUNK
    causal = col <= row
    strict = col < row
    upper_half = lane >= CHUNK
    eye_hi = jnp.where(jnp.logical_and(upper_half, col == row), 1.0, 0.0).astype(F32)
    norm_w = nw_ref[...]
    states = [state_ref[0], state_ref[1]]
    chains = [(c, e) for c in range(n_chunks) for e in range(2)]

    def recurrence(c):
        rows = slice(c * CHUNK, (c + 1) * CHUNK)
        for e in range(2):
            qms = _dot(qm_ref[slot_b, c, e], states[e].astype(BF16))
            o_n = on_ref[slot_b, c, e]
            o = qms[:CHUNK] + o_n[:CHUNK]
            states[e] = states[e] * cd_ref[slot_b, c, e][:1, :] + o_n[CHUNK:] - qms[CHUNK:]
            o = o * lax.rsqrt(jnp.mean(o * o, axis=-1, keepdims=True) + RMS_EPS) * norm_w
            ze = z_ref[rows, e * HEAD_DIM:(e + 1) * HEAD_DIM].astype(F32)
            o_ref[rows, e * HEAD_DIM:(e + 1) * HEAD_DIM] = (o * _silu(ze)).astype(o_ref.dtype)

    gsel_all = pltpu.roll(gc_ref[...], (LANES - SUBLANES * kh) % LANES, axis=1)
    kf, qf, k_t, qk2, kk2 = [], [], [], [], []
    for c in range(n_chunks):
        rows = slice(c * CHUNK, (c + 1) * CHUNK)
        qc, kc = q_ref[rows, :], k_ref[rows, :]
        qf.append(qc.astype(F32))
        kf.append(kc.astype(F32))
        k_t.append(jnp.transpose(kf[c]).astype(BF16))
        qkk = _dot_nt(jnp.concatenate([qc, kc], axis=0), jnp.concatenate([kc, kc], axis=0))
        qk2.append(qkk[:CHUNK])
        kk2.append(qkk[CHUNK:])
    recurrence(0)

    beta_col, g_col, g_last, decay, w_mat = {}, {}, {}, {}, {}
    for (c, e) in chains:
        gsel = gsel_all[c * CHUNK:(c + 1) * CHUNK]
        g_row = gt_ref[c][2 + e:3 + e, :]
        beta_col[c, e] = jnp.broadcast_to(gsel[:, e:e + 1], (CHUNK, LANES))
        g_col[c, e] = jnp.broadcast_to(gsel[:, 2 + e:3 + e], (CHUNK, LANES))
        g_last[c, e] = g_row[:, CHUNK - 1:CHUNK]
        decay[c, e] = jnp.exp(jnp.where(causal, g_col[c, e] - g_row, NEG_BIG))
        x_mat = jnp.where(strict, -(kk2[c] * beta_col[c, e] * decay[c, e]), 0.0)
        x_b = x_mat.astype(BF16)
        w_mat[c, e] = jnp.where(upper_half, x_mat + eye_hi, _dot(x_b[:, :CHUNK], x_b))
    recurrence(1)
    for level in range(4):
        for ce in chains:
            w_b = w_mat[ce].astype(BF16)
            w_mat[ce] = _dot(w_b[:, :CHUNK], w_b) + jnp.where(upper_half, w_mat[ce], 0.0)
        recurrence(2 + level)
    sol = {}
    for (c, e) in chains:
        t_fac = w_mat[c, e][:, CHUNK:]
        t_mat = t_fac + _dot(w_mat[c, e][:, :CHUNK].astype(BF16), t_fac.astype(BF16))
        ve = v_ref[c * CHUNK:(c + 1) * CHUNK, e * HEAD_DIM:(e + 1) * HEAD_DIM].astype(F32)
        exp_g = jnp.exp(g_col[c, e])
        rhs = jnp.concatenate([kf[c] * (beta_col[c, e] * exp_g), ve * beta_col[c, e]], axis=1)
        w_mat[c, e] = (t_mat, rhs, exp_g)
    recurrence(6)
    for ce in chains:
        t_mat, rhs, _ = w_mat[ce]
        sol[ce] = _dot(t_mat.astype(BF16), rhs.astype(BF16))
    recurrence(7)
    for (c, e) in chains:
        a_qk = (qk2[c] * decay[c, e])[:, :CHUNK].astype(BF16)
        qo = _dot(a_qk, sol[c, e].astype(BF16))
        q_eff = qf[c] * w_mat[c, e][2] - qo[:, :HEAD_DIM]
        k_tail = jnp.exp(g_last[c, e] - g_col[c, e])[:, :1]
        mn = _dot(k_t[c], (sol[c, e] * k_tail).astype(BF16))
        qm_ref[slot_a, c, e] = jnp.concatenate([q_eff, mn[:, :HEAD_DIM]], axis=0).astype(BF16)
        on_ref[slot_a, c, e] = jnp.concatenate([qo[:, HEAD_DIM:], mn[:, HEAD_DIM:]], axis=0)
        cd_ref[slot_a, c, e] = jnp.broadcast_to(jnp.exp(g_last[c, e]), (SUBLANES, LANES))

    state_ref[0] = states[0]
    state_ref[1] = states[1]


def _dn_chunk(qk, v, z, gate_c, gate_t, norm_w, batch, seq, n_kh, tc):
    m = qk.shape[0]
    nt = seq // tc
    cpt = tc // CHUNK
    assert cpt == 8, "the recurrence steps are interleaved with 8 preparation levels"

    def prep(b, s):
        return b * nt + jnp.minimum(s, nt - 1)

    def rec(b, s):
        return b * nt + jnp.maximum(s - 1, 0)

    return pl.pallas_call(
        _dn_chunk_kernel,
        grid=(batch, n_kh, nt + 1),
        in_specs=[
            pl.BlockSpec((tc, HEAD_DIM), lambda b, h, s: (prep(b, s), h)),
            pl.BlockSpec((tc, HEAD_DIM), lambda b, h, s: (prep(b, s), n_kh + h)),
            pl.BlockSpec((tc, 2 * HEAD_DIM), lambda b, h, s: (prep(b, s), h)),
            pl.BlockSpec((tc, 2 * HEAD_DIM), lambda b, h, s: (rec(b, s), h)),
            pl.BlockSpec((tc, gate_c.shape[1]), lambda b, h, s: (prep(b, s), 0)),
            pl.BlockSpec((None, cpt, SUBLANES, LANES), lambda b, h, s: (h, prep(b, s), 0, 0)),
            pl.BlockSpec((1, HEAD_DIM), lambda b, h, s: (0, 0)),
        ],
        out_specs=pl.BlockSpec((tc, 2 * HEAD_DIM), lambda b, h, s: (rec(b, s), h)),
        out_shape=jax.ShapeDtypeStruct((m, 2 * n_kh * HEAD_DIM), BF16),
        scratch_shapes=[pltpu.VMEM((2, HEAD_DIM, HEAD_DIM), F32),
                        pltpu.VMEM((2, cpt, 2, CHUNK + HEAD_DIM, HEAD_DIM), BF16),
                        pltpu.VMEM((2, cpt, 2, CHUNK + HEAD_DIM, HEAD_DIM), F32),
                        pltpu.VMEM((2, cpt, 2, SUBLANES, LANES), F32)],
        compiler_params=_cparams(3),
        name="dn_chunk",
    )(qk, qk, v, z, gate_c, gate_t, norm_w)


def _tile(n, want):
    if n <= want:
        return n
    t = want - want % LANES
    while n % t:
        t -= LANES
    return t


def _gate_slots(w_beta_a, a_log, dt_bias, n_vh):
    n_kh = n_vh // 2
    d = w_beta_a.shape[0]
    wb = w_beta_a[:, :n_vh].reshape(d, n_kh, 2)
    wa = w_beta_a[:, n_vh:].reshape(d, n_kh, 2)
    w_slot = jnp.concatenate([wb, wa, jnp.zeros((d, n_kh, SUBLANES - 4), w_beta_a.dtype)], axis=2)
    zeros2 = jnp.zeros((n_kh, 2), F32)
    zeros4 = jnp.zeros((n_kh, SUBLANES - 4), F32)
    al = jnp.concatenate([zeros2, a_log.astype(F32).reshape(n_kh, 2), zeros4], axis=1)
    dt = jnp.concatenate([zeros2, dt_bias.astype(F32).reshape(n_kh, 2), zeros4], axis=1)
    pad = (-n_kh * SUBLANES) % LANES
    w_slot = jnp.pad(w_slot.reshape(d, n_kh * SUBLANES), ((0, 0), (0, pad)))
    return w_slot, jnp.pad(al.reshape(-1), (0, pad)), jnp.pad(dt.reshape(-1), (0, pad))


def kernel(x, sc_w_in, sc_conv_w, sc_w_out, dn_w_in, dn_conv_w, dn_a_log, dn_dt_bias, dn_norm_w,
           dn_w_out, ffn_w_gate_up, ffn_w_down, ln_gain, ln_bias):
    batch, seq, d = x.shape
    depth = ln_gain.shape[0]
    alpha = float((2 * depth) ** 0.25)
    m = batch * seq
    n_vh = dn_a_log.shape[1]
    n_kh = n_vh // 2
    key_dim = n_kh * HEAD_DIM
    val_dim = n_vh * HEAD_DIM
    qkv_dim = 2 * key_dim + val_dim
    hidden = ffn_w_down.shape[1]
    assert seq % CHUNK == 0 and d % LANES == 0 and dn_norm_w.shape[1] == HEAD_DIM

    tm_in = _tile(seq, 1024)
    tm_out = _tile(seq, 512)
    tn = _tile(d, 512)
    th = _tile(hidden, 512)
    tc = _tile(seq, 512)

    h = x.reshape(m, d)
    hb = h.astype(BF16)
    for i in range(depth):
        l = i // 2
        gb0 = jnp.stack([ln_gain[i, 0], ln_bias[i, 0]])
        gb1 = jnp.stack([ln_gain[i, 1], ln_bias[i, 1]])
        if i % 2 == 0:
            a = _sc_in(hb, sc_w_in[l].astype(BF16), sc_conv_w[l], seq, tm_in, tn)
            h, hb = _proj_ln(a, sc_w_out[l].astype(BF16), h, gb0, alpha, tm_out)
        else:
            w_in = dn_w_in[l]
            w_qkvz = w_in[:, :qkv_dim + val_dim].astype(BF16)
            cw = dn_conv_w[l]
            qk = _dn_in(hb, w_qkvz, cw, 0, 2 * key_dim, seq, tm_in, tn, conv=True, l2norm=True,
                        n_scaled=key_dim // tn, name="dn_in_qk")
            v = _dn_in(hb, w_qkvz, cw, 2 * key_dim, val_dim, seq, tm_in, tn, conv=True, name="dn_in_v")
            z = _dn_in(hb, w_qkvz, cw, qkv_dim, val_dim, seq, tm_in, tn, conv=False, name="dn_in_z")
            w_slot, al_slot, dt_slot = _gate_slots(w_in[:, qkv_dim + val_dim:], dn_a_log[l],
                                                   dn_dt_bias[l], n_vh)
            gate_c, gate_t = _dn_gates(hb, w_slot.astype(BF16), al_slot, dt_slot, tm_out)
            gate_t = gate_t[:n_kh * SUBLANES].reshape(n_kh, SUBLANES, m // CHUNK, CHUNK)
            gate_t = jnp.transpose(gate_t, (0, 2, 1, 3))
            gate_t = jnp.concatenate([gate_t, gate_t], axis=-1)
            o = _dn_chunk(qk, v, z, gate_c, gate_t, dn_norm_w[l][None, :], batch, seq, n_kh, tc)
            h, hb = _proj_ln(o, dn_w_out[l].astype(BF16), h, gb0, alpha, tm_out)
        w_gu = ffn_w_gate_up[i].reshape(d, 2, hidden // th, th).transpose(2, 0, 1, 3)
        w_gu = w_gu.reshape(hidden // th, d, 2 * th).astype(BF16)
        h, hb = _ffn(h, w_gu, ffn_w_down[i].astype(BF16), gb1, alpha, tm_out)
    return h.reshape(batch, seq, d)
```

```python
import functools

import jax
import jax.numpy as jnp
from jax import lax
from jax.experimental import pallas as pl
from jax.experimental.pallas import tpu as pltpu

HEAD_DIM = 128
CHUNK = 64
SUBLANES = 8
LANES = 128
LN_EPS = 1e-5
RMS_EPS = 1e-6
NEG_BIG = -1e30
VMEM_LIMIT_BYTES = 60 * 1024 * 1024

F32 = jnp.float32
BF16 = jnp.bfloat16


def _cparams(n_axes):
    return pltpu.CompilerParams(
        dimension_semantics=("arbitrary",) * n_axes,
        vmem_limit_bytes=VMEM_LIMIT_BYTES)


def _dot(a, b):
    return jnp.dot(a, b, preferred_element_type=F32)


def _dot_nt(a, b):
    return lax.dot_general(a, b, (((1,), (1,)), ((), ())), preferred_element_type=F32)


def _dot_tn(a, b):
    return lax.dot_general(a, b, (((0,), (0,)), ((), ())), preferred_element_type=F32)


def _layer_norm(y, gain, bias):
    mu = jnp.mean(y, axis=-1, keepdims=True)
    d = y - mu
    var = jnp.mean(d * d, axis=-1, keepdims=True)
    return d * lax.rsqrt(var + LN_EPS) * gain + bias


def _silu(t):
    return t * jax.nn.sigmoid(t)


def _causal_conv(u, prev, w_ref):
    ksize = w_ref.shape[0]
    out = u * w_ref[ksize - 1:ksize, :]
    head = jnp.concatenate([prev, u[0:SUBLANES]], axis=0)
    for d in range(1, ksize):
        shifted = pltpu.roll(u, d, axis=0)
        top = pltpu.roll(head, d, axis=0)[SUBLANES:2 * SUBLANES]
        shifted = jnp.concatenate([top, shifted[SUBLANES:]], axis=0)
        out = out + shifted * w_ref[ksize - 1 - d:ksize - d, :]
    return out


def _causal_taps(ext, w_rows):
    k = len(w_rows)
    assert k <= SUBLANES
    s1 = pltpu.roll(ext, 1, axis=0)
    out = None
    for d0 in range(0, k, 2):
        term = ext * w_rows[k - 1 - d0]
        if d0 + 1 < k:
            term = term + s1 * w_rows[k - 2 - d0]
        if d0:
            term = pltpu.roll(term, d0, axis=0)
        out = term if out is None else out + term
    return out[SUBLANES:]


def _conv_history(carry_ref, j, i, tiles_per_seq):
    prev = carry_ref[j]
    return jnp.where(i % tiles_per_seq == 0, jnp.zeros_like(prev), prev)


def _sc_in_kernel(xb_ref, wb_ref, wc_ref, wh_ref, cw_ref, o_ref, carry_ref, *, tiles_per_seq):
    i, j = pl.program_id(0), pl.program_id(1)
    xb = xb_ref[...]
    u = _dot(xb, wc_ref[...]) * _dot(xb, wh_ref[...])
    prev = _conv_history(carry_ref, j, i, tiles_per_seq)
    y = _causal_conv(u, prev, cw_ref)
    carry_ref[j] = u[u.shape[0] - SUBLANES:]
    o_ref[...] = (_dot(xb, wb_ref[...]) * y).astype(o_ref.dtype)


def _sc_in(xb, w_in, conv_w, seq, tm, tn):
    m, d = xb.shape
    nd = d // tn
    return pl.pallas_call(
        functools.partial(_sc_in_kernel, tiles_per_seq=seq // tm),
        grid=(m // tm, nd),
        in_specs=[
            pl.BlockSpec((tm, d), lambda i, j: (i, 0)),
            pl.BlockSpec((d, tn), lambda i, j: (0, j)),
            pl.BlockSpec((d, tn), lambda i, j: (0, nd + j)),
            pl.BlockSpec((d, tn), lambda i, j: (0, 2 * nd + j)),
            pl.BlockSpec((conv_w.shape[0], tn), lambda i, j: (0, j)),
        ],
        out_specs=pl.BlockSpec((tm, tn), lambda i, j: (i, j)),
        out_shape=jax.ShapeDtypeStruct((m, d), BF16),
        scratch_shapes=[pltpu.VMEM((nd, SUBLANES, tn), F32)],
        compiler_params=_cparams(2),
        name="sc_in",
    )(xb, w_in, w_in, w_in, conv_w)


def _proj_ln_kernel(a_ref, w_ref, x_ref, gb_ref, o_ref, *, alpha):
    y = alpha * x_ref[...] + _dot(a_ref[...], w_ref[...])
    o_ref[...] = _layer_norm(y, gb_ref[0:1, :], gb_ref[1:2, :])


def _proj_ln(a, w, x, gain_bias, alpha, tm):
    m, d = x.shape
    k = a.shape[1]
    return pl.pallas_call(
        functools.partial(_proj_ln_kernel, alpha=alpha),
        grid=(m // tm,),
        in_specs=[
            pl.BlockSpec((tm, k), lambda i: (i, 0)),
            pl.BlockSpec((k, d), lambda i: (0, 0), pipeline_mode=pl.Buffered(1)),
            pl.BlockSpec((tm, d), lambda i: (i, 0)),
            pl.BlockSpec((2, d), lambda i: (0, 0)),
        ],
        out_specs=pl.BlockSpec((tm, d), lambda i: (i, 0)),
        out_shape=jax.ShapeDtypeStruct((m, d), F32),
        compiler_params=_cparams(1),
        name="proj_ln",
    )(a, w, x, gain_bias)


def _ffn_kernel(x_ref, wg_ref, wu_ref, wd_ref, gb_ref, o_ref, *rest, alpha):
    *ob_ref, xb_ref, acc_ref = rest
    j = pl.program_id(1)

    @pl.when(j == 0)
    def _():
        xb_ref[...] = x_ref[...].astype(BF16)
        acc_ref[...] = jnp.zeros_like(acc_ref)

    xb = xb_ref[...]
    act = _silu(_dot(xb, wg_ref[...])) * _dot(xb, wu_ref[...])
    acc_ref[...] += _dot(act.astype(BF16), wd_ref[...])

    @pl.when(j == pl.num_programs(1) - 1)
    def _():
        y = alpha * x_ref[...] + acc_ref[...]
        out = _layer_norm(y, gb_ref[0:1, :], gb_ref[1:2, :])
        o_ref[...] = out
        for r in ob_ref:
            r[...] = out.astype(BF16)


def _ffn(x, w_gate_up, w_down, gain_bias, alpha, tm, th, emit_bf16):
    m, d = x.shape
    hidden = w_down.shape[0]
    nh = hidden // th
    row_block = pl.BlockSpec((tm, d), lambda i, j: (i, 0))
    outs = pl.pallas_call(
        functools.partial(_ffn_kernel, alpha=alpha),
        grid=(m // tm, nh),
        in_specs=[
            row_block,
            pl.BlockSpec((d, th), lambda i, j: (0, j)),
            pl.BlockSpec((d, th), lambda i, j: (0, nh + j)),
            pl.BlockSpec((th, d), lambda i, j: (j, 0)),
            pl.BlockSpec((2, d), lambda i, j: (0, 0)),
        ],
        out_specs=[row_block] * (2 if emit_bf16 else 1),
        out_shape=[jax.ShapeDtypeStruct((m, d), F32)] + [jax.ShapeDtypeStruct((m, d), BF16)] * emit_bf16,
        scratch_shapes=[pltpu.VMEM((tm, d), BF16), pltpu.VMEM((tm, d), F32)],
        compiler_params=_cparams(2),
        name="ffn",
    )(x, w_gate_up, w_gate_up, w_down, gain_bias)
    return outs if emit_bf16 else (outs[0], None)


def _dn_in_kernel(xb_ref, w_ref, cw_ref, o_ref, *scratch, tiles_per_seq, conv, l2norm, n_scaled, n_sub):
    i, j = pl.program_id(0), pl.program_id(1)
    tm = o_ref.shape[0]
    rs = tm // n_sub

    def block_dot(b):
        return _dot(xb_ref[b * rs:(b + 1) * rs, :], w_ref[...])

    if not conv:
        for b in range(n_sub):
            o_ref[b * rs:(b + 1) * rs, :] = block_dot(b).astype(o_ref.dtype)
        return

    acc0_ref, acc1_ref, carry_ref = scratch
    acc = (acc0_ref, acc1_ref)
    ksize = cw_ref.shape[0]

    @pl.when(i % tiles_per_seq == 0)
    def _():
        acc0_ref[0:SUBLANES, :] = jnp.zeros((SUBLANES, acc0_ref.shape[1]), F32)

    @pl.when(i % tiles_per_seq != 0)
    def _():
        acc0_ref[0:SUBLANES, :] = carry_ref[j]

    scale = jnp.where(j < n_scaled, HEAD_DIM ** -0.5, 1.0).astype(F32)

    def epilogue(b):
        src = acc[b % 2]
        for s in range(o_ref.shape[1] // HEAD_DIM):
            lanes = slice(s * HEAD_DIM, (s + 1) * HEAD_DIM)
            t = _silu(_causal_taps(src[:, lanes], [cw_ref[k:k + 1, lanes] for k in range(ksize)]))
            if l2norm:
                t = t * (lax.rsqrt(jnp.sum(t * t, axis=-1, keepdims=True) + RMS_EPS) * scale)
            o_ref[b * rs:(b + 1) * rs, lanes] = t.astype(o_ref.dtype)

    tail = None
    for b in range(n_sub):
        res = block_dot(b)
        acc[b % 2][SUBLANES:SUBLANES + rs, :] = res
        if b > 0:
            epilogue(b - 1)
            acc[b % 2][0:SUBLANES, :] = tail
        tail = res[rs - SUBLANES:, :]
    epilogue(n_sub - 1)
    carry_ref[j] = tail


def _dn_in(xb, w_in, conv_w, col0, width, seq, tm, tn, *, conv, l2norm=False, n_scaled=0, name):
    m, d = xb.shape
    nj = width // tn
    j0 = col0 // tn
    n_sub = max(1, tm // 128)
    blk = pltpu.VMEM((tm // n_sub + SUBLANES, tn), F32)
    scratch = [blk, blk, pltpu.VMEM((nj, SUBLANES, tn), F32)] if conv else []
    cw_j0 = j0 if conv else 0
    return pl.pallas_call(
        functools.partial(_dn_in_kernel, tiles_per_seq=seq // tm, conv=conv, l2norm=l2norm,
                          n_scaled=n_scaled, n_sub=n_sub),
        grid=(m // tm, nj),
        in_specs=[
            pl.BlockSpec((tm, d), lambda i, j: (i, 0)),
            pl.BlockSpec((d, tn), lambda i, j: (0, j0 + j)),
            pl.BlockSpec((conv_w.shape[0], tn), lambda i, j: (0, cw_j0 + j if conv else 0)),
        ],
        out_specs=pl.BlockSpec((tm, tn), lambda i, j: (i, j)),
        out_shape=jax.ShapeDtypeStruct((m, width), BF16),
        scratch_shapes=scratch,
        compiler_params=_cparams(2),
        name=name,
    )(xb, w_in, conv_w)


def _softplus(t):
    return jnp.maximum(t, 0.0) + jnp.log1p(jnp.exp(-jnp.abs(t)))


def _chunk_cumsum(g, axis):
    pos = lax.broadcasted_iota(jnp.int32, g.shape, axis) % CHUNK
    s = 1
    while s < CHUNK:
        g = g + jnp.where(pos >= s, pltpu.roll(g, s, axis=axis), 0.0)
        s *= 2
    return g


def _gates(raw, a_log, dt_bias, slot_axis, token_axis):
    c = lax.broadcasted_iota(jnp.int32, raw.shape, slot_axis) % SUBLANES
    beta = jax.nn.sigmoid(raw)
    g = -jnp.exp(a_log) * _softplus(raw + dt_bias)
    g = _chunk_cumsum(jnp.where(jnp.logical_and(c >= 2, c < 4), g, 0.0), token_axis)
    return jnp.where(c < 2, beta, g)


def _dn_gate_kernel(x_ref, w_ref, wt_ref, al_ref, dt_ref, alt_ref, dtt_ref, gc_ref, gt_ref):
    xb = x_ref[...]
    gc_ref[...] = _gates(_dot(xb, w_ref[...]), al_ref[...], dt_ref[...], 1, 0)
    gt_ref[...] = _gates(_dot_nt(wt_ref[...], xb), alt_ref[...], dtt_ref[...], 0, 1)


def _dn_gates(x, w_slot, a_log_slot, dt_slot, tm):
    m, d = x.shape
    ns = w_slot.shape[1]
    return pl.pallas_call(
        _dn_gate_kernel,
        grid=(m // tm,),
        in_specs=[
            pl.BlockSpec((tm, d), lambda i: (i, 0)),
            pl.BlockSpec((d, ns), lambda i: (0, 0)),
            pl.BlockSpec((ns, d), lambda i: (0, 0)),
            pl.BlockSpec((1, ns), lambda i: (0, 0)),
            pl.BlockSpec((1, ns), lambda i: (0, 0)),
            pl.BlockSpec((ns, 1), lambda i: (0, 0)),
            pl.BlockSpec((ns, 1), lambda i: (0, 0)),
        ],
        out_specs=[pl.BlockSpec((tm, ns), lambda i: (i, 0)),
                   pl.BlockSpec((ns, tm), lambda i: (0, i))],
        out_shape=[jax.ShapeDtypeStruct((m, ns), F32), jax.ShapeDtypeStruct((ns, m), F32)],
        compiler_params=_cparams(1),
        name="dn_gates",
    )(x, w_slot, w_slot.T, a_log_slot[None, :], dt_slot[None, :], a_log_slot[:, None], dt_slot[:, None])


def _block_diag(a, b):
    zero = jnp.zeros_like(a)
    return jnp.concatenate([jnp.concatenate([a, zero], axis=1), jnp.concatenate([zero, b], axis=1)], axis=0)


def _dn_chunk_kernel(q_ref, k_ref, v_ref, z_ref, gc_ref, gt_ref, nw_ref, o_ref,
                     state_ref, qm_ref, on_ref, cd_ref):
    kh, s = pl.program_id(1), pl.program_id(2)
    n_chunks = q_ref.shape[0] // CHUNK
    slot_a = s % 2
    slot_b = 1 - slot_a

    @pl.when(s == 0)
    def _():
        state_ref[...] = jnp.zeros_like(state_ref)
        qm_ref[1] = jnp.zeros(qm_ref.shape[1:], qm_ref.dtype)
        on_ref[1] = jnp.zeros(on_ref.shape[1:], on_ref.dtype)
        cd_ref[1] = jnp.zeros(cd_ref.shape[1:], cd_ref.dtype)

    row = lax.broadcasted_iota(jnp.int32, (CHUNK, LANES), 0)
    lane = lax.broadcasted_iota(jnp.int32, (CHUNK, LANES), 1)
    col = lane % CHUNK
    causal = col <= row
    strict = col < row
    upper_half = lane >= CHUNK
    lower_half = lane < CHUNK
    eye = col == row
    eye_hi = jnp.where(jnp.logical_and(upper_half, eye), 1.0, 0.0).astype(F32)
    eye_lo = jnp.where(jnp.logical_and(lower_half, eye), 1.0, 0.0).astype(F32)
    norm_w = nw_ref[...]
    state = [state_ref[...]]

    def recurrence(c):
        rows = slice(c * CHUNK, (c + 1) * CHUNK)
        s_b = state[0].astype(BF16)
        qms = _dot(qm_ref[slot_b, c], _block_diag(s_b[:, :HEAD_DIM], s_b[:, HEAD_DIM:]))
        o_n = on_ref[slot_b, c]
        o_all = qms[:CHUNK] + o_n[:CHUNK]
        state[0] = state[0] * cd_ref[slot_b, c][:1, :] + o_n[CHUNK:] - qms[CHUNK:]
        for e in range(2):
            o = o_all[:, e * HEAD_DIM:(e + 1) * HEAD_DIM]
            o = o * lax.rsqrt(jnp.mean(o * o, axis=-1, keepdims=True) + RMS_EPS) * norm_w
            ze = z_ref[rows, e * HEAD_DIM:(e + 1) * HEAD_DIM].astype(F32)
            o_ref[rows, e * HEAD_DIM:(e + 1) * HEAD_DIM] = (o * _silu(ze)).astype(o_ref.dtype)

    recurrence(0)
    gsel_all = pltpu.roll(gc_ref[...], (LANES - SUBLANES * kh) % LANES, axis=1)
    kf, qf, k_t, qk2, kk2 = [], [], [], [], []
    for c in range(n_chunks):
        rows = slice(c * CHUNK, (c + 1) * CHUNK)
        qc, kc = q_ref[rows, :], k_ref[rows, :]
        qf.append(qc.astype(F32))
        kf.append(kc.astype(F32))
        k_t.append(jnp.transpose(kf[c]).astype(BF16))
        qkk = _dot_nt(jnp.concatenate([qc, kc], axis=0), jnp.concatenate([kc, kc], axis=0))
        qk2.append(qkk[:CHUNK])
        kk2.append(qkk[CHUNK:])
    recurrence(1)

    chains = [(c, e) for c in range(n_chunks) for e in range(2)]
    beta_col, g_col, g_last, decay, x_mat, w_mat = {}, {}, {}, {}, {}, {}
    for (c, e) in chains:
        gsel = gsel_all[c * CHUNK:(c + 1) * CHUNK]
        g_row = gt_ref[c][2 + e:3 + e, :]
        beta_col[c, e] = jnp.broadcast_to(gsel[:, e:e + 1], (CHUNK, LANES))
        g_col[c, e] = jnp.broadcast_to(gsel[:, 2 + e:3 + e], (CHUNK, LANES))
        g_last[c, e] = g_row[:, CHUNK - 1:CHUNK]
        decay[c, e] = jnp.exp(jnp.where(causal, g_col[c, e] - g_row, NEG_BIG))
        x_mat[c, e] = jnp.where(strict, -(kk2[c] * beta_col[c, e] * decay[c, e]), 0.0)
    for c in range(n_chunks):
        xa, xb = x_mat[c, 0], x_mat[c, 1]
        sq = _dot(jnp.where(lower_half, xa, xb).astype(BF16), _block_diag(xa.astype(BF16), xb.astype(BF16)))
        w_mat[c, 0] = jnp.where(upper_half, xa + eye_hi, sq[:, :LANES])
        w_mat[c, 1] = jnp.where(upper_half, sq[:, LANES:], xb + eye_lo)
    recurrence(2)
    for level in range(5):
        for c in range(n_chunks):
            wa, wb = w_mat[c, 0], w_mat[c, 1]
            pw = _dot(jnp.where(lower_half, wa, wb).astype(BF16), _block_diag(wa.astype(BF16), wb.astype(BF16)))
            w_mat[c, 0] = pw[:, :LANES] + jnp.where(upper_half, wa, 0.0)
            w_mat[c, 1] = pw[:, LANES:] + jnp.where(lower_half, wb, 0.0)
        if level < 4:
            recurrence(3 + level)
    sol, exp_g = {}, {}
    for (c, e) in chains:
        t_mat = w_mat[c, e][:, CHUNK:] if e == 0 else w_mat[c, e][:, :CHUNK]
        ve = v_ref[c * CHUNK:(c + 1) * CHUNK, e * HEAD_DIM:(e + 1) * HEAD_DIM].astype(F32)
        exp_g[c, e] = jnp.exp(g_col[c, e])
        rhs = jnp.concatenate([kf[c] * (beta_col[c, e] * exp_g[c, e]), ve * beta_col[c, e]], axis=1)
        sol[c, e] = _dot(t_mat.astype(BF16), rhs.astype(BF16))
    recurrence(7)
    for c in range(n_chunks):
        qm, o_n, cd = [], [], []
        for e in range(2):
            a_qk = (qk2[c] * decay[c, e])[:, :CHUNK].astype(BF16)
            qo = _dot(a_qk, sol[c, e].astype(BF16))
            q_eff = qf[c] * exp_g[c, e] - qo[:, :HEAD_DIM]
            k_tail = jnp.exp(g_last[c, e] - g_col[c, e])[:, :1]
            mn = _dot(k_t[c], (sol[c, e] * k_tail).astype(BF16))
            qm.append(jnp.concatenate([q_eff, mn[:, :HEAD_DIM]], axis=0).astype(BF16))
            o_n.append(jnp.concatenate([qo[:, HEAD_DIM:], mn[:, HEAD_DIM:]], axis=0))
            cd.append(jnp.broadcast_to(jnp.exp(g_last[c, e]), (SUBLANES, LANES)))
        qm_ref[slot_a, c] = jnp.concatenate(qm, axis=1)
        on_ref[slot_a, c] = jnp.concatenate(o_n, axis=1)
        cd_ref[slot_a, c] = jnp.concatenate(cd, axis=1)

    state_ref[...] = state[0]


def _dn_chunk(qk, v, z, gate_c, gate_t, norm_w, batch, seq, n_kh, tc):
    m = qk.shape[0]
    nt = seq // tc
    cpt = tc // CHUNK
    assert cpt == 8, "the recurrence steps are interleaved with 8 preparation levels"

    def prep(b, s):
        return b * nt + jnp.minimum(s, nt - 1)

    def rec(b, s):
        return b * nt + jnp.maximum(s - 1, 0)

    return pl.pallas_call(
        _dn_chunk_kernel,
        grid=(batch, n_kh, nt + 1),
        in_specs=[
            pl.BlockSpec((tc, HEAD_DIM), lambda b, h, s: (prep(b, s), h)),
            pl.BlockSpec((tc, HEAD_DIM), lambda b, h, s: (prep(b, s), n_kh + h)),
            pl.BlockSpec((tc, 2 * HEAD_DIM), lambda b, h, s: (prep(b, s), h)),
            pl.BlockSpec((tc, 2 * HEAD_DIM), lambda b, h, s: (rec(b, s), h)),
            pl.BlockSpec((tc, gate_c.shape[1]), lambda b, h, s: (prep(b, s), 0)),
            pl.BlockSpec((None, cpt, SUBLANES, LANES), lambda b, h, s: (h, prep(b, s), 0, 0)),
            pl.BlockSpec((1, HEAD_DIM), lambda b, h, s: (0, 0)),
        ],
        out_specs=pl.BlockSpec((tc, 2 * HEAD_DIM), lambda b, h, s: (rec(b, s), h)),
        out_shape=jax.ShapeDtypeStruct((m, 2 * n_kh * HEAD_DIM), BF16),
        scratch_shapes=[pltpu.VMEM((HEAD_DIM, 2 * HEAD_DIM), F32),
                        pltpu.VMEM((2, cpt, CHUNK + HEAD_DIM, 2 * HEAD_DIM), BF16),
                        pltpu.VMEM((2, cpt, CHUNK + HEAD_DIM, 2 * HEAD_DIM), F32),
                        pltpu.VMEM((2, cpt, SUBLANES, 2 * LANES), F32)],
        compiler_params=_cparams(3),
        name="dn_chunk",
    )(qk, qk, v, z, gate_c, gate_t, norm_w)


def _tile(n, want):
    if n <= want:
        return n
    t = want - want % LANES
    while n % t:
        t -= LANES
    return t


def _gate_slots(w_beta_a, a_log, dt_bias, n_vh):
    n_kh = n_vh // 2
    d = w_beta_a.shape[0]
    wb = w_beta_a[:, :n_vh].reshape(d, n_kh, 2)
    wa = w_beta_a[:, n_vh:].reshape(d, n_kh, 2)
    w_slot = jnp.concatenate([wb, wa, jnp.zeros((d, n_kh, SUBLANES - 4), w_beta_a.dtype)], axis=2)
    zeros2 = jnp.zeros((n_kh, 2), F32)
    zeros4 = jnp.zeros((n_kh, SUBLANES - 4), F32)
    al = jnp.concatenate([zeros2, a_log.astype(F32).reshape(n_kh, 2), zeros4], axis=1)
    dt = jnp.concatenate([zeros2, dt_bias.astype(F32).reshape(n_kh, 2), zeros4], axis=1)
    pad = (-n_kh * SUBLANES) % LANES
    w_slot = jnp.pad(w_slot.reshape(d, n_kh * SUBLANES), ((0, 0), (0, pad)))
    return w_slot, jnp.pad(al.reshape(-1), (0, pad)), jnp.pad(dt.reshape(-1), (0, pad))


def kernel(x, sc_w_in, sc_conv_w, sc_w_out, dn_w_in, dn_conv_w, dn_a_log, dn_dt_bias, dn_norm_w,
           dn_w_out, ffn_w_gate_up, ffn_w_down, ln_gain, ln_bias):
    batch, seq, d = x.shape
    depth = ln_gain.shape[0]
    alpha = float((2 * depth) ** 0.25)
    m = batch * seq
    n_vh = dn_a_log.shape[1]
    n_kh = n_vh // 2
    key_dim = n_kh * HEAD_DIM
    val_dim = n_vh * HEAD_DIM
    qkv_dim = 2 * key_dim + val_dim
    hidden = ffn_w_down.shape[1]
    assert seq % CHUNK == 0 and d % LANES == 0 and dn_norm_w.shape[1] == HEAD_DIM

    tm_in = _tile(seq, 1024)
    tm_out = _tile(seq, 512)
    tn = _tile(d, 512)
    th = _tile(hidden, 512)
    tc = _tile(seq, 512)

    h = x.reshape(m, d)
    hb = h.astype(BF16)
    for i in range(depth):
        l = i // 2
        gb0 = jnp.stack([ln_gain[i, 0], ln_bias[i, 0]])
        gb1 = jnp.stack([ln_gain[i, 1], ln_bias[i, 1]])
        if i % 2 == 0:
            a = _sc_in(hb, sc_w_in[l].astype(BF16), sc_conv_w[l], seq, tm_in, tn)
            h = _proj_ln(a, sc_w_out[l].astype(BF16), h, gb0, alpha, tm_out)
        else:
            w_in = dn_w_in[l]
            w_qkvz = w_in[:, :qkv_dim + val_dim].astype(BF16)
            cw = dn_conv_w[l]
            qk = _dn_in(hb, w_qkvz, cw, 0, 2 * key_dim, seq, tm_in, tn, conv=True, l2norm=True,
                        n_scaled=key_dim // tn, name="dn_in_qk")
            v = _dn_in(hb, w_qkvz, cw, 2 * key_dim, val_dim, seq, tm_in, tn, conv=True, name="dn_in_v")
            z = _dn_in(hb, w_qkvz, cw, qkv_dim, val_dim, seq, tm_in, tn, conv=False, name="dn_in_z")
            w_slot, al_slot, dt_slot = _gate_slots(w_in[:, qkv_dim + val_dim:], dn_a_log[l],
                                                   dn_dt_bias[l], n_vh)
            gate_c, gate_t = _dn_gates(hb, w_slot.astype(BF16), al_slot, dt_slot, tm_out)
            gate_t = gate_t[:n_kh * SUBLANES].reshape(n_kh, SUBLANES, m // CHUNK, CHUNK)
            gate_t = jnp.transpose(gate_t, (0, 2, 1, 3))
            gate_t = jnp.concatenate([gate_t, gate_t], axis=-1)
            o = _dn_chunk(qk, v, z, gate_c, gate_t, dn_norm_w[l][None, :], batch, seq, n_kh, tc)
            h = _proj_ln(o, dn_w_out[l].astype(BF16), h, gb0, alpha, tm_out)
        h, hb = _ffn(h, ffn_w_gate_up[i].astype(BF16), ffn_w_down[i].astype(BF16), gb1, alpha, tm_out, th,
                     emit_bf16=i + 1 < depth)
    return h.reshape(batch, seq, d)
```

```python
import functools

import jax
import jax.numpy as jnp
from jax import lax
from jax.experimental import pallas as pl
from jax.experimental.pallas import tpu as pltpu

HEAD_DIM = 128
CHUNK = 64
SUBLANES = 8
LANES = 128
LN_EPS = 1e-5
RMS_EPS = 1e-6
NEG_BIG = -1e30
VMEM_LIMIT_BYTES = 60 * 1024 * 1024

F32 = jnp.float32
BF16 = jnp.bfloat16


def _cparams(n_axes):
    return pltpu.CompilerParams(
        dimension_semantics=("arbitrary",) * n_axes,
        vmem_limit_bytes=VMEM_LIMIT_BYTES)


def _dot(a, b):
    return jnp.dot(a, b, preferred_element_type=F32)


def _dot_nt(a, b):
    return lax.dot_general(a, b, (((1,), (1,)), ((), ())), preferred_element_type=F32)


def _dot_tn(a, b):
    return lax.dot_general(a, b, (((0,), (0,)), ((), ())), preferred_element_type=F32)


def _layer_norm(y, gain, bias):
    mu = jnp.mean(y, axis=-1, keepdims=True)
    d = y - mu
    var = jnp.mean(d * d, axis=-1, keepdims=True)
    return d * lax.rsqrt(var + LN_EPS) * gain + bias


def _silu(t):
    return t * jax.nn.sigmoid(t)


def _causal_conv(u, prev, w_ref):
    ksize = w_ref.shape[0]
    out = u * w_ref[ksize - 1:ksize, :]
    head = jnp.concatenate([prev, u[0:SUBLANES]], axis=0)
    for d in range(1, ksize):
        shifted = pltpu.roll(u, d, axis=0)
        top = pltpu.roll(head, d, axis=0)[SUBLANES:2 * SUBLANES]
        shifted = jnp.concatenate([top, shifted[SUBLANES:]], axis=0)
        out = out + shifted * w_ref[ksize - 1 - d:ksize - d, :]
    return out


def _causal_taps(ext, w_rows):
    k = len(w_rows)
    assert k <= SUBLANES
    s1 = pltpu.roll(ext, 1, axis=0)
    out = None
    for d0 in range(0, k, 2):
        term = ext * w_rows[k - 1 - d0]
        if d0 + 1 < k:
            term = term + s1 * w_rows[k - 2 - d0]
        if d0:
            term = pltpu.roll(term, d0, axis=0)
        out = term if out is None else out + term
    return out[SUBLANES:]


def _conv_history(carry_ref, j, i, tiles_per_seq):
    prev = carry_ref[j]
    return jnp.where(i % tiles_per_seq == 0, jnp.zeros_like(prev), prev)


def _sc_in_kernel(xb_ref, wb_ref, wc_ref, wh_ref, cw_ref, o_ref, carry_ref, *, tiles_per_seq):
    i, j = pl.program_id(0), pl.program_id(1)
    xb = xb_ref[...]
    u = _dot(xb, wc_ref[...]) * _dot(xb, wh_ref[...])
    prev = _conv_history(carry_ref, j, i, tiles_per_seq)
    y = _causal_conv(u, prev, cw_ref)
    carry_ref[j] = u[u.shape[0] - SUBLANES:]
    o_ref[...] = (_dot(xb, wb_ref[...]) * y).astype(o_ref.dtype)


def _sc_in(xb, w_in, conv_w, seq, tm, tn):
    m, d = xb.shape
    nd = d // tn
    return pl.pallas_call(
        functools.partial(_sc_in_kernel, tiles_per_seq=seq // tm),
        grid=(m // tm, nd),
        in_specs=[
            pl.BlockSpec((tm, d), lambda i, j: (i, 0)),
            pl.BlockSpec((d, tn), lambda i, j: (0, j)),
            pl.BlockSpec((d, tn), lambda i, j: (0, nd + j)),
            pl.BlockSpec((d, tn), lambda i, j: (0, 2 * nd + j)),
            pl.BlockSpec((conv_w.shape[0], tn), lambda i, j: (0, j)),
        ],
        out_specs=pl.BlockSpec((tm, tn), lambda i, j: (i, j)),
        out_shape=jax.ShapeDtypeStruct((m, d), BF16),
        scratch_shapes=[pltpu.VMEM((nd, SUBLANES, tn), F32)],
        compiler_params=_cparams(2),
        name="sc_in",
    )(xb, w_in, w_in, w_in, conv_w)


def _residual_ln_blocks(n_sub, rs, pre_norm, gb_ref, out_refs):
    gain, bias = gb_ref[0:1, :], gb_ref[1:2, :]

    def finish(b, y):
        out = _layer_norm(y, gain, bias)
        for r in out_refs:
            r[b * rs:(b + 1) * rs, :] = out.astype(r.dtype)

    pending = pre_norm(0)
    for b in range(1, n_sub):
        nxt = pre_norm(b)
        finish(b - 1, pending)
        pending = nxt
    finish(n_sub - 1, pending)


def _proj_ln_kernel(a_ref, w_ref, x_ref, gb_ref, o_ref, *, alpha, n_sub):
    rs = o_ref.shape[0] // n_sub

    def pre_norm(b):
        rows = slice(b * rs, (b + 1) * rs)
        return alpha * x_ref[rows, :] + _dot(a_ref[rows, :], w_ref[...])

    _residual_ln_blocks(n_sub, rs, pre_norm, gb_ref, [o_ref])


def _proj_ln(a, w, x, gain_bias, alpha, tm):
    m, d = x.shape
    k = a.shape[1]
    return pl.pallas_call(
        functools.partial(_proj_ln_kernel, alpha=alpha, n_sub=max(1, tm // 128)),
        grid=(m // tm,),
        in_specs=[
            pl.BlockSpec((tm, k), lambda i: (i, 0)),
            pl.BlockSpec((k, d), lambda i: (0, 0), pipeline_mode=pl.Buffered(1)),
            pl.BlockSpec((tm, d), lambda i: (i, 0)),
            pl.BlockSpec((2, d), lambda i: (0, 0)),
        ],
        out_specs=pl.BlockSpec((tm, d), lambda i: (i, 0)),
        out_shape=jax.ShapeDtypeStruct((m, d), F32),
        compiler_params=_cparams(1),
        name="proj_ln",
    )(a, w, x, gain_bias)


def _ffn_kernel(x_ref, wg_ref, wu_ref, wd_ref, gb_ref, o_ref, *rest, alpha, n_sub):
    *ob_ref, xb_ref, acc_ref = rest
    j = pl.program_id(1)

    @pl.when(j == 0)
    def _():
        xb_ref[...] = x_ref[...].astype(BF16)
        acc_ref[...] = jnp.zeros_like(acc_ref)

    xb = xb_ref[...]
    act = (_silu(_dot(xb, wg_ref[...])) * _dot(xb, wu_ref[...])).astype(BF16)
    last = pl.num_programs(1) - 1

    @pl.when(j < last)
    def _():
        acc_ref[...] += _dot(act, wd_ref[...])

    @pl.when(j == last)
    def _():
        rs = o_ref.shape[0] // n_sub

        def pre_norm(b):
            rows = slice(b * rs, (b + 1) * rs)
            return alpha * x_ref[rows, :] + (acc_ref[rows, :] + _dot(act[rows, :], wd_ref[...]))

        _residual_ln_blocks(n_sub, rs, pre_norm, gb_ref, [o_ref] + list(ob_ref))


def _ffn(x, w_gate_up, w_down, gain_bias, alpha, tm, th, emit_bf16):
    m, d = x.shape
    hidden = w_down.shape[0]
    nh = hidden // th
    row_block = pl.BlockSpec((tm, d), lambda i, j: (i, 0))
    outs = pl.pallas_call(
        functools.partial(_ffn_kernel, alpha=alpha, n_sub=max(1, tm // 128)),
        grid=(m // tm, nh),
        in_specs=[
            row_block,
            pl.BlockSpec((d, th), lambda i, j: (0, j)),
            pl.BlockSpec((d, th), lambda i, j: (0, nh + j)),
            pl.BlockSpec((th, d), lambda i, j: (j, 0)),
            pl.BlockSpec((2, d), lambda i, j: (0, 0)),
        ],
        out_specs=[row_block] * (2 if emit_bf16 else 1),
        out_shape=[jax.ShapeDtypeStruct((m, d), F32)] + [jax.ShapeDtypeStruct((m, d), BF16)] * emit_bf16,
        scratch_shapes=[pltpu.VMEM((tm, d), BF16), pltpu.VMEM((tm, d), F32)],
        compiler_params=_cparams(2),
        name="ffn",
    )(x, w_gate_up, w_gate_up, w_down, gain_bias)
    return outs if emit_bf16 else (outs[0], None)


def _dn_in_kernel(xb_ref, w_ref, cw_ref, o_ref, *scratch, tiles_per_seq, conv, l2norm, n_scaled, n_sub):
    i, j = pl.program_id(0), pl.program_id(1)
    tm = o_ref.shape[0]
    rs = tm // n_sub

    def block_dot(b):
        return _dot(xb_ref[b * rs:(b + 1) * rs, :], w_ref[...])

    if not conv:
        for b in range(n_sub):
            o_ref[b * rs:(b + 1) * rs, :] = block_dot(b).astype(o_ref.dtype)
        return

    acc0_ref, acc1_ref, carry_ref = scratch
    acc = (acc0_ref, acc1_ref)
    ksize = cw_ref.shape[0]

    @pl.when(i % tiles_per_seq == 0)
    def _():
        acc0_ref[0:SUBLANES, :] = jnp.zeros((SUBLANES, acc0_ref.shape[1]), F32)

    @pl.when(i % tiles_per_seq != 0)
    def _():
        acc0_ref[0:SUBLANES, :] = carry_ref[j]

    scale = jnp.where(j < n_scaled, HEAD_DIM ** -0.5, 1.0).astype(F32)

    def epilogue(b):
        src = acc[b % 2]
        for s in range(o_ref.shape[1] // HEAD_DIM):
            lanes = slice(s * HEAD_DIM, (s + 1) * HEAD_DIM)
            t = _silu(_causal_taps(src[:, lanes], [cw_ref[k:k + 1, lanes] for k in range(ksize)]))
            if l2norm:
                t = t * (lax.rsqrt(jnp.sum(t * t, axis=-1, keepdims=True) + RMS_EPS) * scale)
            o_ref[b * rs:(b + 1) * rs, lanes] = t.astype(o_ref.dtype)

    tail = None
    for b in range(n_sub):
        res = block_dot(b)
        acc[b % 2][SUBLANES:SUBLANES + rs, :] = res
        if b > 0:
            epilogue(b - 1)
            acc[b % 2][0:SUBLANES, :] = tail
        tail = res[rs - SUBLANES:, :]
    epilogue(n_sub - 1)
    carry_ref[j] = tail


def _dn_in(xb, w_in, conv_w, col0, width, seq, tm, tn, *, conv, l2norm=False, n_scaled=0, name):
    m, d = xb.shape
    nj = width // tn
    j0 = col0 // tn
    n_sub = max(1, tm // 128)
    blk = pltpu.VMEM((tm // n_sub + SUBLANES, tn), F32)
    scratch = [blk, blk, pltpu.VMEM((nj, SUBLANES, tn), F32)] if conv else []
    cw_j0 = j0 if conv else 0
    return pl.pallas_call(
        functools.partial(_dn_in_kernel, tiles_per_seq=seq // tm, conv=conv, l2norm=l2norm,
                          n_scaled=n_scaled, n_sub=n_sub),
        grid=(m // tm, nj),
        in_specs=[
            pl.BlockSpec((tm, d), lambda i, j: (i, 0)),
            pl.BlockSpec((d, tn), lambda i, j: (0, j0 + j)),
            pl.BlockSpec((conv_w.shape[0], tn), lambda i, j: (0, cw_j0 + j if conv else 0)),
        ],
        out_specs=pl.BlockSpec((tm, tn), lambda i, j: (i, j)),
        out_shape=jax.ShapeDtypeStruct((m, width), BF16),
        scratch_shapes=scratch,
        compiler_params=_cparams(2),
        name=name,
    )(xb, w_in, conv_w)


def _softplus(t):
    return jnp.maximum(t, 0.0) + jnp.log1p(jnp.exp(-jnp.abs(t)))


def _chunk_cumsum(g, axis):
    pos = lax.broadcasted_iota(jnp.int32, g.shape, axis) % CHUNK
    s = 1
    while s < CHUNK:
        g = g + jnp.where(pos >= s, pltpu.roll(g, s, axis=axis), 0.0)
        s *= 2
    return g


def _gates(raw, a_log, dt_bias, slot_axis, token_axis):
    c = lax.broadcasted_iota(jnp.int32, raw.shape, slot_axis) % SUBLANES
    beta = jax.nn.sigmoid(raw)
    g = -jnp.exp(a_log) * _softplus(raw + dt_bias)
    g = _chunk_cumsum(jnp.where(jnp.logical_and(c >= 2, c < 4), g, 0.0), token_axis)
    return jnp.where(c < 2, beta, g)


def _dn_gate_kernel(x_ref, w_ref, wt_ref, al_ref, dt_ref, alt_ref, dtt_ref, gc_ref, gt_ref):
    xb = x_ref[...]
    gc_ref[...] = _gates(_dot(xb, w_ref[...]), al_ref[...], dt_ref[...], 1, 0)
    gt_ref[...] = _gates(_dot_nt(wt_ref[...], xb), alt_ref[...], dtt_ref[...], 0, 1)


def _dn_gates(x, w_slot, a_log_slot, dt_slot, tm):
    m, d = x.shape
    ns = w_slot.shape[1]
    return pl.pallas_call(
        _dn_gate_kernel,
        grid=(m // tm,),
        in_specs=[
            pl.BlockSpec((tm, d), lambda i: (i, 0)),
            pl.BlockSpec((d, ns), lambda i: (0, 0)),
            pl.BlockSpec((ns, d), lambda i: (0, 0)),
            pl.BlockSpec((1, ns), lambda i: (0, 0)),
            pl.BlockSpec((1, ns), lambda i: (0, 0)),
            pl.BlockSpec((ns, 1), lambda i: (0, 0)),
            pl.BlockSpec((ns, 1), lambda i: (0, 0)),
        ],
        out_specs=[pl.BlockSpec((tm, ns), lambda i: (i, 0)),
                   pl.BlockSpec((ns, tm), lambda i: (0, i))],
        out_shape=[jax.ShapeDtypeStruct((m, ns), F32), jax.ShapeDtypeStruct((ns, m), F32)],
        compiler_params=_cparams(1),
        name="dn_gates",
    )(x, w_slot, w_slot.T, a_log_slot[None, :], dt_slot[None, :], a_log_slot[:, None], dt_slot[:, None])


def _block_diag(a, b):
    zero = jnp.zeros_like(a)
    return jnp.concatenate([jnp.concatenate([a, zero], axis=1), jnp.concatenate([zero, b], axis=1)], axis=0)


def _dn_chunk_kernel(q_ref, k_ref, v_ref, z_ref, gc_ref, gt_ref, nw_ref, o_ref,
                     state_ref, qm_ref, on_ref, cd_ref, *, nt, n_kh):
    g = pl.program_id(0)
    kh = (jnp.minimum(g, pl.num_programs(0) - 2) // nt) % n_kh
    n_chunks = q_ref.shape[0] // CHUNK
    slot_a = g % 2
    slot_b = 1 - slot_a

    @pl.when(g == 0)
    def _():
        qm_ref[1] = jnp.zeros(qm_ref.shape[1:], qm_ref.dtype)
        on_ref[1] = jnp.zeros(on_ref.shape[1:], on_ref.dtype)
        cd_ref[1] = jnp.zeros(cd_ref.shape[1:], cd_ref.dtype)

    @pl.when(jnp.maximum(g - 1, 0) % nt == 0)
    def _():
        state_ref[...] = jnp.zeros_like(state_ref)

    row = lax.broadcasted_iota(jnp.int32, (CHUNK, LANES), 0)
    lane = lax.broadcasted_iota(jnp.int32, (CHUNK, LANES), 1)
    col = lane % CHUNK
    causal = col <= row
    strict = col < row
    upper_half = lane >= CHUNK
    lower_half = lane < CHUNK
    eye = col == row
    eye_hi = jnp.where(jnp.logical_and(upper_half, eye), 1.0, 0.0).astype(F32)
    eye_lo = jnp.where(jnp.logical_and(lower_half, eye), 1.0, 0.0).astype(F32)
    norm_w = nw_ref[...]
    state = [state_ref[...]]

    def recurrence(c):
        rows = slice(c * CHUNK, (c + 1) * CHUNK)
        s_b = state[0].astype(BF16)
        qms = _dot(qm_ref[slot_b, c], _block_diag(s_b[:, :HEAD_DIM], s_b[:, HEAD_DIM:]))
        o_n = on_ref[slot_b, c]
        o_all = qms[:CHUNK] + o_n[:CHUNK]
        state[0] = state[0] * cd_ref[slot_b, c][:1, :] + o_n[CHUNK:] - qms[CHUNK:]
        for e in range(2):
            o = o_all[:, e * HEAD_DIM:(e + 1) * HEAD_DIM]
            o = o * lax.rsqrt(jnp.mean(o * o, axis=-1, keepdims=True) + RMS_EPS) * norm_w
            ze = z_ref[rows, e * HEAD_DIM:(e + 1) * HEAD_DIM].astype(F32)
            o_ref[rows, e * HEAD_DIM:(e + 1) * HEAD_DIM] = (o * _silu(ze)).astype(o_ref.dtype)

    recurrence(0)
    gsel_all = pltpu.roll(gc_ref[...], (LANES - SUBLANES * kh) % LANES, axis=1)
    kf, qf, k_t, qk2, kk2 = [], [], [], [], []
    for c in range(n_chunks):
        rows = slice(c * CHUNK, (c + 1) * CHUNK)
        qc, kc = q_ref[rows, :], k_ref[rows, :]
        qf.append(qc.astype(F32))
        kf.append(kc.astype(F32))
        k_t.append(jnp.transpose(kf[c]).astype(BF16))
        qkk = _dot_nt(jnp.concatenate([qc, kc], axis=0), jnp.concatenate([kc, kc], axis=0))
        qk2.append(qkk[:CHUNK])
        kk2.append(qkk[CHUNK:])
    recurrence(1)

    chains = [(c, e) for c in range(n_chunks) for e in range(2)]
    beta_col, g_col, g_last, decay, x_mat, w_mat = {}, {}, {}, {}, {}, {}
    for (c, e) in chains:
        gsel = gsel_all[c * CHUNK:(c + 1) * CHUNK]
        g_row = gt_ref[c][2 + e:3 + e, :]
        beta_col[c, e] = jnp.broadcast_to(gsel[:, e:e + 1], (CHUNK, LANES))
        g_col[c, e] = jnp.broadcast_to(gsel[:, 2 + e:3 + e], (CHUNK, LANES))
        g_last[c, e] = g_row[:, CHUNK - 1:CHUNK]
        decay[c, e] = jnp.exp(jnp.where(causal, g_col[c, e] - g_row, NEG_BIG))
        x_mat[c, e] = jnp.where(strict, -(kk2[c] * beta_col[c, e] * decay[c, e]), 0.0)
    for c in range(n_chunks):
        xa, xb = x_mat[c, 0], x_mat[c, 1]
        sq = _dot(jnp.where(lower_half, xa, xb).astype(BF16), _block_diag(xa.astype(BF16), xb.astype(BF16)))
        w_mat[c, 0] = jnp.where(upper_half, xa + eye_hi, sq[:, :LANES])
        w_mat[c, 1] = jnp.where(upper_half, sq[:, LANES:], xb + eye_lo)
    recurrence(2)
    for level in range(5):
        for c in range(n_chunks):
            wa, wb = w_mat[c, 0], w_mat[c, 1]
            pw = _dot(jnp.where(lower_half, wa, wb).astype(BF16), _block_diag(wa.astype(BF16), wb.astype(BF16)))
            w_mat[c, 0] = pw[:, :LANES] + jnp.where(upper_half, wa, 0.0)
            w_mat[c, 1] = pw[:, LANES:] + jnp.where(lower_half, wb, 0.0)
        if level < 4:
            recurrence(3 + level)
    sol, exp_g = {}, {}
    for (c, e) in chains:
        t_mat = w_mat[c, e][:, CHUNK:] if e == 0 else w_mat[c, e][:, :CHUNK]
        ve = v_ref[c * CHUNK:(c + 1) * CHUNK, e * HEAD_DIM:(e + 1) * HEAD_DIM].astype(F32)
        exp_g[c, e] = jnp.exp(g_col[c, e])
        rhs = jnp.concatenate([kf[c] * (beta_col[c, e] * exp_g[c, e]), ve * beta_col[c, e]], axis=1)
        sol[c, e] = _dot(t_mat.astype(BF16), rhs.astype(BF16))
    recurrence(7)
    for c in range(n_chunks):
        qm, o_n, cd = [], [], []
        for e in range(2):
            a_qk = (qk2[c] * decay[c, e])[:, :CHUNK].astype(BF16)
            qo = _dot(a_qk, sol[c, e].astype(BF16))
            q_eff = qf[c] * exp_g[c, e] - qo[:, :HEAD_DIM]
            k_tail = jnp.exp(g_last[c, e] - g_col[c, e])[:, :1]
            mn = _dot(k_t[c], (sol[c, e] * k_tail).astype(BF16))
            qm.append(jnp.concatenate([q_eff, mn[:, :HEAD_DIM]], axis=0).astype(BF16))
            o_n.append(jnp.concatenate([qo[:, HEAD_DIM:], mn[:, HEAD_DIM:]], axis=0))
            cd.append(jnp.broadcast_to(jnp.exp(g_last[c, e]), (SUBLANES, LANES)))
        qm_ref[slot_a, c] = jnp.concatenate(qm, axis=1)
        on_ref[slot_a, c] = jnp.concatenate(o_n, axis=1)
        cd_ref[slot_a, c] = jnp.concatenate(cd, axis=1)

    state_ref[...] = state[0]


def _dn_chunk(qk, v, z, gate_c, gate_t, norm_w, batch, seq, n_kh, tc):
    m = qk.shape[0]
    nt = seq // tc
    cpt = tc // CHUNK
    assert cpt == 8, "the recurrence steps are interleaved with 8 preparation levels"

    n_tiles = batch * n_kh * nt

    def split(t):
        return (t // (n_kh * nt)) * nt + t % nt, (t // nt) % n_kh

    def prep(g):
        return split(jnp.minimum(g, n_tiles - 1))

    def rec(g):
        return split(jnp.maximum(g - 1, 0))

    return pl.pallas_call(
        functools.partial(_dn_chunk_kernel, nt=nt, n_kh=n_kh),
        grid=(n_tiles + 1,),
        in_specs=[
            pl.BlockSpec((tc, HEAD_DIM), lambda g: prep(g)),
            pl.BlockSpec((tc, HEAD_DIM), lambda g: (prep(g)[0], n_kh + prep(g)[1])),
            pl.BlockSpec((tc, 2 * HEAD_DIM), lambda g: prep(g)),
            pl.BlockSpec((tc, 2 * HEAD_DIM), lambda g: rec(g)),
            pl.BlockSpec((tc, gate_c.shape[1]), lambda g: (prep(g)[0], 0)),
            pl.BlockSpec((None, cpt, SUBLANES, LANES), lambda g: (prep(g)[1], prep(g)[0], 0, 0)),
            pl.BlockSpec((1, HEAD_DIM), lambda g: (0, 0)),
        ],
        out_specs=pl.BlockSpec((tc, 2 * HEAD_DIM), lambda g: rec(g)),
        out_shape=jax.ShapeDtypeStruct((m, 2 * n_kh * HEAD_DIM), BF16),
        scratch_shapes=[pltpu.VMEM((HEAD_DIM, 2 * HEAD_DIM), F32),
                        pltpu.VMEM((2, cpt, CHUNK + HEAD_DIM, 2 * HEAD_DIM), BF16),
                        pltpu.VMEM((2, cpt, CHUNK + HEAD_DIM, 2 * HEAD_DIM), F32),
                        pltpu.VMEM((2, cpt, SUBLANES, 2 * LANES), F32)],
        compiler_params=_cparams(1),
        name="dn_chunk",
    )(qk, qk, v, z, gate_c, gate_t, norm_w)


def _tile(n, want):
    if n <= want:
        return n
    t = want - want % LANES
    while n % t:
        t -= LANES
    return t


def _gate_slots(w_beta_a, a_log, dt_bias, n_vh):
    n_kh = n_vh // 2
    d = w_beta_a.shape[0]
    wb = w_beta_a[:, :n_vh].reshape(d, n_kh, 2)
    wa = w_beta_a[:, n_vh:].reshape(d, n_kh, 2)
    w_slot = jnp.concatenate([wb, wa, jnp.zeros((d, n_kh, SUBLANES - 4), w_beta_a.dtype)], axis=2)
    zeros2 = jnp.zeros((n_kh, 2), F32)
    zeros4 = jnp.zeros((n_kh, SUBLANES - 4), F32)
    al = jnp.concatenate([zeros2, a_log.astype(F32).reshape(n_kh, 2), zeros4], axis=1)
    dt = jnp.concatenate([zeros2, dt_bias.astype(F32).reshape(n_kh, 2), zeros4], axis=1)
    pad = (-n_kh * SUBLANES) % LANES
    w_slot = jnp.pad(w_slot.reshape(d, n_kh * SUBLANES), ((0, 0), (0, pad)))
    return w_slot, jnp.pad(al.reshape(-1), (0, pad)), jnp.pad(dt.reshape(-1), (0, pad))


def kernel(x, sc_w_in, sc_conv_w, sc_w_out, dn_w_in, dn_conv_w, dn_a_log, dn_dt_bias, dn_norm_w,
           dn_w_out, ffn_w_gate_up, ffn_w_down, ln_gain, ln_bias):
    batch, seq, d = x.shape
    depth = ln_gain.shape[0]
    alpha = float((2 * depth) ** 0.25)
    m = batch * seq
    n_vh = dn_a_log.shape[1]
    n_kh = n_vh // 2
    key_dim = n_kh * HEAD_DIM
    val_dim = n_vh * HEAD_DIM
    qkv_dim = 2 * key_dim + val_dim
    hidden = ffn_w_down.shape[1]
    assert seq % CHUNK == 0 and d % LANES == 0 and dn_norm_w.shape[1] == HEAD_DIM

    tm_in = _tile(seq, 1024)
    tm_dn = _tile(seq, 2048)
    tm_out = _tile(seq, 512)
    tn = _tile(d, 512)
    th = _tile(hidden, 512)
    tc = _tile(seq, 512)

    h = x.reshape(m, d)
    hb = h.astype(BF16)
    for i in range(depth):
        l = i // 2
        gb0 = jnp.stack([ln_gain[i, 0], ln_bias[i, 0]])
        gb1 = jnp.stack([ln_gain[i, 1], ln_bias[i, 1]])
        if i % 2 == 0:
            a = _sc_in(hb, sc_w_in[l].astype(BF16), sc_conv_w[l], seq, tm_in, tn)
            h = _proj_ln(a, sc_w_out[l].astype(BF16), h, gb0, alpha, tm_out)
        else:
            w_in = dn_w_in[l]
            w_qkvz = w_in[:, :qkv_dim + val_dim].astype(BF16)
            cw = dn_conv_w[l]
            qk = _dn_in(hb, w_qkvz, cw, 0, 2 * key_dim, seq, tm_dn, tn, conv=True, l2norm=True,
                        n_scaled=key_dim // tn, name="dn_in_qk")
            v = _dn_in(hb, w_qkvz, cw, 2 * key_dim, val_dim, seq, tm_dn, tn, conv=True, name="dn_in_v")
            z = _dn_in(hb, w_qkvz, cw, qkv_dim, val_dim, seq, tm_dn, tn, conv=False, name="dn_in_z")
            w_slot, al_slot, dt_slot = _gate_slots(w_in[:, qkv_dim + val_dim:], dn_a_log[l],
                                                   dn_dt_bias[l], n_vh)
            gate_c, gate_t = _dn_gates(hb, w_slot.astype(BF16), al_slot, dt_slot, tm_out)
            gate_t = gate_t[:n_kh * SUBLANES].reshape(n_kh, SUBLANES, m // CHUNK, CHUNK)
            gate_t = jnp.transpose(gate_t, (0, 2, 1, 3))
            gate_t = jnp.concatenate([gate_t, gate_t], axis=-1)
            o = _dn_chunk(qk, v, z, gate_c, gate_t, dn_norm_w[l][None, :], batch, seq, n_kh, tc)
            h = _proj_ln(o, dn_w_out[l].astype(BF16), h, gb0, alpha, tm_out)
        h, hb = _ffn(h, ffn_w_gate_up[i].astype(BF16), ffn_w_down[i].astype(BF16), gb1, alpha, tm_out, th,
                     emit_bf16=i + 1 < depth)
    return h.reshape(batch, seq, d)
```

```python
import functools

import jax
import jax.numpy as jnp
from jax import lax
from jax.experimental import pallas as pl
from jax.experimental.pallas import tpu as pltpu

HEAD_DIM = 128
CHUNK = 64
SUBLANES = 8
LANES = 128
LN_EPS = 1e-5
RMS_EPS = 1e-6
NEG_BIG = -1e30
VMEM_LIMIT_BYTES = 60 * 1024 * 1024

F32 = jnp.float32
BF16 = jnp.bfloat16


def _cparams(n_axes):
    return pltpu.CompilerParams(
        dimension_semantics=("arbitrary",) * n_axes,
        vmem_limit_bytes=VMEM_LIMIT_BYTES)


def _dot(a, b):
    return jnp.dot(a, b, preferred_element_type=F32)


def _dot_nt(a, b):
    return lax.dot_general(a, b, (((1,), (1,)), ((), ())), preferred_element_type=F32)


def _dot_tn(a, b):
    return lax.dot_general(a, b, (((0,), (0,)), ((), ())), preferred_element_type=F32)


def _layer_norm(y, gain, bias):
    mu = jnp.mean(y, axis=-1, keepdims=True)
    d = y - mu
    var = jnp.mean(d * d, axis=-1, keepdims=True)
    return d * lax.rsqrt(var + LN_EPS) * gain + bias


def _silu(t):
    return t * jax.nn.sigmoid(t)


def _causal_conv(u, prev, w_ref):
    ksize = w_ref.shape[0]
    out = u * w_ref[ksize - 1:ksize, :]
    head = jnp.concatenate([prev, u[0:SUBLANES]], axis=0)
    for d in range(1, ksize):
        shifted = pltpu.roll(u, d, axis=0)
        top = pltpu.roll(head, d, axis=0)[SUBLANES:2 * SUBLANES]
        shifted = jnp.concatenate([top, shifted[SUBLANES:]], axis=0)
        out = out + shifted * w_ref[ksize - 1 - d:ksize - d, :]
    return out


def _causal_taps(ext, w_rows):
    k = len(w_rows)
    assert k <= SUBLANES
    s1 = pltpu.roll(ext, 1, axis=0)
    out = None
    for d0 in range(0, k, 2):
        term = ext * w_rows[k - 1 - d0]
        if d0 + 1 < k:
            term = term + s1 * w_rows[k - 2 - d0]
        if d0:
            term = pltpu.roll(term, d0, axis=0)
        out = term if out is None else out + term
    return out[SUBLANES:]


def _conv_history(carry_ref, j, i, tiles_per_seq):
    prev = carry_ref[j]
    return jnp.where(i % tiles_per_seq == 0, jnp.zeros_like(prev), prev)


def _sc_in_kernel(xb_ref, wb_ref, wc_ref, wh_ref, cw_ref, o_ref, carry_ref, *, tiles_per_seq):
    i, j = pl.program_id(0), pl.program_id(1)
    xb = xb_ref[...]
    u = _dot(xb, wc_ref[...]) * _dot(xb, wh_ref[...])
    prev = _conv_history(carry_ref, j, i, tiles_per_seq)
    y = _causal_conv(u, prev, cw_ref)
    carry_ref[j] = u[u.shape[0] - SUBLANES:]
    o_ref[...] = (_dot(xb, wb_ref[...]) * y).astype(o_ref.dtype)


def _sc_in(xb, w_in, conv_w, seq, tm, tn):
    m, d = xb.shape
    nd = d // tn
    return pl.pallas_call(
        functools.partial(_sc_in_kernel, tiles_per_seq=seq // tm),
        grid=(m // tm, nd),
        in_specs=[
            pl.BlockSpec((tm, d), lambda i, j: (i, 0)),
            pl.BlockSpec((d, tn), lambda i, j: (0, j)),
            pl.BlockSpec((d, tn), lambda i, j: (0, nd + j)),
            pl.BlockSpec((d, tn), lambda i, j: (0, 2 * nd + j)),
            pl.BlockSpec((conv_w.shape[0], tn), lambda i, j: (0, j)),
        ],
        out_specs=pl.BlockSpec((tm, tn), lambda i, j: (i, j)),
        out_shape=jax.ShapeDtypeStruct((m, d), BF16),
        scratch_shapes=[pltpu.VMEM((nd, SUBLANES, tn), F32)],
        compiler_params=_cparams(2),
        name="sc_in",
    )(xb, w_in, w_in, w_in, conv_w)


def _residual_ln_blocks(n_sub, rs, pre_norm, gb_ref, out_refs):
    gain, bias = gb_ref[0:1, :], gb_ref[1:2, :]

    def finish(b, y):
        out = _layer_norm(y, gain, bias)
        for r in out_refs:
            r[b * rs:(b + 1) * rs, :] = out.astype(r.dtype)

    pending = pre_norm(0)
    for b in range(1, n_sub):
        nxt = pre_norm(b)
        finish(b - 1, pending)
        pending = nxt
    finish(n_sub - 1, pending)


def _proj_ln_kernel(a_ref, w_ref, x_ref, gb_ref, o_ref, *, alpha, n_sub):
    rs = o_ref.shape[0] // n_sub

    def pre_norm(b):
        rows = slice(b * rs, (b + 1) * rs)
        return alpha * x_ref[rows, :] + _dot(a_ref[rows, :], w_ref[...])

    _residual_ln_blocks(n_sub, rs, pre_norm, gb_ref, [o_ref])


def _proj_ln(a, w, x, gain_bias, alpha, tm):
    m, d = x.shape
    k = a.shape[1]
    return pl.pallas_call(
        functools.partial(_proj_ln_kernel, alpha=alpha, n_sub=max(1, tm // 128)),
        grid=(m // tm,),
        in_specs=[
            pl.BlockSpec((tm, k), lambda i: (i, 0)),
            pl.BlockSpec((k, d), lambda i: (0, 0), pipeline_mode=pl.Buffered(1)),
            pl.BlockSpec((tm, d), lambda i: (i, 0)),
            pl.BlockSpec((2, d), lambda i: (0, 0)),
        ],
        out_specs=pl.BlockSpec((tm, d), lambda i: (i, 0)),
        out_shape=jax.ShapeDtypeStruct((m, d), F32),
        compiler_params=_cparams(1),
        name="proj_ln",
    )(a, w, x, gain_bias)


def _ffn_kernel(x_ref, wg_ref, wu_ref, wd_ref, gb_ref, o_ref, *rest, alpha, n_sub):
    *ob_ref, xb_ref = rest
    j = pl.program_id(1)

    @pl.when(j == 0)
    def _():
        xb_ref[...] = x_ref[...].astype(BF16)
        o_ref[...] = jnp.zeros_like(o_ref)

    xb = xb_ref[...]
    act = _silu(_dot(xb, wg_ref[...])) * _dot(xb, wu_ref[...])
    o_ref[...] += _dot(act.astype(BF16), wd_ref[...])

    @pl.when(j == pl.num_programs(1) - 1)
    def _():
        rs = o_ref.shape[0] // n_sub

        def pre_norm(b):
            rows = slice(b * rs, (b + 1) * rs)
            return alpha * x_ref[rows, :] + o_ref[rows, :]

        _residual_ln_blocks(n_sub, rs, pre_norm, gb_ref, [o_ref] + list(ob_ref))


def _ffn(x, w_gate_up, w_down, gain_bias, alpha, tm, th, emit_bf16):
    m, d = x.shape
    hidden = w_down.shape[0]
    nh = hidden // th
    row_block = pl.BlockSpec((tm, d), lambda i, j: (i, 0))
    outs = pl.pallas_call(
        functools.partial(_ffn_kernel, alpha=alpha, n_sub=max(1, tm // 128)),
        grid=(m // tm, nh),
        in_specs=[
            pl.BlockSpec((tm, d), lambda i, j: (i, 0), pipeline_mode=pl.Buffered(1)),
            pl.BlockSpec((d, th), lambda i, j: (0, j)),
            pl.BlockSpec((d, th), lambda i, j: (0, nh + j)),
            pl.BlockSpec((th, d), lambda i, j: (j, 0)),
            pl.BlockSpec((2, d), lambda i, j: (0, 0)),
        ],
        out_specs=[row_block] * (2 if emit_bf16 else 1),
        out_shape=[jax.ShapeDtypeStruct((m, d), F32)] + [jax.ShapeDtypeStruct((m, d), BF16)] * emit_bf16,
        scratch_shapes=[pltpu.VMEM((tm, d), BF16)],
        compiler_params=_cparams(2),
        name="ffn",
    )(x, w_gate_up, w_gate_up, w_down, gain_bias)
    return outs if emit_bf16 else (outs[0], None)


def _dn_in_kernel(xb_ref, w_ref, cw_ref, o_ref, *scratch, tiles_per_seq, conv, l2norm, n_scaled, n_sub):
    i, j = pl.program_id(0), pl.program_id(1)
    tm = o_ref.shape[0]
    rs = tm // n_sub

    def block_dot(b):
        return _dot(xb_ref[b * rs:(b + 1) * rs, :], w_ref[...])

    if not conv:
        for b in range(n_sub):
            o_ref[b * rs:(b + 1) * rs, :] = block_dot(b).astype(o_ref.dtype)
        return

    acc0_ref, acc1_ref, carry_ref = scratch
    acc = (acc0_ref, acc1_ref)
    ksize = cw_ref.shape[0]

    @pl.when(i % tiles_per_seq == 0)
    def _():
        acc0_ref[0:SUBLANES, :] = jnp.zeros((SUBLANES, acc0_ref.shape[1]), F32)

    @pl.when(i % tiles_per_seq != 0)
    def _():
        acc0_ref[0:SUBLANES, :] = carry_ref[j]

    scale = jnp.where(j < n_scaled, HEAD_DIM ** -0.5, 1.0).astype(F32)

    def epilogue(b):
        src = acc[b % 2]
        for s in range(o_ref.shape[1] // HEAD_DIM):
            lanes = slice(s * HEAD_DIM, (s + 1) * HEAD_DIM)
            t = _silu(_causal_taps(src[:, lanes], [cw_ref[k:k + 1, lanes] for k in range(ksize)]))
            if l2norm:
                t = t * (lax.rsqrt(jnp.sum(t * t, axis=-1, keepdims=True) + RMS_EPS) * scale)
            o_ref[b * rs:(b + 1) * rs, lanes] = t.astype(o_ref.dtype)

    tail = None
    for b in range(n_sub):
        res = block_dot(b)
        acc[b % 2][SUBLANES:SUBLANES + rs, :] = res
        if b > 0:
            epilogue(b - 1)
            acc[b % 2][0:SUBLANES, :] = tail
        tail = res[rs - SUBLANES:, :]
    epilogue(n_sub - 1)
    carry_ref[j] = tail


def _dn_in(xb, w_in, conv_w, col0, width, seq, tm, tn, *, conv, l2norm=False, n_scaled=0, name):
    m, d = xb.shape
    nj = width // tn
    j0 = col0 // tn
    n_sub = max(1, tm // 128)
    blk = pltpu.VMEM((tm // n_sub + SUBLANES, tn), F32)
    scratch = [blk, blk, pltpu.VMEM((nj, SUBLANES, tn), F32)] if conv else []
    cw_j0 = j0 if conv else 0
    return pl.pallas_call(
        functools.partial(_dn_in_kernel, tiles_per_seq=seq // tm, conv=conv, l2norm=l2norm,
                          n_scaled=n_scaled, n_sub=n_sub),
        grid=(m // tm, nj),
        in_specs=[
            pl.BlockSpec((tm, d), lambda i, j: (i, 0)),
            pl.BlockSpec((d, tn), lambda i, j: (0, j0 + j)),
            pl.BlockSpec((conv_w.shape[0], tn), lambda i, j: (0, cw_j0 + j if conv else 0)),
        ],
        out_specs=pl.BlockSpec((tm, tn), lambda i, j: (i, j)),
        out_shape=jax.ShapeDtypeStruct((m, width), BF16),
        scratch_shapes=scratch,
        compiler_params=_cparams(2),
        name=name,
    )(xb, w_in, conv_w)


def _softplus(t):
    return jnp.maximum(t, 0.0) + jnp.log1p(jnp.exp(-jnp.abs(t)))


def _chunk_cumsum(g, axis):
    pos = lax.broadcasted_iota(jnp.int32, g.shape, axis) % CHUNK
    s = 1
    while s < CHUNK:
        g = g + jnp.where(pos >= s, pltpu.roll(g, s, axis=axis), 0.0)
        s *= 2
    return g


def _gates(raw, a_log, dt_bias, slot_axis, token_axis):
    c = lax.broadcasted_iota(jnp.int32, raw.shape, slot_axis) % SUBLANES
    beta = jax.nn.sigmoid(raw)
    g = -jnp.exp(a_log) * _softplus(raw + dt_bias)
    g = _chunk_cumsum(jnp.where(jnp.logical_and(c >= 2, c < 4), g, 0.0), token_axis)
    return jnp.where(c < 2, beta, g)


def _dn_gate_kernel(x_ref, w_ref, wt_ref, al_ref, dt_ref, alt_ref, dtt_ref, gc_ref, gt_ref):
    xb = x_ref[...]
    gc_ref[...] = _gates(_dot(xb, w_ref[...]), al_ref[...], dt_ref[...], 1, 0)
    gt_ref[...] = _gates(_dot_nt(wt_ref[...], xb), alt_ref[...], dtt_ref[...], 0, 1)


def _dn_gates(x, w_slot, a_log_slot, dt_slot, tm):
    m, d = x.shape
    ns = w_slot.shape[1]
    return pl.pallas_call(
        _dn_gate_kernel,
        grid=(m // tm,),
        in_specs=[
            pl.BlockSpec((tm, d), lambda i: (i, 0)),
            pl.BlockSpec((d, ns), lambda i: (0, 0)),
            pl.BlockSpec((ns, d), lambda i: (0, 0)),
            pl.BlockSpec((1, ns), lambda i: (0, 0)),
            pl.BlockSpec((1, ns), lambda i: (0, 0)),
            pl.BlockSpec((ns, 1), lambda i: (0, 0)),
            pl.BlockSpec((ns, 1), lambda i: (0, 0)),
        ],
        out_specs=[pl.BlockSpec((tm, ns), lambda i: (i, 0)),
                   pl.BlockSpec((ns, tm), lambda i: (0, i))],
        out_shape=[jax.ShapeDtypeStruct((m, ns), F32), jax.ShapeDtypeStruct((ns, m), F32)],
        compiler_params=_cparams(1),
        name="dn_gates",
    )(x, w_slot, w_slot.T, a_log_slot[None, :], dt_slot[None, :], a_log_slot[:, None], dt_slot[:, None])


def _block_diag(a, b):
    zero = jnp.zeros_like(a)
    return jnp.concatenate([jnp.concatenate([a, zero], axis=1), jnp.concatenate([zero, b], axis=1)], axis=0)


def _dn_chunk_kernel(q_ref, k_ref, v_ref, z_ref, gc_ref, gt_ref, nw_ref, o_ref,
                     state_ref, qm_ref, on_ref, cd_ref, *, nt, n_kh):
    g = pl.program_id(0)
    kh = (jnp.minimum(g, pl.num_programs(0) - 2) // nt) % n_kh
    n_chunks = q_ref.shape[0] // CHUNK
    slot_a = g % 2
    slot_b = 1 - slot_a

    @pl.when(g == 0)
    def _():
        qm_ref[1] = jnp.zeros(qm_ref.shape[1:], qm_ref.dtype)
        on_ref[1] = jnp.zeros(on_ref.shape[1:], on_ref.dtype)
        cd_ref[1] = jnp.zeros(cd_ref.shape[1:], cd_ref.dtype)

    @pl.when(jnp.maximum(g - 1, 0) % nt == 0)
    def _():
        state_ref[...] = jnp.zeros_like(state_ref)

    row = lax.broadcasted_iota(jnp.int32, (CHUNK, LANES), 0)
    lane = lax.broadcasted_iota(jnp.int32, (CHUNK, LANES), 1)
    col = lane % CHUNK
    causal = col <= row
    strict = col < row
    upper_half = lane >= CHUNK
    lower_half = lane < CHUNK
    eye = col == row
    eye_hi = jnp.where(jnp.logical_and(upper_half, eye), 1.0, 0.0).astype(F32)
    eye_lo = jnp.where(jnp.logical_and(lower_half, eye), 1.0, 0.0).astype(F32)
    norm_w = nw_ref[...]
    state = [state_ref[...]]

    def recurrence(c):
        rows = slice(c * CHUNK, (c + 1) * CHUNK)
        s_b = state[0].astype(BF16)
        qms = _dot(qm_ref[slot_b, c], _block_diag(s_b[:, :HEAD_DIM], s_b[:, HEAD_DIM:]))
        o_n = on_ref[slot_b, c]
        o_all = qms[:CHUNK] + o_n[:CHUNK]
        state[0] = state[0] * cd_ref[slot_b, c][:1, :] + o_n[CHUNK:] - qms[CHUNK:]
        for e in range(2):
            o = o_all[:, e * HEAD_DIM:(e + 1) * HEAD_DIM]
            o = o * lax.rsqrt(jnp.mean(o * o, axis=-1, keepdims=True) + RMS_EPS) * norm_w
            ze = z_ref[rows, e * HEAD_DIM:(e + 1) * HEAD_DIM].astype(F32)
            o_ref[rows, e * HEAD_DIM:(e + 1) * HEAD_DIM] = (o * _silu(ze)).astype(o_ref.dtype)

    recurrence(0)
    gsel_all = pltpu.roll(gc_ref[...], (LANES - SUBLANES * kh) % LANES, axis=1)
    kf, qf, k_t, qk2, kk2 = [], [], [], [], []
    for c in range(n_chunks):
        rows = slice(c * CHUNK, (c + 1) * CHUNK)
        qc, kc = q_ref[rows, :], k_ref[rows, :]
        qf.append(qc.astype(F32))
        kf.append(kc.astype(F32))
        k_t.append(jnp.transpose(kf[c]).astype(BF16))
        qkk = _dot_nt(jnp.concatenate([qc, kc], axis=0), jnp.concatenate([kc, kc], axis=0))
        qk2.append(qkk[:CHUNK])
        kk2.append(qkk[CHUNK:])
    recurrence(1)

    chains = [(c, e) for c in range(n_chunks) for e in range(2)]
    beta_col, g_col, g_last, decay, x_mat, w_mat = {}, {}, {}, {}, {}, {}
    for (c, e) in chains:
        gsel = gsel_all[c * CHUNK:(c + 1) * CHUNK]
        g_row = gt_ref[c][2 + e:3 + e, :]
        beta_col[c, e] = jnp.broadcast_to(gsel[:, e:e + 1], (CHUNK, LANES))
        g_col[c, e] = jnp.broadcast_to(gsel[:, 2 + e:3 + e], (CHUNK, LANES))
        g_last[c, e] = g_row[:, CHUNK - 1:CHUNK]
        decay[c, e] = jnp.exp(jnp.where(causal, g_col[c, e] - g_row, NEG_BIG))
        x_mat[c, e] = jnp.where(strict, -(kk2[c] * beta_col[c, e] * decay[c, e]), 0.0)
    for c in range(n_chunks):
        xa, xb = x_mat[c, 0], x_mat[c, 1]
        sq = _dot(jnp.where(lower_half, xa, xb).astype(BF16), _block_diag(xa.astype(BF16), xb.astype(BF16)))
        w_mat[c, 0] = jnp.where(upper_half, xa + eye_hi, sq[:, :LANES])
        w_mat[c, 1] = jnp.where(upper_half, sq[:, LANES:], xb + eye_lo)
    recurrence(2)
    for level in range(5):
        for c in range(n_chunks):
            wa, wb = w_mat[c, 0], w_mat[c, 1]
            pw = _dot(jnp.where(lower_half, wa, wb).astype(BF16), _block_diag(wa.astype(BF16), wb.astype(BF16)))
            w_mat[c, 0] = pw[:, :LANES] + jnp.where(upper_half, wa, 0.0)
            w_mat[c, 1] = pw[:, LANES:] + jnp.where(lower_half, wb, 0.0)
        if level < 4:
            recurrence(3 + level)
    sol, exp_g = {}, {}
    for (c, e) in chains:
        t_mat = w_mat[c, e][:, CHUNK:] if e == 0 else w_mat[c, e][:, :CHUNK]
        ve = v_ref[c * CHUNK:(c + 1) * CHUNK, e * HEAD_DIM:(e + 1) * HEAD_DIM].astype(F32)
        exp_g[c, e] = jnp.exp(g_col[c, e])
        rhs = jnp.concatenate([kf[c] * (beta_col[c, e] * exp_g[c, e]), ve * beta_col[c, e]], axis=1)
        sol[c, e] = _dot(t_mat.astype(BF16), rhs.astype(BF16))
    recurrence(7)
    for c in range(n_chunks):
        qm, o_n, cd = [], [], []
        for e in range(2):
            a_qk = (qk2[c] * decay[c, e])[:, :CHUNK].astype(BF16)
            qo = _dot(a_qk, sol[c, e].astype(BF16))
            q_eff = qf[c] * exp_g[c, e] - qo[:, :HEAD_DIM]
            k_tail = jnp.exp(g_last[c, e] - g_col[c, e])[:, :1]
            mn = _dot(k_t[c], (sol[c, e] * k_tail).astype(BF16))
            qm.append(jnp.concatenate([q_eff, mn[:, :HEAD_DIM]], axis=0).astype(BF16))
            o_n.append(jnp.concatenate([qo[:, HEAD_DIM:], mn[:, HEAD_DIM:]], axis=0))
            cd.append(jnp.broadcast_to(jnp.exp(g_last[c, e]), (SUBLANES, LANES)))
        qm_ref[slot_a, c] = jnp.concatenate(qm, axis=1)
        on_ref[slot_a, c] = jnp.concatenate(o_n, axis=1)
        cd_ref[slot_a, c] = jnp.concatenate(cd, axis=1)

    state_ref[...] = state[0]


def _dn_chunk(qk, v, z, gate_c, gate_t, norm_w, batch, seq, n_kh, tc):
    m = qk.shape[0]
    nt = seq // tc
    cpt = tc // CHUNK
    assert cpt == 8, "the recurrence steps are interleaved with 8 preparation levels"

    n_tiles = batch * n_kh * nt

    def split(t):
        return (t // (n_kh * nt)) * nt + t % nt, (t // nt) % n_kh

    def prep(g):
        return split(jnp.minimum(g, n_tiles - 1))

    def rec(g):
        return split(jnp.maximum(g - 1, 0))

    return pl.pallas_call(
        functools.partial(_dn_chunk_kernel, nt=nt, n_kh=n_kh),
        grid=(n_tiles + 1,),
        in_specs=[
            pl.BlockSpec((tc, HEAD_DIM), lambda g: prep(g)),
            pl.BlockSpec((tc, HEAD_DIM), lambda g: (prep(g)[0], n_kh + prep(g)[1])),
            pl.BlockSpec((tc, 2 * HEAD_DIM), lambda g: prep(g)),
            pl.BlockSpec((tc, 2 * HEAD_DIM), lambda g: rec(g)),
            pl.BlockSpec((tc, gate_c.shape[1]), lambda g: (prep(g)[0], 0)),
            pl.BlockSpec((None, cpt, SUBLANES, LANES), lambda g: (prep(g)[1], prep(g)[0], 0, 0)),
            pl.BlockSpec((1, HEAD_DIM), lambda g: (0, 0)),
        ],
        out_specs=pl.BlockSpec((tc, 2 * HEAD_DIM), lambda g: rec(g)),
        out_shape=jax.ShapeDtypeStruct((m, 2 * n_kh * HEAD_DIM), BF16),
        scratch_shapes=[pltpu.VMEM((HEAD_DIM, 2 * HEAD_DIM), F32),
                        pltpu.VMEM((2, cpt, CHUNK + HEAD_DIM, 2 * HEAD_DIM), BF16),
                        pltpu.VMEM((2, cpt, CHUNK + HEAD_DIM, 2 * HEAD_DIM), F32),
                        pltpu.VMEM((2, cpt, SUBLANES, 2 * LANES), F32)],
        compiler_params=_cparams(1),
        name="dn_chunk",
    )(qk, qk, v, z, gate_c, gate_t, norm_w)


def _tile(n, want):
    if n <= want:
        return n
    t = want - want % LANES
    while n % t:
        t -= LANES
    return t


def _gate_slots(w_beta_a, a_log, dt_bias, n_vh):
    n_kh = n_vh // 2
    d = w_beta_a.shape[0]
    wb = w_beta_a[:, :n_vh].reshape(d, n_kh, 2)
    wa = w_beta_a[:, n_vh:].reshape(d, n_kh, 2)
    w_slot = jnp.concatenate([wb, wa, jnp.zeros((d, n_kh, SUBLANES - 4), w_beta_a.dtype)], axis=2)
    zeros2 = jnp.zeros((n_kh, 2), F32)
    zeros4 = jnp.zeros((n_kh, SUBLANES - 4), F32)
    al = jnp.concatenate([zeros2, a_log.astype(F32).reshape(n_kh, 2), zeros4], axis=1)
    dt = jnp.concatenate([zeros2, dt_bias.astype(F32).reshape(n_kh, 2), zeros4], axis=1)
    pad = (-n_kh * SUBLANES) % LANES
    w_slot = jnp.pad(w_slot.reshape(d, n_kh * SUBLANES), ((0, 0), (0, pad)))
    return w_slot, jnp.pad(al.reshape(-1), (0, pad)), jnp.pad(dt.reshape(-1), (0, pad))


def kernel(x, sc_w_in, sc_conv_w, sc_w_out, dn_w_in, dn_conv_w, dn_a_log, dn_dt_bias, dn_norm_w,
           dn_w_out, ffn_w_gate_up, ffn_w_down, ln_gain, ln_bias):
    batch, seq, d = x.shape
    depth = ln_gain.shape[0]
    alpha = float((2 * depth) ** 0.25)
    m = batch * seq
    n_vh = dn_a_log.shape[1]
    n_kh = n_vh // 2
    key_dim = n_kh * HEAD_DIM
    val_dim = n_vh * HEAD_DIM
    qkv_dim = 2 * key_dim + val_dim
    hidden = ffn_w_down.shape[1]
    assert seq % CHUNK == 0 and d % LANES == 0 and dn_norm_w.shape[1] == HEAD_DIM

    tm_in = _tile(seq, 1024)
    tm_dn = _tile(seq, 2048)
    tm_out = _tile(seq, 512)
    tm_ffn = _tile(seq, 1024)
    tn = _tile(d, 512)
    th = _tile(hidden, 512)
    tc = _tile(seq, 512)

    h = x.reshape(m, d)
    hb = h.astype(BF16)
    for i in range(depth):
        l = i // 2
        gb0 = jnp.stack([ln_gain[i, 0], ln_bias[i, 0]])
        gb1 = jnp.stack([ln_gain[i, 1], ln_bias[i, 1]])
        if i % 2 == 0:
            a = _sc_in(hb, sc_w_in[l].astype(BF16), sc_conv_w[l], seq, tm_in, tn)
            h = _proj_ln(a, sc_w_out[l].astype(BF16), h, gb0, alpha, tm_out)
        else:
            w_in = dn_w_in[l]
            w_qkvz = w_in[:, :qkv_dim + val_dim].astype(BF16)
            cw = dn_conv_w[l]
            qk = _dn_in(hb, w_qkvz, cw, 0, 2 * key_dim, seq, tm_dn, tn, conv=True, l2norm=True,
                        n_scaled=key_dim // tn, name="dn_in_qk")
            v = _dn_in(hb, w_qkvz, cw, 2 * key_dim, val_dim, seq, tm_dn, tn, conv=True, name="dn_in_v")
            z = _dn_in(hb, w_qkvz, cw, qkv_dim, val_dim, seq, tm_dn, tn, conv=False, name="dn_in_z")
            w_slot, al_slot, dt_slot = _gate_slots(w_in[:, qkv_dim + val_dim:], dn_a_log[l],
                                                   dn_dt_bias[l], n_vh)
            gate_c, gate_t = _dn_gates(hb, w_slot.astype(BF16), al_slot, dt_slot, tm_out)
            gate_t = gate_t[:n_kh * SUBLANES].reshape(n_kh, SUBLANES, m // CHUNK, CHUNK)
            gate_t = jnp.transpose(gate_t, (0, 2, 1, 3))
            gate_t = jnp.concatenate([gate_t, gate_t], axis=-1)
            o = _dn_chunk(qk, v, z, gate_c, gate_t, dn_norm_w[l][None, :], batch, seq, n_kh, tc)
            h = _proj_ln(o, dn_w_out[l].astype(BF16), h, gb0, alpha, tm_out)
        h, hb = _ffn(h, ffn_w_gate_up[i].astype(BF16), ffn_w_down[i].astype(BF16), gb1, alpha, tm_ffn, th,
                     emit_bf16=i + 1 < depth)
    return h.reshape(batch, seq, d)
```

```python
import functools

import jax
import jax.numpy as jnp
from jax import lax
from jax.experimental import pallas as pl
from jax.experimental.pallas import tpu as pltpu

HEAD_DIM = 128
CHUNK = 64
SUBLANES = 8
LANES = 128
LN_EPS = 1e-5
RMS_EPS = 1e-6
NEG_BIG = -1e30
VMEM_LIMIT_BYTES = 60 * 1024 * 1024

F32 = jnp.float32
BF16 = jnp.bfloat16


def _cparams(n_axes):
    return pltpu.CompilerParams(
        dimension_semantics=("arbitrary",) * n_axes,
        vmem_limit_bytes=VMEM_LIMIT_BYTES)


def _dot(a, b):
    return jnp.dot(a, b, preferred_element_type=F32)


def _dot_nt(a, b):
    return lax.dot_general(a, b, (((1,), (1,)), ((), ())), preferred_element_type=F32)


def _dot_tn(a, b):
    return lax.dot_general(a, b, (((0,), (0,)), ((), ())), preferred_element_type=F32)


def _layer_norm(y, gain, bias):
    mu = jnp.mean(y, axis=-1, keepdims=True)
    d = y - mu
    var = jnp.mean(d * d, axis=-1, keepdims=True)
    return d * lax.rsqrt(var + LN_EPS) * gain + bias


def _silu(t):
    return t * jax.nn.sigmoid(t)


def _causal_conv(u, prev, w_ref):
    ksize = w_ref.shape[0]
    out = u * w_ref[ksize - 1:ksize, :]
    head = jnp.concatenate([prev, u[0:SUBLANES]], axis=0)
    for d in range(1, ksize):
        shifted = pltpu.roll(u, d, axis=0)
        top = pltpu.roll(head, d, axis=0)[SUBLANES:2 * SUBLANES]
        shifted = jnp.concatenate([top, shifted[SUBLANES:]], axis=0)
        out = out + shifted * w_ref[ksize - 1 - d:ksize - d, :]
    return out


def _causal_taps(ext, w_rows):
    k = len(w_rows)
    assert k <= SUBLANES
    s1 = pltpu.roll(ext, 1, axis=0)
    out = None
    for d0 in range(0, k, 2):
        term = ext * w_rows[k - 1 - d0]
        if d0 + 1 < k:
            term = term + s1 * w_rows[k - 2 - d0]
        if d0:
            term = pltpu.roll(term, d0, axis=0)
        out = term if out is None else out + term
    return out[SUBLANES:]


def _conv_history(carry_ref, j, i, tiles_per_seq):
    prev = carry_ref[j]
    return jnp.where(i % tiles_per_seq == 0, jnp.zeros_like(prev), prev)


def _sc_in_kernel(xb_ref, wb_ref, wc_ref, wh_ref, cw_ref, o_ref, carry_ref, *, tiles_per_seq):
    i, j = pl.program_id(0), pl.program_id(1)
    xb = xb_ref[...]
    u = _dot(xb, wc_ref[...]) * _dot(xb, wh_ref[...])
    prev = _conv_history(carry_ref, j, i, tiles_per_seq)
    y = _causal_conv(u, prev, cw_ref)
    carry_ref[j] = u[u.shape[0] - SUBLANES:]
    o_ref[...] = (_dot(xb, wb_ref[...]) * y).astype(o_ref.dtype)


def _sc_in(xb, w_in, conv_w, seq, tm, tn):
    m, d = xb.shape
    nd = d // tn
    return pl.pallas_call(
        functools.partial(_sc_in_kernel, tiles_per_seq=seq // tm),
        grid=(m // tm, nd),
        in_specs=[
            pl.BlockSpec((tm, d), lambda i, j: (i, 0)),
            pl.BlockSpec((d, tn), lambda i, j: (0, j)),
            pl.BlockSpec((d, tn), lambda i, j: (0, nd + j)),
            pl.BlockSpec((d, tn), lambda i, j: (0, 2 * nd + j)),
            pl.BlockSpec((conv_w.shape[0], tn), lambda i, j: (0, j)),
        ],
        out_specs=pl.BlockSpec((tm, tn), lambda i, j: (i, j)),
        out_shape=jax.ShapeDtypeStruct((m, d), BF16),
        scratch_shapes=[pltpu.VMEM((nd, SUBLANES, tn), F32)],
        compiler_params=_cparams(2),
        name="sc_in",
    )(xb, w_in, w_in, w_in, conv_w)


def _residual_ln_blocks(n_sub, rs, pre_norm, gb_ref, out_refs):
    gain, bias = gb_ref[0:1, :], gb_ref[1:2, :]

    def finish(b, y):
        out = _layer_norm(y, gain, bias)
        for r in out_refs:
            r[b * rs:(b + 1) * rs, :] = out.astype(r.dtype)

    pending = pre_norm(0)
    for b in range(1, n_sub):
        nxt = pre_norm(b)
        finish(b - 1, pending)
        pending = nxt
    finish(n_sub - 1, pending)


def _proj_ln_kernel(a_ref, w_ref, x_ref, gb_ref, o_ref, *, alpha, n_sub):
    rs = o_ref.shape[0] // n_sub

    def pre_norm(b):
        rows = slice(b * rs, (b + 1) * rs)
        return alpha * x_ref[rows, :] + _dot(a_ref[rows, :], w_ref[...])

    _residual_ln_blocks(n_sub, rs, pre_norm, gb_ref, [o_ref])


def _proj_ln(a, w, x, gain_bias, alpha, tm):
    m, d = x.shape
    k = a.shape[1]
    return pl.pallas_call(
        functools.partial(_proj_ln_kernel, alpha=alpha, n_sub=max(1, tm // 128)),
        grid=(m // tm,),
        in_specs=[
            pl.BlockSpec((tm, k), lambda i: (i, 0)),
            pl.BlockSpec((k, d), lambda i: (0, 0), pipeline_mode=pl.Buffered(1)),
            pl.BlockSpec((tm, d), lambda i: (i, 0)),
            pl.BlockSpec((2, d), lambda i: (0, 0)),
        ],
        out_specs=pl.BlockSpec((tm, d), lambda i: (i, 0)),
        out_shape=jax.ShapeDtypeStruct((m, d), F32),
        compiler_params=_cparams(1),
        name="proj_ln",
    )(a, w, x, gain_bias)


def _ffn_kernel(x_ref, wg_ref, wu_ref, wd_ref, gb_ref, o_ref, *rest, alpha):
    *ob_ref, xb_ref, acc_ref = rest
    j = pl.program_id(1)

    @pl.when(j == 0)
    def _():
        xb_ref[...] = x_ref[...].astype(BF16)
        acc_ref[...] = jnp.zeros_like(acc_ref)

    xb = xb_ref[...]
    act = _silu(_dot(xb, wg_ref[...])) * _dot(xb, wu_ref[...])
    acc_ref[...] += _dot(act.astype(BF16), wd_ref[...])

    @pl.when(j == pl.num_programs(1) - 1)
    def _():
        y = alpha * x_ref[...] + acc_ref[...]
        out = _layer_norm(y, gb_ref[0:1, :], gb_ref[1:2, :])
        o_ref[...] = out
        for r in ob_ref:
            r[...] = out.astype(BF16)


def _ffn(x, w_gate_up, w_down, gain_bias, alpha, tm, th, emit_bf16):
    m, d = x.shape
    hidden = w_down.shape[0]
    nh = hidden // th
    row_block = pl.BlockSpec((tm, d), lambda i, j: (i, 0))
    outs = pl.pallas_call(
        functools.partial(_ffn_kernel, alpha=alpha),
        grid=(m // tm, nh),
        in_specs=[
            row_block,
            pl.BlockSpec((d, th), lambda i, j: (0, j)),
            pl.BlockSpec((d, th), lambda i, j: (0, nh + j)),
            pl.BlockSpec((th, d), lambda i, j: (j, 0)),
            pl.BlockSpec((2, d), lambda i, j: (0, 0)),
        ],
        out_specs=[row_block] * (2 if emit_bf16 else 1),
        out_shape=[jax.ShapeDtypeStruct((m, d), F32)] + [jax.ShapeDtypeStruct((m, d), BF16)] * emit_bf16,
        scratch_shapes=[pltpu.VMEM((tm, d), BF16), pltpu.VMEM((tm, d), F32)],
        compiler_params=_cparams(2),
        name="ffn",
    )(x, w_gate_up, w_gate_up, w_down, gain_bias)
    return outs if emit_bf16 else (outs[0], None)


def _dn_in_kernel(xb_ref, w_ref, cw_ref, o_ref, *scratch, tiles_per_seq, conv, l2norm, n_scaled, n_sub):
    i, j = pl.program_id(0), pl.program_id(1)
    tm = o_ref.shape[0]
    rs = tm // n_sub

    def block_dot(b):
        return _dot(xb_ref[b * rs:(b + 1) * rs, :], w_ref[...])

    if not conv:
        for b in range(n_sub):
            o_ref[b * rs:(b + 1) * rs, :] = block_dot(b).astype(o_ref.dtype)
        return

    acc0_ref, acc1_ref, carry_ref = scratch
    acc = (acc0_ref, acc1_ref)
    ksize = cw_ref.shape[0]

    @pl.when(i % tiles_per_seq == 0)
    def _():
        acc0_ref[0:SUBLANES, :] = jnp.zeros((SUBLANES, acc0_ref.shape[1]), F32)

    @pl.when(i % tiles_per_seq != 0)
    def _():
        acc0_ref[0:SUBLANES, :] = carry_ref[j]

    scale = jnp.where(j < n_scaled, HEAD_DIM ** -0.5, 1.0).astype(F32)

    def epilogue(b):
        src = acc[b % 2]
        for s in range(o_ref.shape[1] // HEAD_DIM):
            lanes = slice(s * HEAD_DIM, (s + 1) * HEAD_DIM)
            t = _silu(_causal_taps(src[:, lanes], [cw_ref[k:k + 1, lanes] for k in range(ksize)]))
            if l2norm:
                t = t * (lax.rsqrt(jnp.sum(t * t, axis=-1, keepdims=True) + RMS_EPS) * scale)
            o_ref[b * rs:(b + 1) * rs, lanes] = t.astype(o_ref.dtype)

    tail = None
    for b in range(n_sub):
        res = block_dot(b)
        acc[b % 2][SUBLANES:SUBLANES + rs, :] = res
        if b > 0:
            epilogue(b - 1)
            acc[b % 2][0:SUBLANES, :] = tail
        tail = res[rs - SUBLANES:, :]
    epilogue(n_sub - 1)
    carry_ref[j] = tail


def _dn_in(xb, w_in, conv_w, col0, width, seq, tm, tn, *, conv, l2norm=False, n_scaled=0, name):
    m, d = xb.shape
    nj = width // tn
    j0 = col0 // tn
    n_sub = max(1, tm // 128)
    blk = pltpu.VMEM((tm // n_sub + SUBLANES, tn), F32)
    scratch = [blk, blk, pltpu.VMEM((nj, SUBLANES, tn), F32)] if conv else []
    cw_j0 = j0 if conv else 0
    return pl.pallas_call(
        functools.partial(_dn_in_kernel, tiles_per_seq=seq // tm, conv=conv, l2norm=l2norm,
                          n_scaled=n_scaled, n_sub=n_sub),
        grid=(m // tm, nj),
        in_specs=[
            pl.BlockSpec((tm, d), lambda i, j: (i, 0)),
            pl.BlockSpec((d, tn), lambda i, j: (0, j0 + j)),
            pl.BlockSpec((conv_w.shape[0], tn), lambda i, j: (0, cw_j0 + j if conv else 0)),
        ],
        out_specs=pl.BlockSpec((tm, tn), lambda i, j: (i, j)),
        out_shape=jax.ShapeDtypeStruct((m, width), BF16),
        scratch_shapes=scratch,
        compiler_params=_cparams(2),
        name=name,
    )(xb, w_in, conv_w)


def _softplus(t):
    return jnp.maximum(t, 0.0) + jnp.log1p(jnp.exp(-jnp.abs(t)))


def _chunk_cumsum(g, axis):
    pos = lax.broadcasted_iota(jnp.int32, g.shape, axis) % CHUNK
    s = 1
    while s < CHUNK:
        g = g + jnp.where(pos >= s, pltpu.roll(g, s, axis=axis), 0.0)
        s *= 2
    return g


def _gates(raw, a_log, dt_bias, slot_axis, token_axis):
    c = lax.broadcasted_iota(jnp.int32, raw.shape, slot_axis) % SUBLANES
    beta = jax.nn.sigmoid(raw)
    g = -jnp.exp(a_log) * _softplus(raw + dt_bias)
    g = _chunk_cumsum(jnp.where(jnp.logical_and(c >= 2, c < 4), g, 0.0), token_axis)
    return jnp.where(c < 2, beta, g)


def _dn_gate_kernel(x_ref, w_ref, wt_ref, al_ref, dt_ref, alt_ref, dtt_ref, gc_ref, gt_ref):
    xb = x_ref[...]
    gc_ref[...] = _gates(_dot(xb, w_ref[...]), al_ref[...], dt_ref[...], 1, 0)
    gt_ref[...] = _gates(_dot_nt(wt_ref[...], xb), alt_ref[...], dtt_ref[...], 0, 1)


def _dn_gates(x, w_slot, a_log_slot, dt_slot, tm):
    m, d = x.shape
    ns = w_slot.shape[1]
    return pl.pallas_call(
        _dn_gate_kernel,
        grid=(m // tm,),
        in_specs=[
            pl.BlockSpec((tm, d), lambda i: (i, 0)),
            pl.BlockSpec((d, ns), lambda i: (0, 0)),
            pl.BlockSpec((ns, d), lambda i: (0, 0)),
            pl.BlockSpec((1, ns), lambda i: (0, 0)),
            pl.BlockSpec((1, ns), lambda i: (0, 0)),
            pl.BlockSpec((ns, 1), lambda i: (0, 0)),
            pl.BlockSpec((ns, 1), lambda i: (0, 0)),
        ],
        out_specs=[pl.BlockSpec((tm, ns), lambda i: (i, 0)),
                   pl.BlockSpec((ns, tm), lambda i: (0, i))],
        out_shape=[jax.ShapeDtypeStruct((m, ns), F32), jax.ShapeDtypeStruct((ns, m), F32)],
        compiler_params=_cparams(1),
        name="dn_gates",
    )(x, w_slot, w_slot.T, a_log_slot[None, :], dt_slot[None, :], a_log_slot[:, None], dt_slot[:, None])


def _block_diag(a, b):
    zero = jnp.zeros_like(a)
    return jnp.concatenate([jnp.concatenate([a, zero], axis=1), jnp.concatenate([zero, b], axis=1)], axis=0)


def _dn_chunk_kernel(q_ref, k_ref, v_ref, z_ref, gc_ref, gt_ref, nw_ref, o_ref,
                     state_ref, qm_ref, on_ref, cd_ref, *, nt, n_kh):
    g = pl.program_id(0)
    kh = (jnp.minimum(g, pl.num_programs(0) - 2) // nt) % n_kh
    n_chunks = q_ref.shape[0] // CHUNK
    slot_a = g % 2
    slot_b = 1 - slot_a

    @pl.when(g == 0)
    def _():
        qm_ref[1] = jnp.zeros(qm_ref.shape[1:], qm_ref.dtype)
        on_ref[1] = jnp.zeros(on_ref.shape[1:], on_ref.dtype)
        cd_ref[1] = jnp.zeros(cd_ref.shape[1:], cd_ref.dtype)

    @pl.when(jnp.maximum(g - 1, 0) % nt == 0)
    def _():
        state_ref[...] = jnp.zeros_like(state_ref)

    row = lax.broadcasted_iota(jnp.int32, (CHUNK, LANES), 0)
    lane = lax.broadcasted_iota(jnp.int32, (CHUNK, LANES), 1)
    col = lane % CHUNK
    causal = col <= row
    strict = col < row
    upper_half = lane >= CHUNK
    lower_half = lane < CHUNK
    eye = col == row
    eye_hi = jnp.where(jnp.logical_and(upper_half, eye), 1.0, 0.0).astype(F32)
    eye_lo = jnp.where(jnp.logical_and(lower_half, eye), 1.0, 0.0).astype(F32)
    norm_w = nw_ref[...]
    state = [state_ref[...]]

    def recurrence(c):
        rows = slice(c * CHUNK, (c + 1) * CHUNK)
        s_b = state[0].astype(BF16)
        qms = _dot(qm_ref[slot_b, c], _block_diag(s_b[:, :HEAD_DIM], s_b[:, HEAD_DIM:]))
        o_n = on_ref[slot_b, c]
        o_all = qms[:CHUNK] + o_n[:CHUNK]
        state[0] = state[0] * cd_ref[slot_b, c][:1, :] + o_n[CHUNK:] - qms[CHUNK:]
        for e in range(2):
            o = o_all[:, e * HEAD_DIM:(e + 1) * HEAD_DIM]
            o = o * lax.rsqrt(jnp.mean(o * o, axis=-1, keepdims=True) + RMS_EPS) * norm_w
            ze = z_ref[rows, e * HEAD_DIM:(e + 1) * HEAD_DIM].astype(F32)
            o_ref[rows, e * HEAD_DIM:(e + 1) * HEAD_DIM] = (o * _silu(ze)).astype(o_ref.dtype)

    recurrence(0)
    gsel_all = pltpu.roll(gc_ref[...], (LANES - SUBLANES * kh) % LANES, axis=1)
    kf, qf, k_t, qk2, kk2 = [], [], [], [], []
    for c in range(n_chunks):
        rows = slice(c * CHUNK, (c + 1) * CHUNK)
        qc, kc = q_ref[rows, :], k_ref[rows, :]
        qf.append(qc.astype(F32))
        kf.append(kc.astype(F32))
        k_t.append(jnp.transpose(kf[c]).astype(BF16))
        qkk = _dot_nt(jnp.concatenate([qc, kc], axis=0), jnp.concatenate([kc, kc], axis=0))
        qk2.append(qkk[:CHUNK])
        kk2.append(qkk[CHUNK:])
    recurrence(1)

    chains = [(c, e) for c in range(n_chunks) for e in range(2)]
    beta_col, g_col, g_last, decay, x_mat, w_mat = {}, {}, {}, {}, {}, {}
    for (c, e) in chains:
        gsel = gsel_all[c * CHUNK:(c + 1) * CHUNK]
        g_row = gt_ref[c][2 + e:3 + e, :]
        beta_col[c, e] = jnp.broadcast_to(gsel[:, e:e + 1], (CHUNK, LANES))
        g_col[c, e] = jnp.broadcast_to(gsel[:, 2 + e:3 + e], (CHUNK, LANES))
        g_last[c, e] = g_row[:, CHUNK - 1:CHUNK]
        decay[c, e] = jnp.exp(jnp.where(causal, g_col[c, e] - g_row, NEG_BIG))
        x_mat[c, e] = jnp.where(strict, -(kk2[c] * beta_col[c, e] * decay[c, e]), 0.0)
    for c in range(n_chunks):
        xa, xb = x_mat[c, 0], x_mat[c, 1]
        sq = _dot(jnp.where(lower_half, xa, xb).astype(BF16), _block_diag(xa.astype(BF16), xb.astype(BF16)))
        w_mat[c, 0] = jnp.where(upper_half, xa + eye_hi, sq[:, :LANES])
        w_mat[c, 1] = jnp.where(upper_half, sq[:, LANES:], xb + eye_lo)
    recurrence(2)
    for level in range(5):
        for c in range(n_chunks):
            wa, wb = w_mat[c, 0], w_mat[c, 1]
            pw = _dot(jnp.where(lower_half, wa, wb).astype(BF16), _block_diag(wa.astype(BF16), wb.astype(BF16)))
            w_mat[c, 0] = pw[:, :LANES] + jnp.where(upper_half, wa, 0.0)
            w_mat[c, 1] = pw[:, LANES:] + jnp.where(lower_half, wb, 0.0)
        if level < 4:
            recurrence(3 + level)
    sol, exp_g = {}, {}
    for (c, e) in chains:
        t_mat = w_mat[c, e][:, CHUNK:] if e == 0 else w_mat[c, e][:, :CHUNK]
        ve = v_ref[c * CHUNK:(c + 1) * CHUNK, e * HEAD_DIM:(e + 1) * HEAD_DIM].astype(F32)
        exp_g[c, e] = jnp.exp(g_col[c, e])
        rhs = jnp.concatenate([kf[c] * (beta_col[c, e] * exp_g[c, e]), ve * beta_col[c, e]], axis=1)
        sol[c, e] = _dot(t_mat.astype(BF16), rhs.astype(BF16))
    recurrence(7)
    for c in range(n_chunks):
        qm, o_n, cd = [], [], []
        for e in range(2):
            a_qk = (qk2[c] * decay[c, e])[:, :CHUNK].astype(BF16)
            qo = _dot(a_qk, sol[c, e].astype(BF16))
            q_eff = qf[c] * exp_g[c, e] - qo[:, :HEAD_DIM]
            k_tail = jnp.exp(g_last[c, e] - g_col[c, e])[:, :1]
            mn = _dot(k_t[c], (sol[c, e] * k_tail).astype(BF16))
            qm.append(jnp.concatenate([q_eff, mn[:, :HEAD_DIM]], axis=0).astype(BF16))
            o_n.append(jnp.concatenate([qo[:, HEAD_DIM:], mn[:, HEAD_DIM:]], axis=0))
            cd.append(jnp.broadcast_to(jnp.exp(g_last[c, e]), (SUBLANES, LANES)))
        qm_ref[slot_a, c] = jnp.concatenate(qm, axis=1)
        on_ref[slot_a, c] = jnp.concatenate(o_n, axis=1)
        cd_ref[slot_a, c] = jnp.concatenate(cd, axis=1)

    state_ref[...] = state[0]


def _dn_chunk(qk, v, z, gate_c, gate_t, norm_w, batch, seq, n_kh, tc):
    m = qk.shape[0]
    nt = seq // tc
    cpt = tc // CHUNK
    assert cpt == 8, "the recurrence steps are interleaved with 8 preparation levels"

    n_tiles = batch * n_kh * nt

    def split(t):
        return (t // (n_kh * nt)) * nt + t % nt, (t // nt) % n_kh

    def prep(g):
        return split(jnp.minimum(g, n_tiles - 1))

    def rec(g):
        return split(jnp.maximum(g - 1, 0))

    return pl.pallas_call(
        functools.partial(_dn_chunk_kernel, nt=nt, n_kh=n_kh),
        grid=(n_tiles + 1,),
        in_specs=[
            pl.BlockSpec((tc, HEAD_DIM), lambda g: prep(g)),
            pl.BlockSpec((tc, HEAD_DIM), lambda g: (prep(g)[0], n_kh + prep(g)[1])),
            pl.BlockSpec((tc, 2 * HEAD_DIM), lambda g: prep(g)),
            pl.BlockSpec((tc, 2 * HEAD_DIM), lambda g: rec(g)),
            pl.BlockSpec((tc, gate_c.shape[1]), lambda g: (prep(g)[0], 0)),
            pl.BlockSpec((None, cpt, SUBLANES, LANES), lambda g: (prep(g)[1], prep(g)[0], 0, 0)),
            pl.BlockSpec((1, HEAD_DIM), lambda g: (0, 0)),
        ],
        out_specs=pl.BlockSpec((tc, 2 * HEAD_DIM), lambda g: rec(g)),
        out_shape=jax.ShapeDtypeStruct((m, 2 * n_kh * HEAD_DIM), BF16),
        scratch_shapes=[pltpu.VMEM((HEAD_DIM, 2 * HEAD_DIM), F32),
                        pltpu.VMEM((2, cpt, CHUNK + HEAD_DIM, 2 * HEAD_DIM), BF16),
                        pltpu.VMEM((2, cpt, CHUNK + HEAD_DIM, 2 * HEAD_DIM), F32),
                        pltpu.VMEM((2, cpt, SUBLANES, 2 * LANES), F32)],
        compiler_params=_cparams(1),
        name="dn_chunk",
    )(qk, qk, v, z, gate_c, gate_t, norm_w)


def _tile(n, want):
    if n <= want:
        return n
    t = want - want % LANES
    while n % t:
        t -= LANES
    return t


def _gate_slots(w_beta_a, a_log, dt_bias, n_vh):
    n_kh = n_vh // 2
    d = w_beta_a.shape[0]
    wb = w_beta_a[:, :n_vh].reshape(d, n_kh, 2)
    wa = w_beta_a[:, n_vh:].reshape(d, n_kh, 2)
    w_slot = jnp.concatenate([wb, wa, jnp.zeros((d, n_kh, SUBLANES - 4), w_beta_a.dtype)], axis=2)
    zeros2 = jnp.zeros((n_kh, 2), F32)
    zeros4 = jnp.zeros((n_kh, SUBLANES - 4), F32)
    al = jnp.concatenate([zeros2, a_log.astype(F32).reshape(n_kh, 2), zeros4], axis=1)
    dt = jnp.concatenate([zeros2, dt_bias.astype(F32).reshape(n_kh, 2), zeros4], axis=1)
    pad = (-n_kh * SUBLANES) % LANES
    w_slot = jnp.pad(w_slot.reshape(d, n_kh * SUBLANES), ((0, 0), (0, pad)))
    return w_slot, jnp.pad(al.reshape(-1), (0, pad)), jnp.pad(dt.reshape(-1), (0, pad))


def kernel(x, sc_w_in, sc_conv_w, sc_w_out, dn_w_in, dn_conv_w, dn_a_log, dn_dt_bias, dn_norm_w,
           dn_w_out, ffn_w_gate_up, ffn_w_down, ln_gain, ln_bias):
    batch, seq, d = x.shape
    depth = ln_gain.shape[0]
    alpha = float((2 * depth) ** 0.25)
    m = batch * seq
    n_vh = dn_a_log.shape[1]
    n_kh = n_vh // 2
    key_dim = n_kh * HEAD_DIM
    val_dim = n_vh * HEAD_DIM
    qkv_dim = 2 * key_dim + val_dim
    hidden = ffn_w_down.shape[1]
    assert seq % CHUNK == 0 and d % LANES == 0 and dn_norm_w.shape[1] == HEAD_DIM

    tm_in = _tile(seq, 1024)
    tm_dn = _tile(seq, 2048)
    tm_out = _tile(seq, 512)
    tn = _tile(d, 512)
    th = _tile(hidden, 512)
    tc = _tile(seq, 512)

    h = x.reshape(m, d)
    hb = h.astype(BF16)
    for i in range(depth):
        l = i // 2
        gb0 = jnp.stack([ln_gain[i, 0], ln_bias[i, 0]])
        gb1 = jnp.stack([ln_gain[i, 1], ln_bias[i, 1]])
        if i % 2 == 0:
            a = _sc_in(hb, sc_w_in[l].astype(BF16), sc_conv_w[l], seq, tm_in, tn)
            h = _proj_ln(a, sc_w_out[l].astype(BF16), h, gb0, alpha, tm_out)
        else:
            w_in = dn_w_in[l]
            w_qkvz = w_in[:, :qkv_dim + val_dim].astype(BF16)
            cw = dn_conv_w[l]
            qk = _dn_in(hb, w_qkvz, cw, 0, 2 * key_dim, seq, tm_dn, tn, conv=True, l2norm=True,
                        n_scaled=key_dim // tn, name="dn_in_qk")
            v = _dn_in(hb, w_qkvz, cw, 2 * key_dim, val_dim, seq, tm_dn, tn, conv=True, name="dn_in_v")
            z = _dn_in(hb, w_qkvz, cw, qkv_dim, val_dim, seq, tm_dn, tn, conv=False, name="dn_in_z")
            w_slot, al_slot, dt_slot = _gate_slots(w_in[:, qkv_dim + val_dim:], dn_a_log[l],
                                                   dn_dt_bias[l], n_vh)
            gate_c, gate_t = _dn_gates(hb, w_slot.astype(BF16), al_slot, dt_slot, tm_out)
            gate_t = gate_t[:n_kh * SUBLANES].reshape(n_kh, SUBLANES, m // CHUNK, CHUNK)
            gate_t = jnp.transpose(gate_t, (0, 2, 1, 3))
            gate_t = jnp.concatenate([gate_t, gate_t], axis=-1)
            o = _dn_chunk(qk, v, z, gate_c, gate_t, dn_norm_w[l][None, :], batch, seq, n_kh, tc)
            h = _proj_ln(o, dn_w_out[l].astype(BF16), h, gb0, alpha, tm_out)
        h, hb = _ffn(h, ffn_w_gate_up[i].astype(BF16), ffn_w_down[i].astype(BF16), gb1, alpha, tm_out, th,
                     emit_bf16=i + 1 < depth)
    return h.reshape(batch, seq, d)
```

```python
import functools

import jax
import jax.numpy as jnp
from jax import lax
from jax.experimental import pallas as pl
from jax.experimental.pallas import tpu as pltpu

HEAD_DIM = 128
CHUNK = 64
SUBLANES = 8
LANES = 128
LN_EPS = 1e-5
RMS_EPS = 1e-6
NEG_BIG = -1e30
RECURRENCE_SLOTS = 8
VMEM_LIMIT_BYTES = 60 * 1024 * 1024

F32 = jnp.float32
BF16 = jnp.bfloat16


def _cparams(n_axes):
    return pltpu.CompilerParams(
        dimension_semantics=("arbitrary",) * n_axes,
        vmem_limit_bytes=VMEM_LIMIT_BYTES)


def _dot(a, b):
    return jnp.dot(a, b, preferred_element_type=F32)


def _dot_nt(a, b):
    return lax.dot_general(a, b, (((1,), (1,)), ((), ())), preferred_element_type=F32)


def _dot_tn(a, b):
    return lax.dot_general(a, b, (((0,), (0,)), ((), ())), preferred_element_type=F32)


def _layer_norm(y, gain, bias):
    mu = jnp.mean(y, axis=-1, keepdims=True)
    d = y - mu
    var = jnp.mean(d * d, axis=-1, keepdims=True)
    return d * lax.rsqrt(var + LN_EPS) * gain + bias


def _silu(t):
    return t * jax.nn.sigmoid(t)


def _causal_conv(u, prev, w_ref):
    ksize = w_ref.shape[0]
    out = u * w_ref[ksize - 1:ksize, :]
    head = jnp.concatenate([prev, u[0:SUBLANES]], axis=0)
    for d in range(1, ksize):
        shifted = pltpu.roll(u, d, axis=0)
        top = pltpu.roll(head, d, axis=0)[SUBLANES:2 * SUBLANES]
        shifted = jnp.concatenate([top, shifted[SUBLANES:]], axis=0)
        out = out + shifted * w_ref[ksize - 1 - d:ksize - d, :]
    return out


def _causal_taps(ext, w_rows):
    k = len(w_rows)
    assert k <= SUBLANES
    s1 = pltpu.roll(ext, 1, axis=0)
    out = None
    for d0 in range(0, k, 2):
        term = ext * w_rows[k - 1 - d0]
        if d0 + 1 < k:
            term = term + s1 * w_rows[k - 2 - d0]
        if d0:
            term = pltpu.roll(term, d0, axis=0)
        out = term if out is None else out + term
    return out[SUBLANES:]


def _conv_history(carry_ref, j, i, tiles_per_seq):
    prev = carry_ref[j]
    return jnp.where(i % tiles_per_seq == 0, jnp.zeros_like(prev), prev)


def _sc_in_kernel(xb_ref, wb_ref, wc_ref, wh_ref, cw_ref, o_ref, carry_ref, *, tiles_per_seq):
    i, j = pl.program_id(0), pl.program_id(1)
    xb = xb_ref[...]
    u = _dot(xb, wc_ref[...]) * _dot(xb, wh_ref[...])
    prev = _conv_history(carry_ref, j, i, tiles_per_seq)
    y = _causal_conv(u, prev, cw_ref)
    carry_ref[j] = u[u.shape[0] - SUBLANES:]
    o_ref[...] = (_dot(xb, wb_ref[...]) * y).astype(o_ref.dtype)


def _sc_in(xb, w_in, layer, conv_w, seq, tm, tn):
    m, d = xb.shape
    nd = d // tn
    return pl.pallas_call(
        functools.partial(_sc_in_kernel, tiles_per_seq=seq // tm),
        grid=(m // tm, nd),
        in_specs=[
            pl.BlockSpec((tm, d), lambda i, j: (i, 0)),
            pl.BlockSpec((None, d, tn), lambda i, j: (layer, 0, j)),
            pl.BlockSpec((None, d, tn), lambda i, j: (layer, 0, nd + j)),
            pl.BlockSpec((None, d, tn), lambda i, j: (layer, 0, 2 * nd + j)),
            pl.BlockSpec((conv_w.shape[0], tn), lambda i, j: (0, j)),
        ],
        out_specs=pl.BlockSpec((tm, tn), lambda i, j: (i, j)),
        out_shape=jax.ShapeDtypeStruct((m, d), BF16),
        scratch_shapes=[pltpu.VMEM((nd, SUBLANES, tn), F32)],
        compiler_params=_cparams(2),
        name="sc_in",
    )(xb, w_in, w_in, w_in, conv_w)


def _residual_ln_blocks(n_sub, rs, pre_norm, gb_ref, out_refs):
    gain, bias = gb_ref[0:1, :], gb_ref[1:2, :]

    def finish(b, y):
        out = _layer_norm(y, gain, bias)
        for r in out_refs:
            r[b * rs:(b + 1) * rs, :] = out.astype(r.dtype)

    pending = pre_norm(0)
    for b in range(1, n_sub):
        nxt = pre_norm(b)
        finish(b - 1, pending)
        pending = nxt
    finish(n_sub - 1, pending)


def _proj_ln_kernel(a_ref, w_ref, x_ref, gb_ref, o_ref, *, alpha, n_sub):
    rs = o_ref.shape[0] // n_sub

    def pre_norm(b):
        rows = slice(b * rs, (b + 1) * rs)
        return alpha * x_ref[rows, :] + _dot(a_ref[rows, :], w_ref[...])

    _residual_ln_blocks(n_sub, rs, pre_norm, gb_ref, [o_ref])


def _proj_ln(a, w, layer, x, gain_bias, alpha, tm):
    m, d = x.shape
    k = a.shape[1]
    return pl.pallas_call(
        functools.partial(_proj_ln_kernel, alpha=alpha, n_sub=max(1, tm // 128)),
        grid=(m // tm,),
        in_specs=[
            pl.BlockSpec((tm, k), lambda i: (i, 0)),
            pl.BlockSpec((None, k, d), lambda i: (layer, 0, 0), pipeline_mode=pl.Buffered(1)),
            pl.BlockSpec((tm, d), lambda i: (i, 0)),
            pl.BlockSpec((2, d), lambda i: (0, 0)),
        ],
        out_specs=pl.BlockSpec((tm, d), lambda i: (i, 0)),
        out_shape=jax.ShapeDtypeStruct((m, d), F32),
        compiler_params=_cparams(1),
        name="proj_ln",
    )(a, w, x, gain_bias)


def _ffn_kernel(x_ref, wg_ref, wu_ref, wd_ref, gb_ref, o_ref, *rest, alpha):
    *ob_ref, xb_ref, acc_ref = rest
    j = pl.program_id(1)

    @pl.when(j == 0)
    def _():
        xb_ref[...] = x_ref[...].astype(BF16)
        acc_ref[...] = jnp.zeros_like(acc_ref)

    xb = xb_ref[...]
    act = _silu(_dot(xb, wg_ref[...])) * _dot(xb, wu_ref[...])
    acc_ref[...] += _dot(act.astype(BF16), wd_ref[...])

    @pl.when(j == pl.num_programs(1) - 1)
    def _():
        y = alpha * x_ref[...] + acc_ref[...]
        out = _layer_norm(y, gb_ref[0:1, :], gb_ref[1:2, :])
        o_ref[...] = out
        for r in ob_ref:
            r[...] = out.astype(BF16)


def _ffn(x, w_gate_up, w_down, layer, gain_bias, alpha, tm, th, emit_bf16):
    m, d = x.shape
    hidden = w_down.shape[1]
    nh = hidden // th
    row_block = pl.BlockSpec((tm, d), lambda i, j: (i, 0))
    outs = pl.pallas_call(
        functools.partial(_ffn_kernel, alpha=alpha),
        grid=(m // tm, nh),
        in_specs=[
            row_block,
            pl.BlockSpec((None, d, th), lambda i, j: (layer, 0, j)),
            pl.BlockSpec((None, d, th), lambda i, j: (layer, 0, nh + j)),
            pl.BlockSpec((None, th, d), lambda i, j: (layer, j, 0)),
            pl.BlockSpec((2, d), lambda i, j: (0, 0)),
        ],
        out_specs=[row_block] * (2 if emit_bf16 else 1),
        out_shape=[jax.ShapeDtypeStruct((m, d), F32)] + [jax.ShapeDtypeStruct((m, d), BF16)] * emit_bf16,
        scratch_shapes=[pltpu.VMEM((tm, d), BF16), pltpu.VMEM((tm, d), F32)],
        compiler_params=_cparams(2),
        name="ffn",
    )(x, w_gate_up, w_gate_up, w_down, gain_bias)
    return outs if emit_bf16 else (outs[0], None)


def _dn_in_kernel(xb_ref, w_ref, cw_ref, o_ref, *scratch, tiles_per_seq, conv, l2norm, n_scaled, n_sub):
    i, j = pl.program_id(0), pl.program_id(1)
    tm = o_ref.shape[0]
    rs = tm // n_sub

    def block_dot(b):
        return _dot(xb_ref[b * rs:(b + 1) * rs, :], w_ref[...])

    if not conv:
        for b in range(n_sub):
            o_ref[b * rs:(b + 1) * rs, :] = block_dot(b).astype(o_ref.dtype)
        return

    acc0_ref, acc1_ref, carry_ref = scratch
    acc = (acc0_ref, acc1_ref)
    ksize = cw_ref.shape[0]

    @pl.when(i % tiles_per_seq == 0)
    def _():
        acc0_ref[0:SUBLANES, :] = jnp.zeros((SUBLANES, acc0_ref.shape[1]), F32)

    @pl.when(i % tiles_per_seq != 0)
    def _():
        acc0_ref[0:SUBLANES, :] = carry_ref[j]

    scale = jnp.where(j < n_scaled, HEAD_DIM ** -0.5, 1.0).astype(F32)

    def epilogue(b):
        src = acc[b % 2]
        for s in range(o_ref.shape[1] // HEAD_DIM):
            lanes = slice(s * HEAD_DIM, (s + 1) * HEAD_DIM)
            t = _silu(_causal_taps(src[:, lanes], [cw_ref[k:k + 1, lanes] for k in range(ksize)]))
            if l2norm:
                t = t * (lax.rsqrt(jnp.sum(t * t, axis=-1, keepdims=True) + RMS_EPS) * scale)
            o_ref[b * rs:(b + 1) * rs, lanes] = t.astype(o_ref.dtype)

    tail = None
    for b in range(n_sub):
        res = block_dot(b)
        acc[b % 2][SUBLANES:SUBLANES + rs, :] = res
        if b > 0:
            epilogue(b - 1)
            acc[b % 2][0:SUBLANES, :] = tail
        tail = res[rs - SUBLANES:, :]
    epilogue(n_sub - 1)
    carry_ref[j] = tail


def _dn_in(xb, w_in, layer, conv_w, col0, width, seq, tm, tn, *, conv, l2norm=False, n_scaled=0, name):
    m, d = xb.shape
    nj = width // tn
    j0 = col0 // tn
    n_sub = max(1, tm // 128)
    blk = pltpu.VMEM((tm // n_sub + SUBLANES, tn), F32)
    scratch = [blk, blk, pltpu.VMEM((nj, SUBLANES, tn), F32)] if conv else []
    cw_j0 = j0 if conv else 0
    return pl.pallas_call(
        functools.partial(_dn_in_kernel, tiles_per_seq=seq // tm, conv=conv, l2norm=l2norm,
                          n_scaled=n_scaled, n_sub=n_sub),
        grid=(m // tm, nj),
        in_specs=[
            pl.BlockSpec((tm, d), lambda i, j: (i, 0)),
            pl.BlockSpec((None, d, tn), lambda i, j: (layer, 0, j0 + j)),
            pl.BlockSpec((conv_w.shape[0], tn), lambda i, j: (0, cw_j0 + j if conv else 0)),
        ],
        out_specs=pl.BlockSpec((tm, tn), lambda i, j: (i, j)),
        out_shape=jax.ShapeDtypeStruct((m, width), BF16),
        scratch_shapes=scratch,
        compiler_params=_cparams(2),
        name=name,
    )(xb, w_in, conv_w)


def _softplus(t):
    return jnp.maximum(t, 0.0) + jnp.log1p(jnp.exp(-jnp.abs(t)))


def _chunk_cumsum(g, axis):
    pos = lax.broadcasted_iota(jnp.int32, g.shape, axis) % CHUNK
    s = 1
    while s < CHUNK:
        g = g + jnp.where(pos >= s, pltpu.roll(g, s, axis=axis), 0.0)
        s *= 2
    return g


def _gates(raw, a_log, dt_bias, slot_axis, token_axis):
    c = lax.broadcasted_iota(jnp.int32, raw.shape, slot_axis) % SUBLANES
    beta = jax.nn.sigmoid(raw)
    g = -jnp.exp(a_log) * _softplus(raw + dt_bias)
    g = _chunk_cumsum(jnp.where(jnp.logical_and(c >= 2, c < 4), g, 0.0), token_axis)
    return jnp.where(c < 2, beta, g)


def _dn_gate_kernel(x_ref, w_ref, wt_ref, al_ref, dt_ref, alt_ref, dtt_ref, gc_ref, gt_ref):
    xb = x_ref[...]
    gc_ref[...] = _gates(_dot(xb, w_ref[...]), al_ref[...], dt_ref[...], 1, 0)
    gt_ref[...] = _gates(_dot_nt(wt_ref[...], xb), alt_ref[...], dtt_ref[...], 0, 1)


def _dn_gates(x, w_slot, a_log_slot, dt_slot, tm):
    m, d = x.shape
    ns = w_slot.shape[1]
    return pl.pallas_call(
        _dn_gate_kernel,
        grid=(m // tm,),
        in_specs=[
            pl.BlockSpec((tm, d), lambda i: (i, 0)),
            pl.BlockSpec((d, ns), lambda i: (0, 0)),
            pl.BlockSpec((ns, d), lambda i: (0, 0)),
            pl.BlockSpec((1, ns), lambda i: (0, 0)),
            pl.BlockSpec((1, ns), lambda i: (0, 0)),
            pl.BlockSpec((ns, 1), lambda i: (0, 0)),
            pl.BlockSpec((ns, 1), lambda i: (0, 0)),
        ],
        out_specs=[pl.BlockSpec((tm, ns), lambda i: (i, 0)),
                   pl.BlockSpec((ns, tm), lambda i: (0, i))],
        out_shape=[jax.ShapeDtypeStruct((m, ns), F32), jax.ShapeDtypeStruct((ns, m), F32)],
        compiler_params=_cparams(1),
        name="dn_gates",
    )(x, w_slot, w_slot.T, a_log_slot[None, :], dt_slot[None, :], a_log_slot[:, None], dt_slot[:, None])


def _block_diag(a, b):
    zero = jnp.zeros_like(a)
    return jnp.concatenate([jnp.concatenate([a, zero], axis=1), jnp.concatenate([zero, b], axis=1)], axis=0)


def _dn_chunk_kernel(q_ref, k_ref, v_ref, z_ref, gc_ref, gt_ref, nw_ref, o_ref,
                     state_ref, qm_ref, on_ref, cd_ref, *, nt, n_kh):
    g = pl.program_id(0)
    kh = (jnp.minimum(g, pl.num_programs(0) - 2) // nt) % n_kh
    n_chunks = q_ref.shape[0] // CHUNK
    slot_a = g % 2
    slot_b = 1 - slot_a

    @pl.when(g == 0)
    def _():
        qm_ref[1] = jnp.zeros(qm_ref.shape[1:], qm_ref.dtype)
        on_ref[1] = jnp.zeros(on_ref.shape[1:], on_ref.dtype)
        cd_ref[1] = jnp.zeros(cd_ref.shape[1:], cd_ref.dtype)

    @pl.when(jnp.maximum(g - 1, 0) % nt == 0)
    def _():
        state_ref[...] = jnp.zeros_like(state_ref)

    row = lax.broadcasted_iota(jnp.int32, (CHUNK, LANES), 0)
    lane = lax.broadcasted_iota(jnp.int32, (CHUNK, LANES), 1)
    col = lane % CHUNK
    causal = col <= row
    strict = col < row
    upper_half = lane >= CHUNK
    lower_half = lane < CHUNK
    eye = col == row
    eye_hi = jnp.where(jnp.logical_and(upper_half, eye), 1.0, 0.0).astype(F32)
    eye_lo = jnp.where(jnp.logical_and(lower_half, eye), 1.0, 0.0).astype(F32)
    norm_w = nw_ref[...]
    state = [state_ref[...]]

    def recurrence_chunk(c):
        rows = slice(c * CHUNK, (c + 1) * CHUNK)
        s_b = state[0].astype(BF16)
        qms = _dot(qm_ref[slot_b, c], _block_diag(s_b[:, :HEAD_DIM], s_b[:, HEAD_DIM:]))
        o_n = on_ref[slot_b, c]
        o_all = qms[:CHUNK] + o_n[:CHUNK]
        state[0] = state[0] * cd_ref[slot_b, c][:1, :] + o_n[CHUNK:] - qms[CHUNK:]
        for e in range(2):
            o = o_all[:, e * HEAD_DIM:(e + 1) * HEAD_DIM]
            o = o * lax.rsqrt(jnp.mean(o * o, axis=-1, keepdims=True) + RMS_EPS) * norm_w
            ze = z_ref[rows, e * HEAD_DIM:(e + 1) * HEAD_DIM].astype(F32)
            o_ref[rows, e * HEAD_DIM:(e + 1) * HEAD_DIM] = (o * _silu(ze)).astype(o_ref.dtype)

    def recurrence(slot):
        per_slot = n_chunks // RECURRENCE_SLOTS
        for c in range(slot * per_slot, (slot + 1) * per_slot):
            recurrence_chunk(c)

    recurrence(0)
    gsel_all = pltpu.roll(gc_ref[...], (LANES - SUBLANES * kh) % LANES, axis=1)
    kf, qf, k_t, qk2, kk2 = [], [], [], [], []
    for c in range(n_chunks):
        rows = slice(c * CHUNK, (c + 1) * CHUNK)
        qc, kc = q_ref[rows, :], k_ref[rows, :]
        qf.append(qc.astype(F32))
        kf.append(kc.astype(F32))
        k_t.append(jnp.transpose(kf[c]).astype(BF16))
        qkk = _dot_nt(jnp.concatenate([qc, kc], axis=0), jnp.concatenate([kc, kc], axis=0))
        qk2.append(qkk[:CHUNK])
        kk2.append(qkk[CHUNK:])
    recurrence(1)

    chains = [(c, e) for c in range(n_chunks) for e in range(2)]
    beta_col, g_col, g_last, decay, x_mat, w_mat = {}, {}, {}, {}, {}, {}
    for (c, e) in chains:
        gsel = gsel_all[c * CHUNK:(c + 1) * CHUNK]
        g_row = gt_ref[c][2 + e:3 + e, :]
        beta_col[c, e] = jnp.broadcast_to(gsel[:, e:e + 1], (CHUNK, LANES))
        g_col[c, e] = jnp.broadcast_to(gsel[:, 2 + e:3 + e], (CHUNK, LANES))
        g_last[c, e] = g_row[:, CHUNK - 1:CHUNK]
        decay[c, e] = jnp.exp(jnp.where(causal, g_col[c, e] - g_row, NEG_BIG))
        x_mat[c, e] = jnp.where(strict, -(kk2[c] * beta_col[c, e] * decay[c, e]), 0.0)
    for c in range(n_chunks):
        xa, xb = x_mat[c, 0], x_mat[c, 1]
        sq = _dot(jnp.where(lower_half, xa, xb).astype(BF16), _block_diag(xa.astype(BF16), xb.astype(BF16)))
        w_mat[c, 0] = jnp.where(upper_half, xa + eye_hi, sq[:, :LANES])
        w_mat[c, 1] = jnp.where(upper_half, sq[:, LANES:], xb + eye_lo)
    recurrence(2)
    for level in range(5):
        for c in range(n_chunks):
            wa, wb = w_mat[c, 0], w_mat[c, 1]
            pw = _dot(jnp.where(lower_half, wa, wb).astype(BF16), _block_diag(wa.astype(BF16), wb.astype(BF16)))
            w_mat[c, 0] = pw[:, :LANES] + jnp.where(upper_half, wa, 0.0)
            w_mat[c, 1] = pw[:, LANES:] + jnp.where(lower_half, wb, 0.0)
        if level < 4:
            recurrence(3 + level)
    sol, exp_g = {}, {}
    for (c, e) in chains:
        t_mat = w_mat[c, e][:, CHUNK:] if e == 0 else w_mat[c, e][:, :CHUNK]
        ve = v_ref[c * CHUNK:(c + 1) * CHUNK, e * HEAD_DIM:(e + 1) * HEAD_DIM].astype(F32)
        exp_g[c, e] = jnp.exp(g_col[c, e])
        rhs = jnp.concatenate([kf[c] * (beta_col[c, e] * exp_g[c, e]), ve * beta_col[c, e]], axis=1)
        sol[c, e] = _dot(t_mat.astype(BF16), rhs.astype(BF16))
    recurrence(7)
    for c in range(n_chunks):
        qm, o_n, cd = [], [], []
        for e in range(2):
            a_qk = (qk2[c] * decay[c, e])[:, :CHUNK].astype(BF16)
            qo = _dot(a_qk, sol[c, e].astype(BF16))
            q_eff = qf[c] * exp_g[c, e] - qo[:, :HEAD_DIM]
            k_tail = jnp.exp(g_last[c, e] - g_col[c, e])[:, :1]
            mn = _dot(k_t[c], (sol[c, e] * k_tail).astype(BF16))
            qm.append(jnp.concatenate([q_eff, mn[:, :HEAD_DIM]], axis=0).astype(BF16))
            o_n.append(jnp.concatenate([qo[:, HEAD_DIM:], mn[:, HEAD_DIM:]], axis=0))
            cd.append(jnp.broadcast_to(jnp.exp(g_last[c, e]), (SUBLANES, LANES)))
        qm_ref[slot_a, c] = jnp.concatenate(qm, axis=1)
        on_ref[slot_a, c] = jnp.concatenate(o_n, axis=1)
        cd_ref[slot_a, c] = jnp.concatenate(cd, axis=1)

    state_ref[...] = state[0]


def _dn_chunk(qk, v, z, gate_c, gate_t, norm_w, batch, seq, n_kh, tc):
    m = qk.shape[0]
    nt = seq // tc
    cpt = tc // CHUNK
    assert cpt % RECURRENCE_SLOTS == 0

    n_tiles = batch * n_kh * nt

    def split(t):
        return (t // (n_kh * nt)) * nt + t % nt, (t // nt) % n_kh

    def prep(g):
        return split(jnp.minimum(g, n_tiles - 1))

    def rec(g):
        return split(jnp.maximum(g - 1, 0))

    return pl.pallas_call(
        functools.partial(_dn_chunk_kernel, nt=nt, n_kh=n_kh),
        grid=(n_tiles + 1,),
        in_specs=[
            pl.BlockSpec((tc, HEAD_DIM), lambda g: prep(g)),
            pl.BlockSpec((tc, HEAD_DIM), lambda g: (prep(g)[0], n_kh + prep(g)[1])),
            pl.BlockSpec((tc, 2 * HEAD_DIM), lambda g: prep(g)),
            pl.BlockSpec((tc, 2 * HEAD_DIM), lambda g: rec(g)),
            pl.BlockSpec((tc, gate_c.shape[1]), lambda g: (prep(g)[0], 0)),
            pl.BlockSpec((None, cpt, SUBLANES, LANES), lambda g: (prep(g)[1], prep(g)[0], 0, 0)),
            pl.BlockSpec((1, HEAD_DIM), lambda g: (0, 0)),
        ],
        out_specs=pl.BlockSpec((tc, 2 * HEAD_DIM), lambda g: rec(g)),
        out_shape=jax.ShapeDtypeStruct((m, 2 * n_kh * HEAD_DIM), BF16),
        scratch_shapes=[pltpu.VMEM((HEAD_DIM, 2 * HEAD_DIM), F32),
                        pltpu.VMEM((2, cpt, CHUNK + HEAD_DIM, 2 * HEAD_DIM), BF16),
                        pltpu.VMEM((2, cpt, CHUNK + HEAD_DIM, 2 * HEAD_DIM), F32),
                        pltpu.VMEM((2, cpt, SUBLANES, 2 * LANES), F32)],
        compiler_params=_cparams(1),
        name="dn_chunk",
    )(qk, qk, v, z, gate_c, gate_t, norm_w)


def _tile(n, want):
    if n <= want:
        return n
    t = want - want % LANES
    while n % t:
        t -= LANES
    return t


def _gate_slots(w_beta_a, a_log, dt_bias, n_vh):
    n_kh = n_vh // 2
    d = w_beta_a.shape[0]
    wb = w_beta_a[:, :n_vh].reshape(d, n_kh, 2)
    wa = w_beta_a[:, n_vh:].reshape(d, n_kh, 2)
    w_slot = jnp.concatenate([wb, wa, jnp.zeros((d, n_kh, SUBLANES - 4), w_beta_a.dtype)], axis=2)
    zeros2 = jnp.zeros((n_kh, 2), F32)
    zeros4 = jnp.zeros((n_kh, SUBLANES - 4), F32)
    al = jnp.concatenate([zeros2, a_log.astype(F32).reshape(n_kh, 2), zeros4], axis=1)
    dt = jnp.concatenate([zeros2, dt_bias.astype(F32).reshape(n_kh, 2), zeros4], axis=1)
    pad = (-n_kh * SUBLANES) % LANES
    w_slot = jnp.pad(w_slot.reshape(d, n_kh * SUBLANES), ((0, 0), (0, pad)))
    return w_slot, jnp.pad(al.reshape(-1), (0, pad)), jnp.pad(dt.reshape(-1), (0, pad))


def kernel(x, sc_w_in, sc_conv_w, sc_w_out, dn_w_in, dn_conv_w, dn_a_log, dn_dt_bias, dn_norm_w,
           dn_w_out, ffn_w_gate_up, ffn_w_down, ln_gain, ln_bias):
    batch, seq, d = x.shape
    depth = ln_gain.shape[0]
    alpha = float((2 * depth) ** 0.25)
    m = batch * seq
    n_vh = dn_a_log.shape[1]
    n_kh = n_vh // 2
    key_dim = n_kh * HEAD_DIM
    val_dim = n_vh * HEAD_DIM
    qkv_dim = 2 * key_dim + val_dim
    hidden = ffn_w_down.shape[1]
    assert seq % CHUNK == 0 and d % LANES == 0 and dn_norm_w.shape[1] == HEAD_DIM

    tm_in = _tile(seq, 1024)
    tm_dn = _tile(seq, 2048)
    tm_out = _tile(seq, 512)
    tn = _tile(d, 512)
    th = _tile(hidden, 512)
    tc = _tile(seq, 1024)

    sc_w_in_b, sc_w_out_b = sc_w_in.astype(BF16), sc_w_out.astype(BF16)
    dn_w_in_b, dn_w_out_b = dn_w_in.astype(BF16), dn_w_out.astype(BF16)
    ffn_w_gu_b, ffn_w_down_b = ffn_w_gate_up.astype(BF16), ffn_w_down.astype(BF16)

    h = x.reshape(m, d)
    hb = h.astype(BF16)
    for i in range(depth):
        l = i // 2
        gb0 = jnp.stack([ln_gain[i, 0], ln_bias[i, 0]])
        gb1 = jnp.stack([ln_gain[i, 1], ln_bias[i, 1]])
        if i % 2 == 0:
            a = _sc_in(hb, sc_w_in_b, l, sc_conv_w[l], seq, tm_in, tn)
            h = _proj_ln(a, sc_w_out_b, l, h, gb0, alpha, tm_out)
        else:
            cw = dn_conv_w[l]
            qk = _dn_in(hb, dn_w_in_b, l, cw, 0, 2 * key_dim, seq, tm_dn, tn, conv=True, l2norm=True,
                        n_scaled=key_dim // tn, name="dn_in_qk")
            v = _dn_in(hb, dn_w_in_b, l, cw, 2 * key_dim, val_dim, seq, tm_dn, tn, conv=True, name="dn_in_v")
            z = _dn_in(hb, dn_w_in_b, l, cw, qkv_dim, val_dim, seq, tm_dn, tn, conv=False, name="dn_in_z")
            w_slot, al_slot, dt_slot = _gate_slots(dn_w_in[l][:, qkv_dim + val_dim:], dn_a_log[l],
                                                   dn_dt_bias[l], n_vh)
            gate_c, gate_t = _dn_gates(hb, w_slot.astype(BF16), al_slot, dt_slot, tm_out)
            gate_t = gate_t[:n_kh * SUBLANES].reshape(n_kh, SUBLANES, m // CHUNK, CHUNK)
            gate_t = jnp.transpose(gate_t, (0, 2, 1, 3))
            gate_t = jnp.concatenate([gate_t, gate_t], axis=-1)
            o = _dn_chunk(qk, v, z, gate_c, gate_t, dn_norm_w[l][None, :], batch, seq, n_kh, tc)
            h = _proj_ln(o, dn_w_out_b, l, h, gb0, alpha, tm_out)
        h, hb = _ffn(h, ffn_w_gu_b, ffn_w_down_b, i, gb1, alpha, tm_out, th, emit_bf16=i + 1 < depth)
    return h.reshape(batch, seq, d)
```

```python
import functools

import jax
import jax.numpy as jnp
from jax import lax
from jax.experimental import pallas as pl
from jax.experimental.pallas import tpu as pltpu

HEAD_DIM = 128
CHUNK = 64
SUBLANES = 8
LANES = 128
LN_EPS = 1e-5
RMS_EPS = 1e-6
NEG_BIG = -1e30
VMEM_LIMIT_BYTES = 60 * 1024 * 1024

F32 = jnp.float32
BF16 = jnp.bfloat16


def _cparams(n_axes):
    return pltpu.CompilerParams(
        dimension_semantics=("arbitrary",) * n_axes,
        vmem_limit_bytes=VMEM_LIMIT_BYTES)


def _dot(a, b):
    return jnp.dot(a, b, preferred_element_type=F32)


def _dot_nt(a, b):
    return lax.dot_general(a, b, (((1,), (1,)), ((), ())), preferred_element_type=F32)


def _dot_tn(a, b):
    return lax.dot_general(a, b, (((0,), (0,)), ((), ())), preferred_element_type=F32)


def _layer_norm(y, gain, bias):
    mu = jnp.mean(y, axis=-1, keepdims=True)
    d = y - mu
    var = jnp.mean(d * d, axis=-1, keepdims=True)
    return d * lax.rsqrt(var + LN_EPS) * gain + bias


def _silu(t):
    return t * jax.nn.sigmoid(t)


def _causal_conv(u, prev, w_ref):
    ksize = w_ref.shape[0]
    out = u * w_ref[ksize - 1:ksize, :]
    head = jnp.concatenate([prev, u[0:SUBLANES]], axis=0)
    for d in range(1, ksize):
        shifted = pltpu.roll(u, d, axis=0)
        top = pltpu.roll(head, d, axis=0)[SUBLANES:2 * SUBLANES]
        shifted = jnp.concatenate([top, shifted[SUBLANES:]], axis=0)
        out = out + shifted * w_ref[ksize - 1 - d:ksize - d, :]
    return out


def _causal_taps(ext, w_rows):
    k = len(w_rows)
    assert k <= SUBLANES
    s1 = pltpu.roll(ext, 1, axis=0)
    out = None
    for d0 in range(0, k, 2):
        term = ext * w_rows[k - 1 - d0]
        if d0 + 1 < k:
            term = term + s1 * w_rows[k - 2 - d0]
        if d0:
            term = pltpu.roll(term, d0, axis=0)
        out = term if out is None else out + term
    return out[SUBLANES:]


def _conv_history(carry_ref, j, i, tiles_per_seq):
    prev = carry_ref[j]
    return jnp.where(i % tiles_per_seq == 0, jnp.zeros_like(prev), prev)


def _sc_in_kernel(xb_ref, wb_ref, wc_ref, wh_ref, cw_ref, o_ref, carry_ref, *, tiles_per_seq):
    i, j = pl.program_id(0), pl.program_id(1)
    xb = xb_ref[...]
    u = _dot(xb, wc_ref[...]) * _dot(xb, wh_ref[...])
    prev = _conv_history(carry_ref, j, i, tiles_per_seq)
    y = _causal_conv(u, prev, cw_ref)
    carry_ref[j] = u[u.shape[0] - SUBLANES:]
    o_ref[...] = (_dot(xb, wb_ref[...]) * y).astype(o_ref.dtype)


def _sc_in(xb, w_in, layer, conv_w, seq, tm, tn):
    m, d = xb.shape
    nd = d // tn
    return pl.pallas_call(
        functools.partial(_sc_in_kernel, tiles_per_seq=seq // tm),
        grid=(m // tm, nd),
        in_specs=[
            pl.BlockSpec((tm, d), lambda i, j: (i, 0)),
            pl.BlockSpec((None, d, tn), lambda i, j: (layer, 0, j)),
            pl.BlockSpec((None, d, tn), lambda i, j: (layer, 0, nd + j)),
            pl.BlockSpec((None, d, tn), lambda i, j: (layer, 0, 2 * nd + j)),
            pl.BlockSpec((conv_w.shape[0], tn), lambda i, j: (0, j)),
        ],
        out_specs=pl.BlockSpec((tm, tn), lambda i, j: (i, j)),
        out_shape=jax.ShapeDtypeStruct((m, d), BF16),
        scratch_shapes=[pltpu.VMEM((nd, SUBLANES, tn), F32)],
        compiler_params=_cparams(2),
        name="sc_in",
    )(xb, w_in, w_in, w_in, conv_w)


def _residual_ln_blocks(n_sub, rs, pre_norm, gb_ref, out_refs):
    gain, bias = gb_ref[0:1, :], gb_ref[1:2, :]

    def finish(b, y):
        out = _layer_norm(y, gain, bias)
        for r in out_refs:
            r[b * rs:(b + 1) * rs, :] = out.astype(r.dtype)

    pending = pre_norm(0)
    for b in range(1, n_sub):
        nxt = pre_norm(b)
        finish(b - 1, pending)
        pending = nxt
    finish(n_sub - 1, pending)


def _proj_ln_kernel(a_ref, w_ref, x_ref, gb_ref, o_ref, *, alpha, n_sub):
    rs = o_ref.shape[0] // n_sub

    def pre_norm(b):
        rows = slice(b * rs, (b + 1) * rs)
        return alpha * x_ref[rows, :] + _dot(a_ref[rows, :], w_ref[...])

    _residual_ln_blocks(n_sub, rs, pre_norm, gb_ref, [o_ref])


def _proj_ln(a, w, layer, x, gain_bias, alpha, tm):
    m, d = x.shape
    k = a.shape[1]
    return pl.pallas_call(
        functools.partial(_proj_ln_kernel, alpha=alpha, n_sub=max(1, tm // 128)),
        grid=(m // tm,),
        in_specs=[
            pl.BlockSpec((tm, k), lambda i: (i, 0)),
            pl.BlockSpec((None, k, d), lambda i: (layer, 0, 0), pipeline_mode=pl.Buffered(1)),
            pl.BlockSpec((tm, d), lambda i: (i, 0)),
            pl.BlockSpec((2, d), lambda i: (0, 0)),
        ],
        out_specs=pl.BlockSpec((tm, d), lambda i: (i, 0)),
        out_shape=jax.ShapeDtypeStruct((m, d), F32),
        compiler_params=_cparams(1),
        name="proj_ln",
    )(a, w, x, gain_bias)


def _ffn_kernel(x_ref, wg_ref, wu_ref, wd_ref, gb_ref, o_ref, *rest, alpha):
    *ob_ref, xb_ref, acc_ref = rest
    j = pl.program_id(1)

    @pl.when(j == 0)
    def _():
        xb_ref[...] = x_ref[...].astype(BF16)
        acc_ref[...] = jnp.zeros_like(acc_ref)

    xb = xb_ref[...]
    act = _silu(_dot(xb, wg_ref[...])) * _dot(xb, wu_ref[...])
    acc_ref[...] += _dot(act.astype(BF16), wd_ref[...])

    @pl.when(j == pl.num_programs(1) - 1)
    def _():
        y = alpha * x_ref[...] + acc_ref[...]
        out = _layer_norm(y, gb_ref[0:1, :], gb_ref[1:2, :])
        o_ref[...] = out
        for r in ob_ref:
            r[...] = out.astype(BF16)


def _ffn(x, w_gate_up, w_down, layer, gain_bias, alpha, tm, th, emit_bf16):
    m, d = x.shape
    hidden = w_down.shape[1]
    nh = hidden // th
    row_block = pl.BlockSpec((tm, d), lambda i, j: (i, 0))
    outs = pl.pallas_call(
        functools.partial(_ffn_kernel, alpha=alpha),
        grid=(m // tm, nh),
        in_specs=[
            row_block,
            pl.BlockSpec((None, d, th), lambda i, j: (layer, 0, j)),
            pl.BlockSpec((None, d, th), lambda i, j: (layer, 0, nh + j)),
            pl.BlockSpec((None, th, d), lambda i, j: (layer, j, 0)),
            pl.BlockSpec((2, d), lambda i, j: (0, 0)),
        ],
        out_specs=[row_block] * (2 if emit_bf16 else 1),
        out_shape=[jax.ShapeDtypeStruct((m, d), F32)] + [jax.ShapeDtypeStruct((m, d), BF16)] * emit_bf16,
        scratch_shapes=[pltpu.VMEM((tm, d), BF16), pltpu.VMEM((tm, d), F32)],
        compiler_params=_cparams(2),
        name="ffn",
    )(x, w_gate_up, w_gate_up, w_down, gain_bias)
    return outs if emit_bf16 else (outs[0], None)


def _dn_in_kernel(xb_ref, w_ref, cw_ref, o_ref, *scratch, tiles_per_seq, conv, l2norm, n_scaled, n_sub):
    i, j = pl.program_id(0), pl.program_id(1)
    tm = o_ref.shape[0]
    rs = tm // n_sub

    def block_dot(b):
        return _dot(xb_ref[b * rs:(b + 1) * rs, :], w_ref[...])

    if not conv:
        for b in range(n_sub):
            o_ref[b * rs:(b + 1) * rs, :] = block_dot(b).astype(o_ref.dtype)
        return

    acc0_ref, acc1_ref, carry_ref = scratch
    acc = (acc0_ref, acc1_ref)
    ksize = cw_ref.shape[0]

    @pl.when(i % tiles_per_seq == 0)
    def _():
        acc0_ref[0:SUBLANES, :] = jnp.zeros((SUBLANES, acc0_ref.shape[1]), F32)

    @pl.when(i % tiles_per_seq != 0)
    def _():
        acc0_ref[0:SUBLANES, :] = carry_ref[j]

    scale = jnp.where(j < n_scaled, HEAD_DIM ** -0.5, 1.0).astype(F32)

    def epilogue(b):
        src = acc[b % 2]
        for s in range(o_ref.shape[1] // HEAD_DIM):
            lanes = slice(s * HEAD_DIM, (s + 1) * HEAD_DIM)
            t = _silu(_causal_taps(src[:, lanes], [cw_ref[k:k + 1, lanes] for k in range(ksize)]))
            if l2norm:
                t = t * (lax.rsqrt(jnp.sum(t * t, axis=-1, keepdims=True) + RMS_EPS) * scale)
            o_ref[b * rs:(b + 1) * rs, lanes] = t.astype(o_ref.dtype)

    tail = None
    for b in range(n_sub):
        res = block_dot(b)
        acc[b % 2][SUBLANES:SUBLANES + rs, :] = res
        if b > 0:
            epilogue(b - 1)
            acc[b % 2][0:SUBLANES, :] = tail
        tail = res[rs - SUBLANES:, :]
    epilogue(n_sub - 1)
    carry_ref[j] = tail


def _dn_in(xb, w_in, layer, conv_w, col0, width, seq, tm, tn, *, conv, l2norm=False, n_scaled=0, name):
    m, d = xb.shape
    nj = width // tn
    j0 = col0 // tn
    n_sub = max(1, tm // 128)
    blk = pltpu.VMEM((tm // n_sub + SUBLANES, tn), F32)
    scratch = [blk, blk, pltpu.VMEM((nj, SUBLANES, tn), F32)] if conv else []
    cw_j0 = j0 if conv else 0
    return pl.pallas_call(
        functools.partial(_dn_in_kernel, tiles_per_seq=seq // tm, conv=conv, l2norm=l2norm,
                          n_scaled=n_scaled, n_sub=n_sub),
        grid=(m // tm, nj),
        in_specs=[
            pl.BlockSpec((tm, d), lambda i, j: (i, 0)),
            pl.BlockSpec((None, d, tn), lambda i, j: (layer, 0, j0 + j)),
            pl.BlockSpec((conv_w.shape[0], tn), lambda i, j: (0, cw_j0 + j if conv else 0)),
        ],
        out_specs=pl.BlockSpec((tm, tn), lambda i, j: (i, j)),
        out_shape=jax.ShapeDtypeStruct((m, width), BF16),
        scratch_shapes=scratch,
        compiler_params=_cparams(2),
        name=name,
    )(xb, w_in, conv_w)


def _softplus(t):
    return jnp.maximum(t, 0.0) + jnp.log1p(jnp.exp(-jnp.abs(t)))


def _chunk_cumsum(g, axis):
    pos = lax.broadcasted_iota(jnp.int32, g.shape, axis) % CHUNK
    s = 1
    while s < CHUNK:
        g = g + jnp.where(pos >= s, pltpu.roll(g, s, axis=axis), 0.0)
        s *= 2
    return g


def _gates(raw, a_log, dt_bias, slot_axis, token_axis):
    c = lax.broadcasted_iota(jnp.int32, raw.shape, slot_axis) % SUBLANES
    beta = jax.nn.sigmoid(raw)
    g = -jnp.exp(a_log) * _softplus(raw + dt_bias)
    g = _chunk_cumsum(jnp.where(jnp.logical_and(c >= 2, c < 4), g, 0.0), token_axis)
    return jnp.where(c < 2, beta, g)


def _dn_gate_kernel(x_ref, w_ref, wt_ref, al_ref, dt_ref, alt_ref, dtt_ref, gc_ref, gt_ref):
    xb = x_ref[...]
    gc_ref[...] = _gates(_dot(xb, w_ref[...]), al_ref[...], dt_ref[...], 1, 0)
    gt_ref[...] = _gates(_dot_nt(wt_ref[...], xb), alt_ref[...], dtt_ref[...], 0, 1)


def _dn_gates(x, w_slot, a_log_slot, dt_slot, tm):
    m, d = x.shape
    ns = w_slot.shape[1]
    return pl.pallas_call(
        _dn_gate_kernel,
        grid=(m // tm,),
        in_specs=[
            pl.BlockSpec((tm, d), lambda i: (i, 0)),
            pl.BlockSpec((d, ns), lambda i: (0, 0)),
            pl.BlockSpec((ns, d), lambda i: (0, 0)),
            pl.BlockSpec((1, ns), lambda i: (0, 0)),
            pl.BlockSpec((1, ns), lambda i: (0, 0)),
            pl.BlockSpec((ns, 1), lambda i: (0, 0)),
            pl.BlockSpec((ns, 1), lambda i: (0, 0)),
        ],
        out_specs=[pl.BlockSpec((tm, ns), lambda i: (i, 0)),
                   pl.BlockSpec((ns, tm), lambda i: (0, i))],
        out_shape=[jax.ShapeDtypeStruct((m, ns), F32), jax.ShapeDtypeStruct((ns, m), F32)],
        compiler_params=_cparams(1),
        name="dn_gates",
    )(x, w_slot, w_slot.T, a_log_slot[None, :], dt_slot[None, :], a_log_slot[:, None], dt_slot[:, None])


def _block_diag(a, b):
    zero = jnp.zeros_like(a)
    return jnp.concatenate([jnp.concatenate([a, zero], axis=1), jnp.concatenate([zero, b], axis=1)], axis=0)


def _dn_chunk_kernel(q_ref, k_ref, v_ref, z_ref, gc_ref, gt_ref, nw_ref, o_ref,
                     state_ref, qm_ref, on_ref, cd_ref, *, nt, n_kh):
    g = pl.program_id(0)
    kh = (jnp.minimum(g, pl.num_programs(0) - 2) // nt) % n_kh
    n_chunks = q_ref.shape[0] // CHUNK
    slot_a = g % 2
    slot_b = 1 - slot_a

    @pl.when(g == 0)
    def _():
        qm_ref[1] = jnp.zeros(qm_ref.shape[1:], qm_ref.dtype)
        on_ref[1] = jnp.zeros(on_ref.shape[1:], on_ref.dtype)
        cd_ref[1] = jnp.zeros(cd_ref.shape[1:], cd_ref.dtype)

    @pl.when(jnp.maximum(g - 1, 0) % nt == 0)
    def _():
        state_ref[...] = jnp.zeros_like(state_ref)

    row = lax.broadcasted_iota(jnp.int32, (CHUNK, LANES), 0)
    lane = lax.broadcasted_iota(jnp.int32, (CHUNK, LANES), 1)
    col = lane % CHUNK
    causal = col <= row
    strict = col < row
    upper_half = lane >= CHUNK
    lower_half = lane < CHUNK
    eye = col == row
    eye_hi = jnp.where(jnp.logical_and(upper_half, eye), 1.0, 0.0).astype(F32)
    eye_lo = jnp.where(jnp.logical_and(lower_half, eye), 1.0, 0.0).astype(F32)
    norm_w = nw_ref[...]
    state = [state_ref[...]]

    def recurrence_chunk(c):
        rows = slice(c * CHUNK, (c + 1) * CHUNK)
        s_b = state[0].astype(BF16)
        qms = _dot(qm_ref[slot_b, c], _block_diag(s_b[:, :HEAD_DIM], s_b[:, HEAD_DIM:]))
        o_n = on_ref[slot_b, c]
        o_all = qms[:CHUNK] + o_n[:CHUNK]
        state[0] = state[0] * cd_ref[slot_b, c][:1, :] + o_n[CHUNK:] - qms[CHUNK:]
        for e in range(2):
            o = o_all[:, e * HEAD_DIM:(e + 1) * HEAD_DIM]
            o = o * lax.rsqrt(jnp.mean(o * o, axis=-1, keepdims=True) + RMS_EPS) * norm_w
            ze = z_ref[rows, e * HEAD_DIM:(e + 1) * HEAD_DIM].astype(F32)
            o_ref[rows, e * HEAD_DIM:(e + 1) * HEAD_DIM] = (o * _silu(ze)).astype(o_ref.dtype)

    gsel_all = pltpu.roll(gc_ref[...], (LANES - SUBLANES * kh) % LANES, axis=1)
    kf, qf, k_t, qk2, kk2 = {}, {}, {}, {}, {}
    beta_col, g_col, g_last, decay, x_mat, w_mat, sol, exp_g = {}, {}, {}, {}, {}, {}, {}, {}

    def load_chunk(c):
        rows = slice(c * CHUNK, (c + 1) * CHUNK)
        qc, kc = q_ref[rows, :], k_ref[rows, :]
        qf[c] = qc.astype(F32)
        kf[c] = kc.astype(F32)
        k_t[c] = jnp.transpose(kf[c]).astype(BF16)
        qkk = _dot_nt(jnp.concatenate([qc, kc], axis=0), jnp.concatenate([kc, kc], axis=0))
        qk2[c] = qkk[:CHUNK]
        kk2[c] = qkk[CHUNK:]

    def gate_terms(c, e):
        gsel = gsel_all[c * CHUNK:(c + 1) * CHUNK]
        g_row = gt_ref[c][2 + e:3 + e, :]
        beta_col[c, e] = jnp.broadcast_to(gsel[:, e:e + 1], (CHUNK, LANES))
        g_col[c, e] = jnp.broadcast_to(gsel[:, 2 + e:3 + e], (CHUNK, LANES))
        g_last[c, e] = g_row[:, CHUNK - 1:CHUNK]
        decay[c, e] = jnp.exp(jnp.where(causal, g_col[c, e] - g_row, NEG_BIG))
        x_mat[c, e] = jnp.where(strict, -(kk2[c] * beta_col[c, e] * decay[c, e]), 0.0)

    def square(c):
        xa, xb = x_mat[c, 0], x_mat[c, 1]
        sq = _dot(jnp.where(lower_half, xa, xb).astype(BF16), _block_diag(xa.astype(BF16), xb.astype(BF16)))
        w_mat[c, 0] = jnp.where(upper_half, xa + eye_hi, sq[:, :LANES])
        w_mat[c, 1] = jnp.where(upper_half, sq[:, LANES:], xb + eye_lo)

    def advance(c):
        wa, wb = w_mat[c, 0], w_mat[c, 1]
        pw = _dot(jnp.where(lower_half, wa, wb).astype(BF16), _block_diag(wa.astype(BF16), wb.astype(BF16)))
        w_mat[c, 0] = pw[:, :LANES] + jnp.where(upper_half, wa, 0.0)
        w_mat[c, 1] = pw[:, LANES:] + jnp.where(lower_half, wb, 0.0)

    def solve(c, e):
        t_mat = w_mat[c, e][:, CHUNK:] if e == 0 else w_mat[c, e][:, :CHUNK]
        ve = v_ref[c * CHUNK:(c + 1) * CHUNK, e * HEAD_DIM:(e + 1) * HEAD_DIM].astype(F32)
        exp_g[c, e] = jnp.exp(g_col[c, e])
        rhs = jnp.concatenate([kf[c] * (beta_col[c, e] * exp_g[c, e]), ve * beta_col[c, e]], axis=1)
        sol[c, e] = _dot(t_mat.astype(BF16), rhs.astype(BF16))

    def finish(c):
        qm, o_n, cd = [], [], []
        for e in range(2):
            a_qk = (qk2[c] * decay[c, e])[:, :CHUNK].astype(BF16)
            qo = _dot(a_qk, sol[c, e].astype(BF16))
            q_eff = qf[c] * exp_g[c, e] - qo[:, :HEAD_DIM]
            k_tail = jnp.exp(g_last[c, e] - g_col[c, e])[:, :1]
            mn = _dot(k_t[c], (sol[c, e] * k_tail).astype(BF16))
            qm.append(jnp.concatenate([q_eff, mn[:, :HEAD_DIM]], axis=0).astype(BF16))
            o_n.append(jnp.concatenate([qo[:, HEAD_DIM:], mn[:, HEAD_DIM:]], axis=0))
            cd.append(jnp.broadcast_to(jnp.exp(g_last[c, e]), (SUBLANES, LANES)))
        qm_ref[slot_a, c] = jnp.concatenate(qm, axis=1)
        on_ref[slot_a, c] = jnp.concatenate(o_n, axis=1)
        cd_ref[slot_a, c] = jnp.concatenate(cd, axis=1)

    chunks = range(n_chunks)
    chains = [(c, e) for c in chunks for e in range(2)]
    items = [functools.partial(load_chunk, c) for c in chunks]
    items += [functools.partial(gate_terms, c, e) for (c, e) in chains]
    items += [functools.partial(square, c) for c in chunks]
    items += [functools.partial(advance, c) for _ in range(5) for c in chunks]
    items += [functools.partial(solve, c, e) for (c, e) in chains]
    items += [functools.partial(finish, c) for c in chunks]
    for idx, item in enumerate(items):
        for c in chunks:
            if idx == c * len(items) // n_chunks:
                recurrence_chunk(c)
        item()

    state_ref[...] = state[0]


def _dn_chunk(qk, v, z, gate_c, gate_t, norm_w, batch, seq, n_kh, tc):
    m = qk.shape[0]
    nt = seq // tc
    cpt = tc // CHUNK

    n_tiles = batch * n_kh * nt

    def split(t):
        return (t // (n_kh * nt)) * nt + t % nt, (t // nt) % n_kh

    def prep(g):
        return split(jnp.minimum(g, n_tiles - 1))

    def rec(g):
        return split(jnp.maximum(g - 1, 0))

    return pl.pallas_call(
        functools.partial(_dn_chunk_kernel, nt=nt, n_kh=n_kh),
        grid=(n_tiles + 1,),
        in_specs=[
            pl.BlockSpec((tc, HEAD_DIM), lambda g: prep(g)),
            pl.BlockSpec((tc, HEAD_DIM), lambda g: (prep(g)[0], n_kh + prep(g)[1])),
            pl.BlockSpec((tc, 2 * HEAD_DIM), lambda g: prep(g)),
            pl.BlockSpec((tc, 2 * HEAD_DIM), lambda g: rec(g)),
            pl.BlockSpec((tc, gate_c.shape[1]), lambda g: (prep(g)[0], 0)),
            pl.BlockSpec((None, cpt, SUBLANES, LANES), lambda g: (prep(g)[1], prep(g)[0], 0, 0)),
            pl.BlockSpec((1, HEAD_DIM), lambda g: (0, 0)),
        ],
        out_specs=pl.BlockSpec((tc, 2 * HEAD_DIM), lambda g: rec(g)),
        out_shape=jax.ShapeDtypeStruct((m, 2 * n_kh * HEAD_DIM), BF16),
        scratch_shapes=[pltpu.VMEM((HEAD_DIM, 2 * HEAD_DIM), F32),
                        pltpu.VMEM((2, cpt, CHUNK + HEAD_DIM, 2 * HEAD_DIM), BF16),
                        pltpu.VMEM((2, cpt, CHUNK + HEAD_DIM, 2 * HEAD_DIM), F32),
                        pltpu.VMEM((2, cpt, SUBLANES, 2 * LANES), F32)],
        compiler_params=_cparams(1),
        name="dn_chunk",
    )(qk, qk, v, z, gate_c, gate_t, norm_w)


def _tile(n, want):
    if n <= want:
        return n
    t = want - want % LANES
    while n % t:
        t -= LANES
    return t


def _gate_slots(w_beta_a, a_log, dt_bias, n_vh):
    n_kh = n_vh // 2
    d = w_beta_a.shape[0]
    wb = w_beta_a[:, :n_vh].reshape(d, n_kh, 2)
    wa = w_beta_a[:, n_vh:].reshape(d, n_kh, 2)
    w_slot = jnp.concatenate([wb, wa, jnp.zeros((d, n_kh, SUBLANES - 4), w_beta_a.dtype)], axis=2)
    zeros2 = jnp.zeros((n_kh, 2), F32)
    zeros4 = jnp.zeros((n_kh, SUBLANES - 4), F32)
    al = jnp.concatenate([zeros2, a_log.astype(F32).reshape(n_kh, 2), zeros4], axis=1)
    dt = jnp.concatenate([zeros2, dt_bias.astype(F32).reshape(n_kh, 2), zeros4], axis=1)
    pad = (-n_kh * SUBLANES) % LANES
    w_slot = jnp.pad(w_slot.reshape(d, n_kh * SUBLANES), ((0, 0), (0, pad)))
    return w_slot, jnp.pad(al.reshape(-1), (0, pad)), jnp.pad(dt.reshape(-1), (0, pad))


def kernel(x, sc_w_in, sc_conv_w, sc_w_out, dn_w_in, dn_conv_w, dn_a_log, dn_dt_bias, dn_norm_w,
           dn_w_out, ffn_w_gate_up, ffn_w_down, ln_gain, ln_bias):
    batch, seq, d = x.shape
    depth = ln_gain.shape[0]
    alpha = float((2 * depth) ** 0.25)
    m = batch * seq
    n_vh = dn_a_log.shape[1]
    n_kh = n_vh // 2
    key_dim = n_kh * HEAD_DIM
    val_dim = n_vh * HEAD_DIM
    qkv_dim = 2 * key_dim + val_dim
    hidden = ffn_w_down.shape[1]
    assert seq % CHUNK == 0 and d % LANES == 0 and dn_norm_w.shape[1] == HEAD_DIM

    tm_in = _tile(seq, 1024)
    tm_dn = _tile(seq, 2048)
    tm_out = _tile(seq, 512)
    tn = _tile(d, 512)
    th = _tile(hidden, 512)
    tc = _tile(seq, 1024)

    sc_w_in_b, sc_w_out_b = sc_w_in.astype(BF16), sc_w_out.astype(BF16)
    dn_w_in_b, dn_w_out_b = dn_w_in.astype(BF16), dn_w_out.astype(BF16)
    ffn_w_gu_b, ffn_w_down_b = ffn_w_gate_up.astype(BF16), ffn_w_down.astype(BF16)

    h = x.reshape(m, d)
    hb = h.astype(BF16)
    for i in range(depth):
        l = i // 2
        gb0 = jnp.stack([ln_gain[i, 0], ln_bias[i, 0]])
        gb1 = jnp.stack([ln_gain[i, 1], ln_bias[i, 1]])
        if i % 2 == 0:
            a = _sc_in(hb, sc_w_in_b, l, sc_conv_w[l], seq, tm_in, tn)
            h = _proj_ln(a, sc_w_out_b, l, h, gb0, alpha, tm_out)
        else:
            cw = dn_conv_w[l]
            qk = _dn_in(hb, dn_w_in_b, l, cw, 0, 2 * key_dim, seq, tm_dn, tn, conv=True, l2norm=True,
                        n_scaled=key_dim // tn, name="dn_in_qk")
            v = _dn_in(hb, dn_w_in_b, l, cw, 2 * key_dim, val_dim, seq, tm_dn, tn, conv=True, name="dn_in_v")
            z = _dn_in(hb, dn_w_in_b, l, cw, qkv_dim, val_dim, seq, tm_dn, tn, conv=False, name="dn_in_z")
            w_slot, al_slot, dt_slot = _gate_slots(dn_w_in_b[l][:, qkv_dim + val_dim:], dn_a_log[l],
                                                   dn_dt_bias[l], n_vh)
            gate_c, gate_t = _dn_gates(hb, w_slot, al_slot, dt_slot, tm_out)
            gate_t = gate_t[:n_kh * SUBLANES].reshape(n_kh, SUBLANES, m // CHUNK, CHUNK)
            gate_t = jnp.transpose(gate_t, (0, 2, 1, 3))
            gate_t = jnp.concatenate([gate_t, gate_t], axis=-1)
            o = _dn_chunk(qk, v, z, gate_c, gate_t, dn_norm_w[l][None, :], batch, seq, n_kh, tc)
            h = _proj_ln(o, dn_w_out_b, l, h, gb0, alpha, tm_out)
        h, hb = _ffn(h, ffn_w_gu_b, ffn_w_down_b, i, gb1, alpha, tm_out, th, emit_bf16=i + 1 < depth)
    return h.reshape(batch, seq, d)
```

```python
import functools

import jax
import jax.numpy as jnp
from jax import lax
from jax.experimental import pallas as pl
from jax.experimental.pallas import tpu as pltpu

HEAD_DIM = 128
CHUNK = 64
SUBLANES = 8
LANES = 128
LN_EPS = 1e-5
RMS_EPS = 1e-6
NEG_BIG = -1e30
PREP_GROUP = 16
VMEM_LIMIT_BYTES = 60 * 1024 * 1024

F32 = jnp.float32
BF16 = jnp.bfloat16


def _cparams(n_axes):
    return pltpu.CompilerParams(
        dimension_semantics=("arbitrary",) * n_axes,
        vmem_limit_bytes=VMEM_LIMIT_BYTES)


def _dot(a, b):
    return jnp.dot(a, b, preferred_element_type=F32)


def _dot_nt(a, b):
    return lax.dot_general(a, b, (((1,), (1,)), ((), ())), preferred_element_type=F32)


def _dot_tn(a, b):
    return lax.dot_general(a, b, (((0,), (0,)), ((), ())), preferred_element_type=F32)


def _layer_norm(y, gain, bias):
    mu = jnp.mean(y, axis=-1, keepdims=True)
    d = y - mu
    var = jnp.mean(d * d, axis=-1, keepdims=True)
    return d * lax.rsqrt(var + LN_EPS) * gain + bias


def _silu(t):
    return t * jax.nn.sigmoid(t)


def _causal_taps(ext, w_rows):
    k = len(w_rows)
    assert k <= SUBLANES
    s1 = pltpu.roll(ext, 1, axis=0)
    out = None
    for d0 in range(0, k, 2):
        term = ext * w_rows[k - 1 - d0]
        if d0 + 1 < k:
            term = term + s1 * w_rows[k - 2 - d0]
        if d0:
            term = pltpu.roll(term, d0, axis=0)
        out = term if out is None else out + term
    return out[SUBLANES:]


def _sc_in_kernel(x_ref, wb_ref, wc_ref, wh_ref, cw_ref, o_ref, *scratch, tiles_per_seq, n_sub):
    i, j = pl.program_id(0), pl.program_id(1)
    if x_ref.dtype == BF16:
        xb_ref = x_ref
        u0_ref, u1_ref, g0_ref, g1_ref, carry_ref = scratch
    else:
        xb_ref, u0_ref, u1_ref, g0_ref, g1_ref, carry_ref = scratch

        @pl.when(j == 0)
        def _():
            xb_ref[...] = x_ref[...].astype(BF16)

    ubuf, gbuf = (u0_ref, u1_ref), (g0_ref, g1_ref)
    tm = o_ref.shape[0]
    rs = tm // n_sub
    ksize = cw_ref.shape[0]

    @pl.when(i % tiles_per_seq == 0)
    def _():
        u0_ref[0:SUBLANES, :] = jnp.zeros((SUBLANES, u0_ref.shape[1]), F32)

    @pl.when(i % tiles_per_seq != 0)
    def _():
        u0_ref[0:SUBLANES, :] = carry_ref[j]

    def epilogue(b):
        for s in range(o_ref.shape[1] // LANES):
            lanes = slice(s * LANES, (s + 1) * LANES)
            y = _causal_taps(ubuf[b % 2][:, lanes], [cw_ref[k:k + 1, lanes] for k in range(ksize)])
            o_ref[b * rs:(b + 1) * rs, lanes] = (gbuf[b % 2][:, lanes] * y).astype(o_ref.dtype)

    tail = None
    for b in range(n_sub):
        xblk = xb_ref[b * rs:(b + 1) * rs, :]
        u = _dot(xblk, wc_ref[...]) * _dot(xblk, wh_ref[...])
        ubuf[b % 2][SUBLANES:SUBLANES + rs, :] = u
        gbuf[b % 2][...] = _dot(xblk, wb_ref[...])
        if b > 0:
            epilogue(b - 1)
            ubuf[b % 2][0:SUBLANES, :] = tail
        tail = u[rs - SUBLANES:, :]
    epilogue(n_sub - 1)
    carry_ref[j] = tail


def _sc_in(x, w_in, layer, conv_w, seq, tm, tn):
    m, d = x.shape
    nd = d // tn
    n_sub = max(1, tm // 128)
    rs = tm // n_sub
    ublk, gblk = pltpu.VMEM((rs + SUBLANES, tn), F32), pltpu.VMEM((rs, tn), F32)
    scratch = [ublk, ublk, gblk, gblk, pltpu.VMEM((nd, SUBLANES, tn), F32)]
    if x.dtype != BF16:
        scratch = [pltpu.VMEM((tm, d), BF16)] + scratch
    return pl.pallas_call(
        functools.partial(_sc_in_kernel, tiles_per_seq=seq // tm, n_sub=n_sub),
        grid=(m // tm, nd),
        in_specs=[
            pl.BlockSpec((tm, d), lambda i, j: (i, 0)),
            pl.BlockSpec((None, d, tn), lambda i, j: (layer, 0, j)),
            pl.BlockSpec((None, d, tn), lambda i, j: (layer, 0, nd + j)),
            pl.BlockSpec((None, d, tn), lambda i, j: (layer, 0, 2 * nd + j)),
            pl.BlockSpec((conv_w.shape[0], tn), lambda i, j: (0, j)),
        ],
        out_specs=pl.BlockSpec((tm, tn), lambda i, j: (i, j)),
        out_shape=jax.ShapeDtypeStruct((m, d), BF16),
        scratch_shapes=scratch,
        compiler_params=_cparams(2),
        name="sc_in",
    )(x, w_in, w_in, w_in, conv_w)


def _residual_ln_blocks(n_sub, rs, pre_norm, gb_ref, out_refs):
    gain, bias = gb_ref[0:1, :], gb_ref[1:2, :]

    def finish(b, y):
        out = _layer_norm(y, gain, bias)
        for r in out_refs:
            r[b * rs:(b + 1) * rs, :] = out.astype(r.dtype)

    pending = pre_norm(0)
    for b in range(1, n_sub):
        nxt = pre_norm(b)
        finish(b - 1, pending)
        pending = nxt
    finish(n_sub - 1, pending)


def _proj_ln_kernel(a_ref, w_ref, x_ref, gb_ref, o_ref, *, alpha, n_sub):
    rs = o_ref.shape[0] // n_sub

    def pre_norm(b):
        rows = slice(b * rs, (b + 1) * rs)
        return alpha * x_ref[rows, :] + _dot(a_ref[rows, :], w_ref[...])

    _residual_ln_blocks(n_sub, rs, pre_norm, gb_ref, [o_ref])


def _proj_ln(a, w, layer, x, gain_bias, alpha, tm):
    m, d = x.shape
    k = a.shape[1]
    return pl.pallas_call(
        functools.partial(_proj_ln_kernel, alpha=alpha, n_sub=max(1, tm // 128)),
        grid=(m // tm,),
        in_specs=[
            pl.BlockSpec((tm, k), lambda i: (i, 0)),
            pl.BlockSpec((None, k, d), lambda i: (layer, 0, 0), pipeline_mode=pl.Buffered(1)),
            pl.BlockSpec((tm, d), lambda i: (i, 0)),
            pl.BlockSpec((2, d), lambda i: (0, 0)),
        ],
        out_specs=pl.BlockSpec((tm, d), lambda i: (i, 0)),
        out_shape=jax.ShapeDtypeStruct((m, d), F32),
        compiler_params=_cparams(1),
        name="proj_ln",
    )(a, w, x, gain_bias)


def _ffn_kernel(x_ref, wg_ref, wu_ref, wd_ref, gb_ref, o_ref, *rest, alpha):
    *ob_ref, xb_ref, acc_ref = rest
    j = pl.program_id(1)

    @pl.when(j == 0)
    def _():
        xb_ref[...] = x_ref[...].astype(BF16)
        acc_ref[...] = jnp.zeros_like(acc_ref)

    xb = xb_ref[...]
    act = _silu(_dot(xb, wg_ref[...])) * _dot(xb, wu_ref[...])
    acc_ref[...] += _dot(act.astype(BF16), wd_ref[...])

    @pl.when(j == pl.num_programs(1) - 1)
    def _():
        y = alpha * x_ref[...] + acc_ref[...]
        out = _layer_norm(y, gb_ref[0:1, :], gb_ref[1:2, :])
        o_ref[...] = out
        for r in ob_ref:
            r[...] = out.astype(BF16)


def _ffn(x, w_gate_up, w_down, layer, gain_bias, alpha, tm, th, emit_bf16):
    m, d = x.shape
    hidden = w_down.shape[1]
    nh = hidden // th
    row_block = pl.BlockSpec((tm, d), lambda i, j: (i, 0))
    outs = pl.pallas_call(
        functools.partial(_ffn_kernel, alpha=alpha),
        grid=(m // tm, nh),
        in_specs=[
            row_block,
            pl.BlockSpec((None, d, th), lambda i, j: (layer, 0, j)),
            pl.BlockSpec((None, d, th), lambda i, j: (layer, 0, nh + j)),
            pl.BlockSpec((None, th, d), lambda i, j: (layer, j, 0)),
            pl.BlockSpec((2, d), lambda i, j: (0, 0)),
        ],
        out_specs=[row_block] * (2 if emit_bf16 else 1),
        out_shape=[jax.ShapeDtypeStruct((m, d), F32)] + [jax.ShapeDtypeStruct((m, d), BF16)] * emit_bf16,
        scratch_shapes=[pltpu.VMEM((tm, d), BF16), pltpu.VMEM((tm, d), F32)],
        compiler_params=_cparams(2),
        name="ffn",
    )(x, w_gate_up, w_gate_up, w_down, gain_bias)
    return outs if emit_bf16 else (outs[0], None)


def _dn_in_kernel(xb_ref, w_ref, cw_ref, o_ref, *scratch, tiles_per_seq, conv, l2norm, n_scaled, n_sub):
    i, j = pl.program_id(0), pl.program_id(1)
    tm = o_ref.shape[0]
    rs = tm // n_sub

    def block_dot(b):
        return _dot(xb_ref[b * rs:(b + 1) * rs, :], w_ref[...])

    if not conv:
        for b in range(n_sub):
            o_ref[b * rs:(b + 1) * rs, :] = block_dot(b).astype(o_ref.dtype)
        return

    acc0_ref, acc1_ref, carry_ref = scratch
    acc = (acc0_ref, acc1_ref)
    ksize = cw_ref.shape[0]

    @pl.when(i % tiles_per_seq == 0)
    def _():
        acc0_ref[0:SUBLANES, :] = jnp.zeros((SUBLANES, acc0_ref.shape[1]), F32)

    @pl.when(i % tiles_per_seq != 0)
    def _():
        acc0_ref[0:SUBLANES, :] = carry_ref[j]

    scale = jnp.where(j < n_scaled, HEAD_DIM ** -0.5, 1.0).astype(F32)

    def epilogue(b):
        src = acc[b % 2]
        for s in range(o_ref.shape[1] // HEAD_DIM):
            lanes = slice(s * HEAD_DIM, (s + 1) * HEAD_DIM)
            t = _silu(_causal_taps(src[:, lanes], [cw_ref[k:k + 1, lanes] for k in range(ksize)]))
            if l2norm:
                t = t * (lax.rsqrt(jnp.sum(t * t, axis=-1, keepdims=True) + RMS_EPS) * scale)
            o_ref[b * rs:(b + 1) * rs, lanes] = t.astype(o_ref.dtype)

    tail = None
    for b in range(n_sub):
        res = block_dot(b)
        acc[b % 2][SUBLANES:SUBLANES + rs, :] = res
        if b > 0:
            epilogue(b - 1)
            acc[b % 2][0:SUBLANES, :] = tail
        tail = res[rs - SUBLANES:, :]
    epilogue(n_sub - 1)
    carry_ref[j] = tail


def _dn_in(xb, w_in, layer, conv_w, col0, width, seq, tm, tn, *, conv, l2norm=False, n_scaled=0, name):
    m, d = xb.shape
    nj = width // tn
    j0 = col0 // tn
    n_sub = max(1, tm // 128)
    blk = pltpu.VMEM((tm // n_sub + SUBLANES, tn), F32)
    scratch = [blk, blk, pltpu.VMEM((nj, SUBLANES, tn), F32)] if conv else []
    cw_j0 = j0 if conv else 0
    return pl.pallas_call(
        functools.partial(_dn_in_kernel, tiles_per_seq=seq // tm, conv=conv, l2norm=l2norm,
                          n_scaled=n_scaled, n_sub=n_sub),
        grid=(m // tm, nj),
        in_specs=[
            pl.BlockSpec((tm, d), lambda i, j: (i, 0)),
            pl.BlockSpec((None, d, tn), lambda i, j: (layer, 0, j0 + j)),
            pl.BlockSpec((conv_w.shape[0], tn), lambda i, j: (0, cw_j0 + j if conv else 0)),
        ],
        out_specs=pl.BlockSpec((tm, tn), lambda i, j: (i, j)),
        out_shape=jax.ShapeDtypeStruct((m, width), BF16),
        scratch_shapes=scratch,
        compiler_params=_cparams(2),
        name=name,
    )(xb, w_in, conv_w)


def _softplus(t):
    return jnp.maximum(t, 0.0) + jnp.log1p(jnp.exp(-jnp.abs(t)))


def _chunk_cumsum(g, axis):
    pos = lax.broadcasted_iota(jnp.int32, g.shape, axis) % CHUNK
    s = 1
    while s < CHUNK:
        g = g + jnp.where(pos >= s, pltpu.roll(g, s, axis=axis), 0.0)
        s *= 2
    return g


def _gates(raw, a_log, dt_bias, slot_axis, token_axis):
    c = lax.broadcasted_iota(jnp.int32, raw.shape, slot_axis) % SUBLANES
    beta = jax.nn.sigmoid(raw)
    g = -jnp.exp(a_log) * _softplus(raw + dt_bias)
    g = _chunk_cumsum(jnp.where(jnp.logical_and(c >= 2, c < 4), g, 0.0), token_axis)
    return jnp.where(c < 2, beta, g)


def _dn_gate_kernel(x_ref, w_ref, wt_ref, al_ref, dt_ref, alt_ref, dtt_ref, gc_ref, gt_ref):
    xb = x_ref[...]
    gc_ref[...] = _gates(_dot(xb, w_ref[...]), al_ref[...], dt_ref[...], 1, 0)
    gt_ref[...] = _gates(_dot_nt(wt_ref[...], xb), alt_ref[...], dtt_ref[...], 0, 1)


def _dn_gates(x, w_slot, a_log_slot, dt_slot, tm):
    m, d = x.shape
    ns = w_slot.shape[1]
    return pl.pallas_call(
        _dn_gate_kernel,
        grid=(m // tm,),
        in_specs=[
            pl.BlockSpec((tm, d), lambda i: (i, 0)),
            pl.BlockSpec((d, ns), lambda i: (0, 0)),
            pl.BlockSpec((ns, d), lambda i: (0, 0)),
            pl.BlockSpec((1, ns), lambda i: (0, 0)),
            pl.BlockSpec((1, ns), lambda i: (0, 0)),
            pl.BlockSpec((ns, 1), lambda i: (0, 0)),
            pl.BlockSpec((ns, 1), lambda i: (0, 0)),
        ],
        out_specs=[pl.BlockSpec((tm, ns), lambda i: (i, 0)),
                   pl.BlockSpec((ns, tm), lambda i: (0, i))],
        out_shape=[jax.ShapeDtypeStruct((m, ns), F32), jax.ShapeDtypeStruct((ns, m), F32)],
        compiler_params=_cparams(1),
        name="dn_gates",
    )(x, w_slot, w_slot.T, a_log_slot[None, :], dt_slot[None, :], a_log_slot[:, None], dt_slot[:, None])


def _block_diag(a, b):
    zero = jnp.zeros_like(a)
    return jnp.concatenate([jnp.concatenate([a, zero], axis=1), jnp.concatenate([zero, b], axis=1)], axis=0)


def _dn_chunk_kernel(q_ref, k_ref, v_ref, z_ref, gc_ref, gt_ref, nw_ref, o_ref,
                     state_ref, qm_ref, on_ref, cd_ref, *, nt, n_kh):
    g = pl.program_id(0)
    kh = (jnp.minimum(g, pl.num_programs(0) - 2) // nt) % n_kh
    n_chunks = q_ref.shape[0] // CHUNK
    slot_a = g % 2
    slot_b = 1 - slot_a

    @pl.when(g == 0)
    def _():
        qm_ref[1] = jnp.zeros(qm_ref.shape[1:], qm_ref.dtype)
        on_ref[1] = jnp.zeros(on_ref.shape[1:], on_ref.dtype)
        cd_ref[1] = jnp.zeros(cd_ref.shape[1:], cd_ref.dtype)

    @pl.when(jnp.maximum(g - 1, 0) % nt == 0)
    def _():
        state_ref[...] = jnp.zeros_like(state_ref)

    row = lax.broadcasted_iota(jnp.int32, (CHUNK, LANES), 0)
    lane = lax.broadcasted_iota(jnp.int32, (CHUNK, LANES), 1)
    col = lane % CHUNK
    causal = col <= row
    strict = col < row
    upper_half = lane >= CHUNK
    lower_half = lane < CHUNK
    eye = col == row
    eye_hi = jnp.where(jnp.logical_and(upper_half, eye), 1.0, 0.0).astype(F32)
    eye_lo = jnp.where(jnp.logical_and(lower_half, eye), 1.0, 0.0).astype(F32)
    norm_w = nw_ref[...]
    state = [state_ref[...]]

    def recurrence_chunk(c):
        rows = slice(c * CHUNK, (c + 1) * CHUNK)
        s_b = state[0].astype(BF16)
        qms = _dot(qm_ref[slot_b, c], _block_diag(s_b[:, :HEAD_DIM], s_b[:, HEAD_DIM:]))
        o_n = on_ref[slot_b, c]
        o_all = qms[:CHUNK] + o_n[:CHUNK]
        state[0] = state[0] * cd_ref[slot_b, c][:1, :] + o_n[CHUNK:] - qms[CHUNK:]
        for e in range(2):
            o = o_all[:, e * HEAD_DIM:(e + 1) * HEAD_DIM]
            o = o * lax.rsqrt(jnp.mean(o * o, axis=-1, keepdims=True) + RMS_EPS) * norm_w
            ze = z_ref[rows, e * HEAD_DIM:(e + 1) * HEAD_DIM].astype(F32)
            o_ref[rows, e * HEAD_DIM:(e + 1) * HEAD_DIM] = (o * _silu(ze)).astype(o_ref.dtype)

    gsel_all = pltpu.roll(gc_ref[...], (LANES - SUBLANES * kh) % LANES, axis=1)
    kf, qf, k_t, qk2, kk2 = {}, {}, {}, {}, {}
    beta_col, g_col, g_last, decay, x_mat, w_mat, sol, exp_g = {}, {}, {}, {}, {}, {}, {}, {}

    def load_chunk(c):
        rows = slice(c * CHUNK, (c + 1) * CHUNK)
        qc, kc = q_ref[rows, :], k_ref[rows, :]
        qf[c] = qc.astype(F32)
        kf[c] = kc.astype(F32)
        k_t[c] = jnp.transpose(kf[c]).astype(BF16)
        qkk = _dot_nt(jnp.concatenate([qc, kc], axis=0), jnp.concatenate([kc, kc], axis=0))
        qk2[c] = qkk[:CHUNK]
        kk2[c] = qkk[CHUNK:]

    def gate_terms(c, e):
        gsel = gsel_all[c * CHUNK:(c + 1) * CHUNK]
        g_row = gt_ref[c][2 + e:3 + e, :]
        beta_col[c, e] = jnp.broadcast_to(gsel[:, e:e + 1], (CHUNK, LANES))
        g_col[c, e] = jnp.broadcast_to(gsel[:, 2 + e:3 + e], (CHUNK, LANES))
        g_last[c, e] = g_row[:, CHUNK - 1:CHUNK]
        decay[c, e] = jnp.exp(jnp.where(causal, g_col[c, e] - g_row, NEG_BIG))
        x_mat[c, e] = jnp.where(strict, -(kk2[c] * beta_col[c, e] * decay[c, e]), 0.0)

    def square(c):
        xa, xb = x_mat[c, 0], x_mat[c, 1]
        sq = _dot(jnp.where(lower_half, xa, xb).astype(BF16), _block_diag(xa.astype(BF16), xb.astype(BF16)))
        w_mat[c, 0] = jnp.where(upper_half, xa + eye_hi, sq[:, :LANES])
        w_mat[c, 1] = jnp.where(upper_half, sq[:, LANES:], xb + eye_lo)

    def advance(c):
        wa, wb = w_mat[c, 0], w_mat[c, 1]
        pw = _dot(jnp.where(lower_half, wa, wb).astype(BF16), _block_diag(wa.astype(BF16), wb.astype(BF16)))
        w_mat[c, 0] = pw[:, :LANES] + jnp.where(upper_half, wa, 0.0)
        w_mat[c, 1] = pw[:, LANES:] + jnp.where(lower_half, wb, 0.0)

    def solve(c, e):
        t_mat = w_mat[c, e][:, CHUNK:] if e == 0 else w_mat[c, e][:, :CHUNK]
        ve = v_ref[c * CHUNK:(c + 1) * CHUNK, e * HEAD_DIM:(e + 1) * HEAD_DIM].astype(F32)
        exp_g[c, e] = jnp.exp(g_col[c, e])
        rhs = jnp.concatenate([kf[c] * (beta_col[c, e] * exp_g[c, e]), ve * beta_col[c, e]], axis=1)
        sol[c, e] = _dot(t_mat.astype(BF16), rhs.astype(BF16))

    def finish(c):
        qm, o_n, cd = [], [], []
        for e in range(2):
            a_qk = (qk2[c] * decay[c, e])[:, :CHUNK].astype(BF16)
            qo = _dot(a_qk, sol[c, e].astype(BF16))
            q_eff = qf[c] * exp_g[c, e] - qo[:, :HEAD_DIM]
            k_tail = jnp.exp(g_last[c, e] - g_col[c, e])[:, :1]
            mn = _dot(k_t[c], (sol[c, e] * k_tail).astype(BF16))
            qm.append(jnp.concatenate([q_eff, mn[:, :HEAD_DIM]], axis=0).astype(BF16))
            o_n.append(jnp.concatenate([qo[:, HEAD_DIM:], mn[:, HEAD_DIM:]], axis=0))
            cd.append(jnp.broadcast_to(jnp.exp(g_last[c, e]), (SUBLANES, LANES)))
        qm_ref[slot_a, c] = jnp.concatenate(qm, axis=1)
        on_ref[slot_a, c] = jnp.concatenate(o_n, axis=1)
        cd_ref[slot_a, c] = jnp.concatenate(cd, axis=1)

    chunks = range(n_chunks)
    items = []
    for g0 in range(0, n_chunks, PREP_GROUP):
        group = range(g0, min(g0 + PREP_GROUP, n_chunks))
        chains = [(c, e) for c in group for e in range(2)]
        items += [functools.partial(load_chunk, c) for c in group]
        items += [functools.partial(gate_terms, c, e) for (c, e) in chains]
        items += [functools.partial(square, c) for c in group]
        items += [functools.partial(advance, c) for _ in range(5) for c in group]
        items += [functools.partial(solve, c, e) for (c, e) in chains]
        items += [functools.partial(finish, c) for c in group]
    for idx, item in enumerate(items):
        for c in chunks:
            if idx == c * len(items) // n_chunks:
                recurrence_chunk(c)
        item()

    state_ref[...] = state[0]


def _dn_chunk(qk, v, z, gate_c, gate_t, norm_w, batch, seq, n_kh, tc):
    m = qk.shape[0]
    nt = seq // tc
    cpt = tc // CHUNK

    n_tiles = batch * n_kh * nt

    def split(t):
        return (t // (n_kh * nt)) * nt + t % nt, (t // nt) % n_kh

    def prep(g):
        return split(jnp.minimum(g, n_tiles - 1))

    def rec(g):
        return split(jnp.maximum(g - 1, 0))

    return pl.pallas_call(
        functools.partial(_dn_chunk_kernel, nt=nt, n_kh=n_kh),
        grid=(n_tiles + 1,),
        in_specs=[
            pl.BlockSpec((tc, HEAD_DIM), lambda g: prep(g)),
            pl.BlockSpec((tc, HEAD_DIM), lambda g: (prep(g)[0], n_kh + prep(g)[1])),
            pl.BlockSpec((tc, 2 * HEAD_DIM), lambda g: prep(g)),
            pl.BlockSpec((tc, 2 * HEAD_DIM), lambda g: rec(g)),
            pl.BlockSpec((tc, gate_c.shape[1]), lambda g: (prep(g)[0], 0)),
            pl.BlockSpec((None, cpt, SUBLANES, LANES), lambda g: (prep(g)[1], prep(g)[0], 0, 0)),
            pl.BlockSpec((1, HEAD_DIM), lambda g: (0, 0)),
        ],
        out_specs=pl.BlockSpec((tc, 2 * HEAD_DIM), lambda g: rec(g)),
        out_shape=jax.ShapeDtypeStruct((m, 2 * n_kh * HEAD_DIM), BF16),
        scratch_shapes=[pltpu.VMEM((HEAD_DIM, 2 * HEAD_DIM), F32),
                        pltpu.VMEM((2, cpt, CHUNK + HEAD_DIM, 2 * HEAD_DIM), BF16),
                        pltpu.VMEM((2, cpt, CHUNK + HEAD_DIM, 2 * HEAD_DIM), F32),
                        pltpu.VMEM((2, cpt, SUBLANES, 2 * LANES), F32)],
        compiler_params=_cparams(1),
        name="dn_chunk",
    )(qk, qk, v, z, gate_c, gate_t, norm_w)


def _tile(n, want):
    if n <= want:
        return n
    t = want - want % LANES
    while n % t:
        t -= LANES
    return t


def _gate_slots(w_beta_a, a_log, dt_bias, n_vh):
    n_kh = n_vh // 2
    d = w_beta_a.shape[0]
    wb = w_beta_a[:, :n_vh].reshape(d, n_kh, 2)
    wa = w_beta_a[:, n_vh:].reshape(d, n_kh, 2)
    w_slot = jnp.concatenate([wb, wa, jnp.zeros((d, n_kh, SUBLANES - 4), w_beta_a.dtype)], axis=2)
    zeros2 = jnp.zeros((n_kh, 2), F32)
    zeros4 = jnp.zeros((n_kh, SUBLANES - 4), F32)
    al = jnp.concatenate([zeros2, a_log.astype(F32).reshape(n_kh, 2), zeros4], axis=1)
    dt = jnp.concatenate([zeros2, dt_bias.astype(F32).reshape(n_kh, 2), zeros4], axis=1)
    pad = (-n_kh * SUBLANES) % LANES
    w_slot = jnp.pad(w_slot.reshape(d, n_kh * SUBLANES), ((0, 0), (0, pad)))
    return w_slot, jnp.pad(al.reshape(-1), (0, pad)), jnp.pad(dt.reshape(-1), (0, pad))


def kernel(x, sc_w_in, sc_conv_w, sc_w_out, dn_w_in, dn_conv_w, dn_a_log, dn_dt_bias, dn_norm_w,
           dn_w_out, ffn_w_gate_up, ffn_w_down, ln_gain, ln_bias):
    batch, seq, d = x.shape
    depth = ln_gain.shape[0]
    alpha = float((2 * depth) ** 0.25)
    m = batch * seq
    n_vh = dn_a_log.shape[1]
    n_kh = n_vh // 2
    key_dim = n_kh * HEAD_DIM
    val_dim = n_vh * HEAD_DIM
    qkv_dim = 2 * key_dim + val_dim
    hidden = ffn_w_down.shape[1]
    assert seq % CHUNK == 0 and d % LANES == 0 and dn_norm_w.shape[1] == HEAD_DIM

    tm_in = _tile(seq, 1024)
    tm_dn = _tile(seq, 2048)
    tm_out = _tile(seq, 512)
    tn = _tile(d, 512)
    th = _tile(hidden, 512)
    tc = _tile(seq, 1024)

    sc_w_in_b, sc_w_out_b = sc_w_in.astype(BF16), sc_w_out.astype(BF16)
    dn_w_in_b, dn_w_out_b = dn_w_in.astype(BF16), dn_w_out.astype(BF16)
    ffn_w_gu_b, ffn_w_down_b = ffn_w_gate_up.astype(BF16), ffn_w_down.astype(BF16)

    h = x.reshape(m, d)
    hb = h
    for i in range(depth):
        l = i // 2
        gb0 = jnp.stack([ln_gain[i, 0], ln_bias[i, 0]])
        gb1 = jnp.stack([ln_gain[i, 1], ln_bias[i, 1]])
        if i % 2 == 0:
            a = _sc_in(hb, sc_w_in_b, l, sc_conv_w[l], seq, tm_in, tn)
            h = _proj_ln(a, sc_w_out_b, l, h, gb0, alpha, tm_out)
        else:
            cw = dn_conv_w[l]
            qk = _dn_in(hb, dn_w_in_b, l, cw, 0, 2 * key_dim, seq, tm_dn, tn, conv=True, l2norm=True,
                        n_scaled=key_dim // tn, name="dn_in_qk")
            v = _dn_in(hb, dn_w_in_b, l, cw, 2 * key_dim, val_dim, seq, tm_dn, tn, conv=True, name="dn_in_v")
            z = _dn_in(hb, dn_w_in_b, l, cw, qkv_dim, val_dim, seq, tm_dn, tn, conv=False, name="dn_in_z")
            w_slot, al_slot, dt_slot = _gate_slots(dn_w_in_b[l][:, qkv_dim + val_dim:], dn_a_log[l],
                                                   dn_dt_bias[l], n_vh)
            gate_c, gate_t = _dn_gates(hb, w_slot, al_slot, dt_slot, tm_dn)
            gate_t = gate_t[:n_kh * SUBLANES].reshape(n_kh, SUBLANES, m // CHUNK, CHUNK)
            gate_t = jnp.transpose(gate_t, (0, 2, 1, 3))
            gate_t = jnp.concatenate([gate_t, gate_t], axis=-1)
            o = _dn_chunk(qk, v, z, gate_c, gate_t, dn_norm_w[l][None, :], batch, seq, n_kh, tc)
            h = _proj_ln(o, dn_w_out_b, l, h, gb0, alpha, tm_out)
        h, hb = _ffn(h, ffn_w_gu_b, ffn_w_down_b, i, gb1, alpha, tm_out, th, emit_bf16=i + 1 < depth)
    return h.reshape(batch, seq, d)
```

```python
import functools

import jax
import jax.numpy as jnp
from jax import lax
from jax.experimental import pallas as pl
from jax.experimental.pallas import tpu as pltpu

HEAD_DIM = 128
CHUNK = 64
SUBLANES = 8
LANES = 128
LN_EPS = 1e-5
RMS_EPS = 1e-6
NEG_BIG = -1e30
PREP_GROUP = 16
VMEM_LIMIT_BYTES = 60 * 1024 * 1024

F32 = jnp.float32
BF16 = jnp.bfloat16


def _cparams(n_axes):
    return pltpu.CompilerParams(
        dimension_semantics=("arbitrary",) * n_axes,
        vmem_limit_bytes=VMEM_LIMIT_BYTES)


def _dot(a, b):
    return jnp.dot(a, b, preferred_element_type=F32)


def _dot_nt(a, b):
    return lax.dot_general(a, b, (((1,), (1,)), ((), ())), preferred_element_type=F32)


def _dot_tn(a, b):
    return lax.dot_general(a, b, (((0,), (0,)), ((), ())), preferred_element_type=F32)


def _layer_norm(y, gain, bias):
    mu = jnp.mean(y, axis=-1, keepdims=True)
    d = y - mu
    var = jnp.mean(d * d, axis=-1, keepdims=True)
    return d * lax.rsqrt(var + LN_EPS) * gain + bias


def _silu(t):
    return t * jax.nn.sigmoid(t)


def _causal_taps(ext, w_rows):
    k = len(w_rows)
    assert k <= SUBLANES
    s1 = pltpu.roll(ext, 1, axis=0)
    out = None
    for d0 in range(0, k, 2):
        term = ext * w_rows[k - 1 - d0]
        if d0 + 1 < k:
            term = term + s1 * w_rows[k - 2 - d0]
        if d0:
            term = pltpu.roll(term, d0, axis=0)
        out = term if out is None else out + term
    return out[SUBLANES:]


def _sc_in_kernel(x_ref, wb_ref, wc_ref, wh_ref, cw_ref, o_ref, *scratch, tiles_per_seq, n_sub):
    i, j = pl.program_id(0), pl.program_id(1)
    if x_ref.dtype == BF16:
        xb_ref = x_ref
        u0_ref, u1_ref, g0_ref, g1_ref, carry_ref = scratch
    else:
        xb_ref, u0_ref, u1_ref, g0_ref, g1_ref, carry_ref = scratch

        @pl.when(j == 0)
        def _():
            xb_ref[...] = x_ref[...].astype(BF16)

    ubuf, gbuf = (u0_ref, u1_ref), (g0_ref, g1_ref)
    tm = o_ref.shape[0]
    rs = tm // n_sub
    ksize = cw_ref.shape[0]

    @pl.when(i % tiles_per_seq == 0)
    def _():
        u0_ref[0:SUBLANES, :] = jnp.zeros((SUBLANES, u0_ref.shape[1]), F32)

    @pl.when(i % tiles_per_seq != 0)
    def _():
        u0_ref[0:SUBLANES, :] = carry_ref[j]

    def epilogue(b):
        for s in range(o_ref.shape[1] // LANES):
            lanes = slice(s * LANES, (s + 1) * LANES)
            y = _causal_taps(ubuf[b % 2][:, lanes], [cw_ref[k:k + 1, lanes] for k in range(ksize)])
            o_ref[b * rs:(b + 1) * rs, lanes] = (gbuf[b % 2][:, lanes] * y).astype(o_ref.dtype)

    tail = None
    for b in range(n_sub):
        xblk = xb_ref[b * rs:(b + 1) * rs, :]
        u = _dot(xblk, wc_ref[...]) * _dot(xblk, wh_ref[...])
        ubuf[b % 2][SUBLANES:SUBLANES + rs, :] = u
        gbuf[b % 2][...] = _dot(xblk, wb_ref[...])
        if b > 0:
            epilogue(b - 1)
            ubuf[b % 2][0:SUBLANES, :] = tail
        tail = u[rs - SUBLANES:, :]
    epilogue(n_sub - 1)
    carry_ref[j] = tail


def _sc_in(x, w_in, layer, conv_w, seq, tm, tn):
    m, d = x.shape
    nd = d // tn
    n_sub = max(1, tm // 128)
    rs = tm // n_sub
    ublk, gblk = pltpu.VMEM((rs + SUBLANES, tn), F32), pltpu.VMEM((rs, tn), F32)
    scratch = [ublk, ublk, gblk, gblk, pltpu.VMEM((nd, SUBLANES, tn), F32)]
    if x.dtype != BF16:
        scratch = [pltpu.VMEM((tm, d), BF16)] + scratch
    return pl.pallas_call(
        functools.partial(_sc_in_kernel, tiles_per_seq=seq // tm, n_sub=n_sub),
        grid=(m // tm, nd),
        in_specs=[
            pl.BlockSpec((tm, d), lambda i, j: (i, 0)),
            pl.BlockSpec((None, d, tn), lambda i, j: (layer, 0, j)),
            pl.BlockSpec((None, d, tn), lambda i, j: (layer, 0, nd + j)),
            pl.BlockSpec((None, d, tn), lambda i, j: (layer, 0, 2 * nd + j)),
            pl.BlockSpec((conv_w.shape[0], tn), lambda i, j: (0, j)),
        ],
        out_specs=pl.BlockSpec((tm, tn), lambda i, j: (i, j)),
        out_shape=jax.ShapeDtypeStruct((m, d), BF16),
        scratch_shapes=scratch,
        compiler_params=_cparams(2),
        name="sc_in",
    )(x, w_in, w_in, w_in, conv_w)


def _residual_ln_blocks(n_sub, rs, pre_norm, gb_ref, out_refs):
    gain, bias = gb_ref[0:1, :], gb_ref[1:2, :]

    def finish(b, y):
        out = _layer_norm(y, gain, bias)
        for r in out_refs:
            r[b * rs:(b + 1) * rs, :] = out.astype(r.dtype)

    pending = pre_norm(0)
    for b in range(1, n_sub):
        nxt = pre_norm(b)
        finish(b - 1, pending)
        pending = nxt
    finish(n_sub - 1, pending)


def _proj_ln_kernel(a_ref, w_ref, x_ref, gb_ref, o_ref, *, alpha, n_sub):
    rs = o_ref.shape[0] // n_sub

    def pre_norm(b):
        rows = slice(b * rs, (b + 1) * rs)
        return alpha * x_ref[rows, :] + _dot(a_ref[rows, :], w_ref[...])

    _residual_ln_blocks(n_sub, rs, pre_norm, gb_ref, [o_ref])


def _proj_ln(a, w, layer, x, gain_bias, alpha, tm):
    m, d = x.shape
    k = a.shape[1]
    return pl.pallas_call(
        functools.partial(_proj_ln_kernel, alpha=alpha, n_sub=max(1, tm // 128)),
        grid=(m // tm,),
        in_specs=[
            pl.BlockSpec((tm, k), lambda i: (i, 0)),
            pl.BlockSpec((None, k, d), lambda i: (layer, 0, 0), pipeline_mode=pl.Buffered(1)),
            pl.BlockSpec((tm, d), lambda i: (i, 0)),
            pl.BlockSpec((2, d), lambda i: (0, 0)),
        ],
        out_specs=pl.BlockSpec((tm, d), lambda i: (i, 0)),
        out_shape=jax.ShapeDtypeStruct((m, d), F32),
        compiler_params=_cparams(1),
        name="proj_ln",
    )(a, w, x, gain_bias)


def _ffn_kernel(x_ref, wg_ref, wu_ref, wd_ref, gb_ref, o_ref, *rest, alpha):
    *ob_ref, xb_ref, acc_ref = rest
    j = pl.program_id(1)

    @pl.when(j == 0)
    def _():
        xb_ref[...] = x_ref[...].astype(BF16)
        acc_ref[...] = jnp.zeros_like(acc_ref)

    xb = xb_ref[...]
    act = _silu(_dot(xb, wg_ref[...])) * _dot(xb, wu_ref[...])
    acc_ref[...] += _dot(act.astype(BF16), wd_ref[...])

    @pl.when(j == pl.num_programs(1) - 1)
    def _():
        y = alpha * x_ref[...] + acc_ref[...]
        out = _layer_norm(y, gb_ref[0:1, :], gb_ref[1:2, :])
        o_ref[...] = out
        for r in ob_ref:
            r[...] = out.astype(BF16)


def _ffn(x, w_gate_up, w_down, layer, gain_bias, alpha, tm, th, emit_bf16):
    m, d = x.shape
    hidden = w_down.shape[1]
    nh = hidden // th
    row_block = pl.BlockSpec((tm, d), lambda i, j: (i, 0))
    outs = pl.pallas_call(
        functools.partial(_ffn_kernel, alpha=alpha),
        grid=(m // tm, nh),
        in_specs=[
            row_block,
            pl.BlockSpec((None, d, th), lambda i, j: (layer, 0, j)),
            pl.BlockSpec((None, d, th), lambda i, j: (layer, 0, nh + j)),
            pl.BlockSpec((None, th, d), lambda i, j: (layer, j, 0)),
            pl.BlockSpec((2, d), lambda i, j: (0, 0)),
        ],
        out_specs=[row_block] * (2 if emit_bf16 else 1),
        out_shape=[jax.ShapeDtypeStruct((m, d), F32)] + [jax.ShapeDtypeStruct((m, d), BF16)] * emit_bf16,
        scratch_shapes=[pltpu.VMEM((tm, d), BF16), pltpu.VMEM((tm, d), F32)],
        compiler_params=_cparams(2),
        name="ffn",
    )(x, w_gate_up, w_gate_up, w_down, gain_bias)
    return outs if emit_bf16 else (outs[0], None)


def _dn_in_kernel(xb_ref, w_ref, cw_ref, o_ref, *scratch, tiles_per_seq, conv, l2norm, n_scaled, n_sub):
    i, j = pl.program_id(0), pl.program_id(1)
    tm = o_ref.shape[0]
    rs = tm // n_sub

    def block_dot(b):
        return _dot(xb_ref[b * rs:(b + 1) * rs, :], w_ref[...])

    if not conv:
        for b in range(n_sub):
            o_ref[b * rs:(b + 1) * rs, :] = block_dot(b).astype(o_ref.dtype)
        return

    acc0_ref, acc1_ref, carry_ref = scratch
    acc = (acc0_ref, acc1_ref)
    ksize = cw_ref.shape[0]

    @pl.when(i % tiles_per_seq == 0)
    def _():
        acc0_ref[0:SUBLANES, :] = jnp.zeros((SUBLANES, acc0_ref.shape[1]), F32)

    @pl.when(i % tiles_per_seq != 0)
    def _():
        acc0_ref[0:SUBLANES, :] = carry_ref[j]

    scale = jnp.where(j < n_scaled, HEAD_DIM ** -0.5, 1.0).astype(F32)

    def epilogue(b):
        src = acc[b % 2]
        for s in range(o_ref.shape[1] // HEAD_DIM):
            lanes = slice(s * HEAD_DIM, (s + 1) * HEAD_DIM)
            t = _silu(_causal_taps(src[:, lanes], [cw_ref[k:k + 1, lanes] for k in range(ksize)]))
            if l2norm:
                t = t * (lax.rsqrt(jnp.sum(t * t, axis=-1, keepdims=True) + RMS_EPS) * scale)
            o_ref[b * rs:(b + 1) * rs, lanes] = t.astype(o_ref.dtype)

    tail = None
    for b in range(n_sub):
        res = block_dot(b)
        acc[b % 2][SUBLANES:SUBLANES + rs, :] = res
        if b > 0:
            epilogue(b - 1)
            acc[b % 2][0:SUBLANES, :] = tail
        tail = res[rs - SUBLANES:, :]
    epilogue(n_sub - 1)
    carry_ref[j] = tail


def _dn_in(xb, w_in, layer, conv_w, col0, width, seq, tm, tn, *, conv, l2norm=False, n_scaled=0, name):
    m, d = xb.shape
    nj = width // tn
    j0 = col0 // tn
    n_sub = max(1, tm // 128)
    blk = pltpu.VMEM((tm // n_sub + SUBLANES, tn), F32)
    scratch = [blk, blk, pltpu.VMEM((nj, SUBLANES, tn), F32)] if conv else []
    cw_j0 = j0 if conv else 0
    return pl.pallas_call(
        functools.partial(_dn_in_kernel, tiles_per_seq=seq // tm, conv=conv, l2norm=l2norm,
                          n_scaled=n_scaled, n_sub=n_sub),
        grid=(m // tm, nj),
        in_specs=[
            pl.BlockSpec((tm, d), lambda i, j: (i, 0)),
            pl.BlockSpec((None, d, tn), lambda i, j: (layer, 0, j0 + j)),
            pl.BlockSpec((conv_w.shape[0], tn), lambda i, j: (0, cw_j0 + j if conv else 0)),
        ],
        out_specs=pl.BlockSpec((tm, tn), lambda i, j: (i, j)),
        out_shape=jax.ShapeDtypeStruct((m, width), BF16),
        scratch_shapes=scratch,
        compiler_params=_cparams(2),
        name=name,
    )(xb, w_in, conv_w)


def _softplus(t):
    return jnp.maximum(t, 0.0) + jnp.log1p(jnp.exp(-jnp.abs(t)))


def _chunk_cumsum(g, axis):
    pos = lax.broadcasted_iota(jnp.int32, g.shape, axis) % CHUNK
    s = 1
    while s < CHUNK:
        g = g + jnp.where(pos >= s, pltpu.roll(g, s, axis=axis), 0.0)
        s *= 2
    return g


def _gates(raw, a_log, dt_bias, slot_axis, token_axis):
    c = lax.broadcasted_iota(jnp.int32, raw.shape, slot_axis) % SUBLANES
    beta = jax.nn.sigmoid(raw)
    g = -jnp.exp(a_log) * _softplus(raw + dt_bias)
    g = _chunk_cumsum(jnp.where(jnp.logical_and(c >= 2, c < 4), g, 0.0), token_axis)
    return jnp.where(c < 2, beta, g)


def _dn_gate_kernel(x_ref, w_ref, wt_ref, al_ref, dt_ref, alt_ref, dtt_ref, gc_ref, gt_ref):
    xb = x_ref[...]
    gc_ref[...] = _gates(_dot(xb, w_ref[...]), al_ref[...], dt_ref[...], 1, 0)
    gt_ref[...] = _gates(_dot_nt(wt_ref[...], xb), alt_ref[...], dtt_ref[...], 0, 1)


def _dn_gates(x, w_slot, a_log_slot, dt_slot, tm):
    m, d = x.shape
    ns = w_slot.shape[1]
    return pl.pallas_call(
        _dn_gate_kernel,
        grid=(m // tm,),
        in_specs=[
            pl.BlockSpec((tm, d), lambda i: (i, 0)),
            pl.BlockSpec((d, ns), lambda i: (0, 0)),
            pl.BlockSpec((ns, d), lambda i: (0, 0)),
            pl.BlockSpec((1, ns), lambda i: (0, 0)),
            pl.BlockSpec((1, ns), lambda i: (0, 0)),
            pl.BlockSpec((ns, 1), lambda i: (0, 0)),
            pl.BlockSpec((ns, 1), lambda i: (0, 0)),
        ],
        out_specs=[pl.BlockSpec((tm, ns), lambda i: (i, 0)),
                   pl.BlockSpec((ns, tm), lambda i: (0, i))],
        out_shape=[jax.ShapeDtypeStruct((m, ns), F32), jax.ShapeDtypeStruct((ns, m), F32)],
        compiler_params=_cparams(1),
        name="dn_gates",
    )(x, w_slot, w_slot.T, a_log_slot[None, :], dt_slot[None, :], a_log_slot[:, None], dt_slot[:, None])


def _block_diag(a, b):
    zero = jnp.zeros_like(a)
    return jnp.concatenate([jnp.concatenate([a, zero], axis=1), jnp.concatenate([zero, b], axis=1)], axis=0)


def _dn_chunk_kernel(q_ref, k_ref, v_ref, z_ref, gc_ref, gt_ref, nw_ref, o_ref,
                     state_ref, qm_ref, on_ref, cd_ref, *, nt, n_kh):
    g = pl.program_id(0)
    kh = (jnp.minimum(g, pl.num_programs(0) - 2) // nt) % n_kh
    n_chunks = q_ref.shape[0] // CHUNK
    slot_a = g % 2
    slot_b = 1 - slot_a

    @pl.when(g == 0)
    def _():
        qm_ref[1] = jnp.zeros(qm_ref.shape[1:], qm_ref.dtype)
        on_ref[1] = jnp.zeros(on_ref.shape[1:], on_ref.dtype)
        cd_ref[1] = jnp.zeros(cd_ref.shape[1:], cd_ref.dtype)

    @pl.when(jnp.maximum(g - 1, 0) % nt == 0)
    def _():
        state_ref[...] = jnp.zeros_like(state_ref)

    row = lax.broadcasted_iota(jnp.int32, (CHUNK, LANES), 0)
    lane = lax.broadcasted_iota(jnp.int32, (CHUNK, LANES), 1)
    col = lane % CHUNK
    causal = col <= row
    strict = col < row
    upper_half = lane >= CHUNK
    lower_half = lane < CHUNK
    eye = col == row
    eye_hi = jnp.where(jnp.logical_and(upper_half, eye), 1.0, 0.0).astype(F32)
    eye_lo = jnp.where(jnp.logical_and(lower_half, eye), 1.0, 0.0).astype(F32)
    lower_half_b = lane.astype(F32).astype(BF16) < CHUNK
    zero_b = jnp.zeros((CHUNK, LANES), BF16)
    norm_w = nw_ref[...]
    state = [state_ref[...]]

    def recurrence_chunk(c):
        rows = slice(c * CHUNK, (c + 1) * CHUNK)
        s_b = state[0].astype(BF16)
        qms = _dot(qm_ref[slot_b, c], _block_diag(s_b[:, :HEAD_DIM], s_b[:, HEAD_DIM:]))
        o_n = on_ref[slot_b, c]
        o_all = qms[:CHUNK] + o_n[:CHUNK]
        state[0] = state[0] * cd_ref[slot_b, c][:1, :] + o_n[CHUNK:] - qms[CHUNK:]
        for e in range(2):
            o = o_all[:, e * HEAD_DIM:(e + 1) * HEAD_DIM]
            o = o * lax.rsqrt(jnp.mean(o * o, axis=-1, keepdims=True) + RMS_EPS) * norm_w
            ze = z_ref[rows, e * HEAD_DIM:(e + 1) * HEAD_DIM].astype(F32)
            o_ref[rows, e * HEAD_DIM:(e + 1) * HEAD_DIM] = (o * _silu(ze)).astype(o_ref.dtype)

    gsel_all = pltpu.roll(gc_ref[...], (LANES - SUBLANES * kh) % LANES, axis=1)
    kf, qf, k_t, qk2, kk2 = {}, {}, {}, {}, {}
    beta_col, g_col, g_last, decay, x_mat, w_mat, sol, exp_g = {}, {}, {}, {}, {}, {}, {}, {}

    def load_chunk(c):
        rows = slice(c * CHUNK, (c + 1) * CHUNK)
        qc, kc = q_ref[rows, :], k_ref[rows, :]
        qf[c] = qc.astype(F32)
        kf[c] = kc.astype(F32)
        k_t[c] = jnp.transpose(kf[c]).astype(BF16)
        qkk = _dot_nt(jnp.concatenate([qc, kc], axis=0), jnp.concatenate([kc, kc], axis=0))
        qk2[c] = qkk[:CHUNK]
        kk2[c] = qkk[CHUNK:]

    def gate_terms(c, e):
        gsel = gsel_all[c * CHUNK:(c + 1) * CHUNK]
        g_row = gt_ref[c][2 + e:3 + e, :]
        beta_col[c, e] = jnp.broadcast_to(gsel[:, e:e + 1], (CHUNK, LANES))
        g_col[c, e] = jnp.broadcast_to(gsel[:, 2 + e:3 + e], (CHUNK, LANES))
        g_last[c, e] = g_row[:, CHUNK - 1:CHUNK]
        decay[c, e] = jnp.exp(jnp.where(causal, g_col[c, e] - g_row, NEG_BIG))
        x_mat[c, e] = jnp.where(strict, -(kk2[c] * beta_col[c, e] * decay[c, e]), 0.0)

    def square(c):
        xa, xb = x_mat[c, 0], x_mat[c, 1]
        sq = _dot(jnp.where(lower_half, xa, xb).astype(BF16), _block_diag(xa.astype(BF16), xb.astype(BF16)))
        w_mat[c, 0] = jnp.where(upper_half, xa + eye_hi, sq[:, :LANES]).astype(BF16)
        w_mat[c, 1] = jnp.where(upper_half, sq[:, LANES:], xb + eye_lo).astype(BF16)

    def advance(c):
        wa, wb = w_mat[c, 0], w_mat[c, 1]
        pw = _dot(jnp.where(lower_half_b, wa, wb), _block_diag(wa, wb))
        w_mat[c, 0] = pw[:, :LANES].astype(BF16) + jnp.where(lower_half_b, zero_b, wa)
        w_mat[c, 1] = pw[:, LANES:].astype(BF16) + jnp.where(lower_half_b, wb, zero_b)

    def solve(c, e):
        t_mat = w_mat[c, e][:, CHUNK:] if e == 0 else w_mat[c, e][:, :CHUNK]
        ve = v_ref[c * CHUNK:(c + 1) * CHUNK, e * HEAD_DIM:(e + 1) * HEAD_DIM].astype(F32)
        exp_g[c, e] = jnp.exp(g_col[c, e])
        rhs = jnp.concatenate([kf[c] * (beta_col[c, e] * exp_g[c, e]), ve * beta_col[c, e]], axis=1)
        sol[c, e] = _dot(t_mat, rhs.astype(BF16))

    def finish(c):
        qm, o_n, cd = [], [], []
        for e in range(2):
            a_qk = (qk2[c] * decay[c, e])[:, :CHUNK].astype(BF16)
            qo = _dot(a_qk, sol[c, e].astype(BF16))
            q_eff = qf[c] * exp_g[c, e] - qo[:, :HEAD_DIM]
            k_tail = jnp.exp(g_last[c, e] - g_col[c, e])[:, :1]
            mn = _dot(k_t[c], (sol[c, e] * k_tail).astype(BF16))
            qm.append(jnp.concatenate([q_eff, mn[:, :HEAD_DIM]], axis=0).astype(BF16))
            o_n.append(jnp.concatenate([qo[:, HEAD_DIM:], mn[:, HEAD_DIM:]], axis=0))
            cd.append(jnp.broadcast_to(jnp.exp(g_last[c, e]), (SUBLANES, LANES)))
        qm_ref[slot_a, c] = jnp.concatenate(qm, axis=1)
        on_ref[slot_a, c] = jnp.concatenate(o_n, axis=1)
        cd_ref[slot_a, c] = jnp.concatenate(cd, axis=1)

    chunks = range(n_chunks)
    items = []
    for g0 in range(0, n_chunks, PREP_GROUP):
        group = range(g0, min(g0 + PREP_GROUP, n_chunks))
        chains = [(c, e) for c in group for e in range(2)]
        items += [functools.partial(load_chunk, c) for c in group]
        items += [functools.partial(gate_terms, c, e) for (c, e) in chains]
        items += [functools.partial(square, c) for c in group]
        items += [functools.partial(advance, c) for _ in range(5) for c in group]
        items += [functools.partial(solve, c, e) for (c, e) in chains]
        items += [functools.partial(finish, c) for c in group]
    for idx, item in enumerate(items):
        for c in chunks:
            if idx == c * len(items) // n_chunks:
                recurrence_chunk(c)
        item()

    state_ref[...] = state[0]


def _dn_chunk(qk, v, z, gate_c, gate_t, norm_w, batch, seq, n_kh, tc):
    m = qk.shape[0]
    nt = seq // tc
    cpt = tc // CHUNK

    n_tiles = batch * n_kh * nt

    def split(t):
        return (t // (n_kh * nt)) * nt + t % nt, (t // nt) % n_kh

    def prep(g):
        return split(jnp.minimum(g, n_tiles - 1))

    def rec(g):
        return split(jnp.maximum(g - 1, 0))

    return pl.pallas_call(
        functools.partial(_dn_chunk_kernel, nt=nt, n_kh=n_kh),
        grid=(n_tiles + 1,),
        in_specs=[
            pl.BlockSpec((tc, HEAD_DIM), lambda g: prep(g)),
            pl.BlockSpec((tc, HEAD_DIM), lambda g: (prep(g)[0], n_kh + prep(g)[1])),
            pl.BlockSpec((tc, 2 * HEAD_DIM), lambda g: prep(g)),
            pl.BlockSpec((tc, 2 * HEAD_DIM), lambda g: rec(g)),
            pl.BlockSpec((tc, gate_c.shape[1]), lambda g: (prep(g)[0], 0)),
            pl.BlockSpec((None, cpt, SUBLANES, LANES), lambda g: (prep(g)[1], prep(g)[0], 0, 0)),
            pl.BlockSpec((1, HEAD_DIM), lambda g: (0, 0)),
        ],
        out_specs=pl.BlockSpec((tc, 2 * HEAD_DIM), lambda g: rec(g)),
        out_shape=jax.ShapeDtypeStruct((m, 2 * n_kh * HEAD_DIM), BF16),
        scratch_shapes=[pltpu.VMEM((HEAD_DIM, 2 * HEAD_DIM), F32),
                        pltpu.VMEM((2, cpt, CHUNK + HEAD_DIM, 2 * HEAD_DIM), BF16),
                        pltpu.VMEM((2, cpt, CHUNK + HEAD_DIM, 2 * HEAD_DIM), F32),
                        pltpu.VMEM((2, cpt, SUBLANES, 2 * LANES), F32)],
        compiler_params=_cparams(1),
        name="dn_chunk",
    )(qk, qk, v, z, gate_c, gate_t, norm_w)


def _tile(n, want):
    if n <= want:
        return n
    t = want - want % LANES
    while n % t:
        t -= LANES
    return t


def _gate_slots(w_beta_a, a_log, dt_bias, n_vh):
    n_kh = n_vh // 2
    d = w_beta_a.shape[0]
    wb = w_beta_a[:, :n_vh].reshape(d, n_kh, 2)
    wa = w_beta_a[:, n_vh:].reshape(d, n_kh, 2)
    w_slot = jnp.concatenate([wb, wa, jnp.zeros((d, n_kh, SUBLANES - 4), w_beta_a.dtype)], axis=2)
    zeros2 = jnp.zeros((n_kh, 2), F32)
    zeros4 = jnp.zeros((n_kh, SUBLANES - 4), F32)
    al = jnp.concatenate([zeros2, a_log.astype(F32).reshape(n_kh, 2), zeros4], axis=1)
    dt = jnp.concatenate([zeros2, dt_bias.astype(F32).reshape(n_kh, 2), zeros4], axis=1)
    pad = (-n_kh * SUBLANES) % LANES
    w_slot = jnp.pad(w_slot.reshape(d, n_kh * SUBLANES), ((0, 0), (0, pad)))
    return w_slot, jnp.pad(al.reshape(-1), (0, pad)), jnp.pad(dt.reshape(-1), (0, pad))


def kernel(x, sc_w_in, sc_conv_w, sc_w_out, dn_w_in, dn_conv_w, dn_a_log, dn_dt_bias, dn_norm_w,
           dn_w_out, ffn_w_gate_up, ffn_w_down, ln_gain, ln_bias):
    batch, seq, d = x.shape
    depth = ln_gain.shape[0]
    alpha = float((2 * depth) ** 0.25)
    m = batch * seq
    n_vh = dn_a_log.shape[1]
    n_kh = n_vh // 2
    key_dim = n_kh * HEAD_DIM
    val_dim = n_vh * HEAD_DIM
    qkv_dim = 2 * key_dim + val_dim
    hidden = ffn_w_down.shape[1]
    assert seq % CHUNK == 0 and d % LANES == 0 and dn_norm_w.shape[1] == HEAD_DIM

    tm_in = _tile(seq, 1024)
    tm_dn = _tile(seq, 2048)
    tm_out = _tile(seq, 512)
    tn = _tile(d, 512)
    th = _tile(hidden, 512)
    tc = _tile(seq, 1024)

    sc_w_in_b, sc_w_out_b = sc_w_in.astype(BF16), sc_w_out.astype(BF16)
    dn_w_in_b, dn_w_out_b = dn_w_in.astype(BF16), dn_w_out.astype(BF16)
    ffn_w_gu_b, ffn_w_down_b = ffn_w_gate_up.astype(BF16), ffn_w_down.astype(BF16)

    h = x.reshape(m, d)
    hb = h
    for i in range(depth):
        l = i // 2
        gb0 = jnp.stack([ln_gain[i, 0], ln_bias[i, 0]])
        gb1 = jnp.stack([ln_gain[i, 1], ln_bias[i, 1]])
        if i % 2 == 0:
            a = _sc_in(hb, sc_w_in_b, l, sc_conv_w[l], seq, tm_in, tn)
            h = _proj_ln(a, sc_w_out_b, l, h, gb0, alpha, tm_out)
        else:
            cw = dn_conv_w[l]
            qk = _dn_in(hb, dn_w_in_b, l, cw, 0, 2 * key_dim, seq, tm_dn, tn, conv=True, l2norm=True,
                        n_scaled=key_dim // tn, name="dn_in_qk")
            v = _dn_in(hb, dn_w_in_b, l, cw, 2 * key_dim, val_dim, seq, tm_dn, tn, conv=True, name="dn_in_v")
            z = _dn_in(hb, dn_w_in_b, l, cw, qkv_dim, val_dim, seq, tm_dn, tn, conv=False, name="dn_in_z")
            w_slot, al_slot, dt_slot = _gate_slots(dn_w_in_b[l][:, qkv_dim + val_dim:], dn_a_log[l],
                                                   dn_dt_bias[l], n_vh)
            gate_c, gate_t = _dn_gates(hb, w_slot, al_slot, dt_slot, tm_dn)
            gate_t = gate_t[:n_kh * SUBLANES].reshape(n_kh, SUBLANES, m // CHUNK, CHUNK)
            gate_t = jnp.transpose(gate_t, (0, 2, 1, 3))
            gate_t = jnp.concatenate([gate_t, gate_t], axis=-1)
            o = _dn_chunk(qk, v, z, gate_c, gate_t, dn_norm_w[l][None, :], batch, seq, n_kh, tc)
            h = _proj_ln(o, dn_w_out_b, l, h, gb0, alpha, tm_out)
        h, hb = _ffn(h, ffn_w_gu_b, ffn_w_down_b, i, gb1, alpha, tm_out, th, emit_bf16=i + 1 < depth)
    return h.reshape(batch, seq, d)
```

```python
import functools

import jax
import jax.numpy as jnp
from jax import lax
from jax.experimental import pallas as pl
from jax.experimental.pallas import tpu as pltpu

HEAD_DIM = 128
CHUNK = 64
SUBLANES = 8
LANES = 128
LN_EPS = 1e-5
RMS_EPS = 1e-6
NEG_BIG = -1e30
PREP_GROUP = 16
VMEM_LIMIT_BYTES = 60 * 1024 * 1024

F32 = jnp.float32
BF16 = jnp.bfloat16


def _cparams(n_axes):
    return pltpu.CompilerParams(
        dimension_semantics=("arbitrary",) * n_axes,
        vmem_limit_bytes=VMEM_LIMIT_BYTES)


def _dot(a, b):
    return jnp.dot(a, b, preferred_element_type=F32)


def _dot_nt(a, b):
    return lax.dot_general(a, b, (((1,), (1,)), ((), ())), preferred_element_type=F32)


def _dot_tn(a, b):
    return lax.dot_general(a, b, (((0,), (0,)), ((), ())), preferred_element_type=F32)


def _layer_norm(y, gain, bias):
    mu = jnp.mean(y, axis=-1, keepdims=True)
    d = y - mu
    var = jnp.mean(d * d, axis=-1, keepdims=True)
    return d * lax.rsqrt(var + LN_EPS) * gain + bias


def _silu(t):
    return t * jax.nn.sigmoid(t)


def _causal_taps(ext, w_rows):
    k = len(w_rows)
    assert k <= SUBLANES
    s1 = pltpu.roll(ext, 1, axis=0)
    out = None
    for d0 in range(0, k, 2):
        term = ext * w_rows[k - 1 - d0]
        if d0 + 1 < k:
            term = term + s1 * w_rows[k - 2 - d0]
        if d0:
            term = pltpu.roll(term, d0, axis=0)
        out = term if out is None else out + term
    return out[SUBLANES:]


def _sc_in_kernel(x_ref, wb_ref, wc_ref, wh_ref, cw_ref, o_ref, *scratch, tiles_per_seq, n_sub):
    i, j = pl.program_id(0), pl.program_id(1)
    if x_ref.dtype == BF16:
        xb_ref = x_ref
        u0_ref, u1_ref, g0_ref, g1_ref, carry_ref = scratch
    else:
        xb_ref, u0_ref, u1_ref, g0_ref, g1_ref, carry_ref = scratch

        @pl.when(j == 0)
        def _():
            xb_ref[...] = x_ref[...].astype(BF16)

    ubuf, gbuf = (u0_ref, u1_ref), (g0_ref, g1_ref)
    tm = o_ref.shape[0]
    rs = tm // n_sub
    ksize = cw_ref.shape[0]

    @pl.when(i % tiles_per_seq == 0)
    def _():
        u0_ref[0:SUBLANES, :] = jnp.zeros((SUBLANES, u0_ref.shape[1]), F32)

    @pl.when(i % tiles_per_seq != 0)
    def _():
        u0_ref[0:SUBLANES, :] = carry_ref[j]

    def epilogue(b):
        for s in range(o_ref.shape[1] // LANES):
            lanes = slice(s * LANES, (s + 1) * LANES)
            y = _causal_taps(ubuf[b % 2][:, lanes], [cw_ref[k:k + 1, lanes] for k in range(ksize)])
            o_ref[b * rs:(b + 1) * rs, lanes] = (gbuf[b % 2][:, lanes] * y).astype(o_ref.dtype)

    tail = None
    for b in range(n_sub):
        xblk = xb_ref[b * rs:(b + 1) * rs, :]
        u = _dot(xblk, wc_ref[...]) * _dot(xblk, wh_ref[...])
        ubuf[b % 2][SUBLANES:SUBLANES + rs, :] = u
        gbuf[b % 2][...] = _dot(xblk, wb_ref[...])
        if b > 0:
            epilogue(b - 1)
            ubuf[b % 2][0:SUBLANES, :] = tail
        tail = u[rs - SUBLANES:, :]
    epilogue(n_sub - 1)
    carry_ref[j] = tail


def _sc_in(x, w_in, layer, conv_w, seq, tm, tn):
    m, d = x.shape
    nd = d // tn
    n_sub = max(1, tm // 128)
    rs = tm // n_sub
    ublk, gblk = pltpu.VMEM((rs + SUBLANES, tn), F32), pltpu.VMEM((rs, tn), F32)
    scratch = [ublk, ublk, gblk, gblk, pltpu.VMEM((nd, SUBLANES, tn), F32)]
    if x.dtype != BF16:
        scratch = [pltpu.VMEM((tm, d), BF16)] + scratch
    return pl.pallas_call(
        functools.partial(_sc_in_kernel, tiles_per_seq=seq // tm, n_sub=n_sub),
        grid=(m // tm, nd),
        in_specs=[
            pl.BlockSpec((tm, d), lambda i, j: (i, 0)),
            pl.BlockSpec((None, d, tn), lambda i, j: (layer, 0, j)),
            pl.BlockSpec((None, d, tn), lambda i, j: (layer, 0, nd + j)),
            pl.BlockSpec((None, d, tn), lambda i, j: (layer, 0, 2 * nd + j)),
            pl.BlockSpec((conv_w.shape[0], tn), lambda i, j: (0, j)),
        ],
        out_specs=pl.BlockSpec((tm, tn), lambda i, j: (i, j)),
        out_shape=jax.ShapeDtypeStruct((m, d), BF16),
        scratch_shapes=scratch,
        compiler_params=_cparams(2),
        name="sc_in",
    )(x, w_in, w_in, w_in, conv_w)


def _residual_ln_blocks(n_sub, rs, pre_norm, gb_ref, out_refs):
    gain, bias = gb_ref[0:1, :], gb_ref[1:2, :]

    def finish(b, y):
        out = _layer_norm(y, gain, bias)
        for r in out_refs:
            r[b * rs:(b + 1) * rs, :] = out.astype(r.dtype)

    pending = pre_norm(0)
    for b in range(1, n_sub):
        nxt = pre_norm(b)
        finish(b - 1, pending)
        pending = nxt
    finish(n_sub - 1, pending)


def _proj_ln_kernel(a_ref, w_ref, x_ref, gb_ref, o_ref, *, alpha, n_sub):
    rs = o_ref.shape[0] // n_sub

    def pre_norm(b):
        rows = slice(b * rs, (b + 1) * rs)
        return alpha * x_ref[rows, :] + _dot(a_ref[rows, :], w_ref[...])

    _residual_ln_blocks(n_sub, rs, pre_norm, gb_ref, [o_ref])


def _proj_ln(a, w, layer, x, gain_bias, alpha, tm):
    m, d = x.shape
    k = a.shape[1]
    return pl.pallas_call(
        functools.partial(_proj_ln_kernel, alpha=alpha, n_sub=max(1, tm // 128)),
        grid=(m // tm,),
        in_specs=[
            pl.BlockSpec((tm, k), lambda i: (i, 0)),
            pl.BlockSpec((None, k, d), lambda i: (layer, 0, 0), pipeline_mode=pl.Buffered(1)),
            pl.BlockSpec((tm, d), lambda i: (i, 0)),
            pl.BlockSpec((2, d), lambda i: (0, 0)),
        ],
        out_specs=pl.BlockSpec((tm, d), lambda i: (i, 0)),
        out_shape=jax.ShapeDtypeStruct((m, d), F32),
        compiler_params=_cparams(1),
        name="proj_ln",
    )(a, w, x, gain_bias)


def _ffn_kernel(x_ref, wg_ref, wu_ref, wd_ref, gb_ref, o_ref, *rest, alpha):
    *ob_ref, xb_ref, acc_ref = rest
    j = pl.program_id(1)

    @pl.when(j == 0)
    def _():
        xb_ref[...] = x_ref[...].astype(BF16)
        acc_ref[...] = jnp.zeros_like(acc_ref)

    xb = xb_ref[...]
    act = _silu(_dot(xb, wg_ref[...])) * _dot(xb, wu_ref[...])
    acc_ref[...] += _dot(act.astype(BF16), wd_ref[...])

    @pl.when(j == pl.num_programs(1) - 1)
    def _():
        y = alpha * x_ref[...] + acc_ref[...]
        out = _layer_norm(y, gb_ref[0:1, :], gb_ref[1:2, :])
        o_ref[...] = out
        for r in ob_ref:
            r[...] = out.astype(BF16)


def _ffn(x, w_gate_up, w_down, layer, gain_bias, alpha, tm, th, emit_bf16):
    m, d = x.shape
    hidden = w_down.shape[1]
    nh = hidden // th
    row_block = pl.BlockSpec((tm, d), lambda i, j: (i, 0))
    outs = pl.pallas_call(
        functools.partial(_ffn_kernel, alpha=alpha),
        grid=(m // tm, nh),
        in_specs=[
            row_block,
            pl.BlockSpec((None, d, th), lambda i, j: (layer, 0, j)),
            pl.BlockSpec((None, d, th), lambda i, j: (layer, 0, nh + j)),
            pl.BlockSpec((None, th, d), lambda i, j: (layer, j, 0)),
            pl.BlockSpec((2, d), lambda i, j: (0, 0)),
        ],
        out_specs=[row_block] * (2 if emit_bf16 else 1),
        out_shape=[jax.ShapeDtypeStruct((m, d), F32)] + [jax.ShapeDtypeStruct((m, d), BF16)] * emit_bf16,
        scratch_shapes=[pltpu.VMEM((tm, d), BF16), pltpu.VMEM((tm, d), F32)],
        compiler_params=_cparams(2),
        name="ffn",
    )(x, w_gate_up, w_gate_up, w_down, gain_bias)
    return outs if emit_bf16 else (outs[0], None)


def _dn_in_kernel(xb_ref, w_ref, cw_ref, o_ref, *scratch, tiles_per_seq, conv, l2norm, n_scaled, n_sub):
    i, j = pl.program_id(0), pl.program_id(1)
    tm = o_ref.shape[0]
    rs = tm // n_sub

    def block_dot(b):
        return _dot(xb_ref[b * rs:(b + 1) * rs, :], w_ref[...])

    if not conv:
        for b in range(n_sub):
            o_ref[b * rs:(b + 1) * rs, :] = block_dot(b).astype(o_ref.dtype)
        return

    acc0_ref, acc1_ref, carry_ref = scratch
    acc = (acc0_ref, acc1_ref)
    ksize = cw_ref.shape[0]

    @pl.when(i % tiles_per_seq == 0)
    def _():
        acc0_ref[0:SUBLANES, :] = jnp.zeros((SUBLANES, acc0_ref.shape[1]), F32)

    @pl.when(i % tiles_per_seq != 0)
    def _():
        acc0_ref[0:SUBLANES, :] = carry_ref[j]

    scale = jnp.where(j < n_scaled, HEAD_DIM ** -0.5, 1.0).astype(F32)

    def epilogue(b):
        src = acc[b % 2]
        for s in range(o_ref.shape[1] // HEAD_DIM):
            lanes = slice(s * HEAD_DIM, (s + 1) * HEAD_DIM)
            t = _silu(_causal_taps(src[:, lanes], [cw_ref[k:k + 1, lanes] for k in range(ksize)]))
            if l2norm:
                t = t * (lax.rsqrt(jnp.sum(t * t, axis=-1, keepdims=True) + RMS_EPS) * scale)
            o_ref[b * rs:(b + 1) * rs, lanes] = t.astype(o_ref.dtype)

    tail = None
    for b in range(n_sub):
        res = block_dot(b)
        acc[b % 2][SUBLANES:SUBLANES + rs, :] = res
        if b > 0:
            epilogue(b - 1)
            acc[b % 2][0:SUBLANES, :] = tail
        tail = res[rs - SUBLANES:, :]
    epilogue(n_sub - 1)
    carry_ref[j] = tail


def _dn_in(xb, w_in, layer, conv_w, col0, width, seq, tm, tn, *, conv, l2norm=False, n_scaled=0, name):
    m, d = xb.shape
    nj = width // tn
    j0 = col0 // tn
    n_sub = max(1, tm // 128)
    blk = pltpu.VMEM((tm // n_sub + SUBLANES, tn), F32)
    scratch = [blk, blk, pltpu.VMEM((nj, SUBLANES, tn), F32)] if conv else []
    cw_j0 = j0 if conv else 0
    return pl.pallas_call(
        functools.partial(_dn_in_kernel, tiles_per_seq=seq // tm, conv=conv, l2norm=l2norm,
                          n_scaled=n_scaled, n_sub=n_sub),
        grid=(m // tm, nj),
        in_specs=[
            pl.BlockSpec((tm, d), lambda i, j: (i, 0)),
            pl.BlockSpec((None, d, tn), lambda i, j: (layer, 0, j0 + j)),
            pl.BlockSpec((conv_w.shape[0], tn), lambda i, j: (0, cw_j0 + j if conv else 0)),
        ],
        out_specs=pl.BlockSpec((tm, tn), lambda i, j: (i, j)),
        out_shape=jax.ShapeDtypeStruct((m, width), BF16),
        scratch_shapes=scratch,
        compiler_params=_cparams(2),
        name=name,
    )(xb, w_in, conv_w)


def _softplus(t):
    return jnp.maximum(t, 0.0) + jnp.log1p(jnp.exp(-jnp.abs(t)))


def _chunk_cumsum(g, axis):
    pos = lax.broadcasted_iota(jnp.int32, g.shape, axis) % CHUNK
    s = 1
    while s < CHUNK:
        g = g + jnp.where(pos >= s, pltpu.roll(g, s, axis=axis), 0.0)
        s *= 2
    return g


def _gates(raw, a_log, dt_bias, slot_axis, token_axis):
    c = lax.broadcasted_iota(jnp.int32, raw.shape, slot_axis) % SUBLANES
    beta = jax.nn.sigmoid(raw)
    g = -jnp.exp(a_log) * _softplus(raw + dt_bias)
    g = _chunk_cumsum(jnp.where(jnp.logical_and(c >= 2, c < 4), g, 0.0), token_axis)
    return jnp.where(c < 2, beta, g)


def _dn_gate_kernel(x_ref, w_ref, wt_ref, al_ref, dt_ref, alt_ref, dtt_ref, gc_ref, gt_ref):
    xb = x_ref[...]
    gc_ref[...] = _gates(_dot(xb, w_ref[...].astype(BF16)), al_ref[...], dt_ref[...], 1, 0)
    gt_ref[...] = _gates(_dot_nt(wt_ref[...].astype(BF16), xb), alt_ref[...], dtt_ref[...], 0, 1)


def _dn_gates(x, w_slot, a_log_slot, dt_slot, tm):
    m, d = x.shape
    ns = w_slot.shape[1]
    return pl.pallas_call(
        _dn_gate_kernel,
        grid=(m // tm,),
        in_specs=[
            pl.BlockSpec((tm, d), lambda i: (i, 0)),
            pl.BlockSpec((d, ns), lambda i: (0, 0)),
            pl.BlockSpec((ns, d), lambda i: (0, 0)),
            pl.BlockSpec((1, ns), lambda i: (0, 0)),
            pl.BlockSpec((1, ns), lambda i: (0, 0)),
            pl.BlockSpec((ns, 1), lambda i: (0, 0)),
            pl.BlockSpec((ns, 1), lambda i: (0, 0)),
        ],
        out_specs=[pl.BlockSpec((tm, ns), lambda i: (i, 0)),
                   pl.BlockSpec((ns, tm), lambda i: (0, i))],
        out_shape=[jax.ShapeDtypeStruct((m, ns), F32), jax.ShapeDtypeStruct((ns, m), F32)],
        compiler_params=_cparams(1),
        name="dn_gates",
    )(x, w_slot, w_slot.T, a_log_slot[None, :], dt_slot[None, :], a_log_slot[:, None], dt_slot[:, None])


def _block_diag(a, b):
    zero = jnp.zeros_like(a)
    return jnp.concatenate([jnp.concatenate([a, zero], axis=1), jnp.concatenate([zero, b], axis=1)], axis=0)


def _dn_chunk_kernel(q_ref, k_ref, v_ref, z_ref, gc_ref, gt_ref, nw_ref, o_ref,
                     state_ref, qm_ref, on_ref, cd_ref, *, nt, n_kh):
    g = pl.program_id(0)
    kh = (jnp.minimum(g, pl.num_programs(0) - 2) // nt) % n_kh
    n_chunks = q_ref.shape[0] // CHUNK
    slot_a = g % 2
    slot_b = 1 - slot_a

    @pl.when(g == 0)
    def _():
        qm_ref[1] = jnp.zeros(qm_ref.shape[1:], qm_ref.dtype)
        on_ref[1] = jnp.zeros(on_ref.shape[1:], on_ref.dtype)
        cd_ref[1] = jnp.zeros(cd_ref.shape[1:], cd_ref.dtype)

    @pl.when(jnp.maximum(g - 1, 0) % nt == 0)
    def _():
        state_ref[...] = jnp.zeros_like(state_ref)

    row = lax.broadcasted_iota(jnp.int32, (CHUNK, LANES), 0)
    lane = lax.broadcasted_iota(jnp.int32, (CHUNK, LANES), 1)
    col = lane % CHUNK
    causal = col <= row
    strict = col < row
    upper_half = lane >= CHUNK
    lower_half = lane < CHUNK
    eye = col == row
    eye_hi = jnp.where(jnp.logical_and(upper_half, eye), 1.0, 0.0).astype(F32)
    eye_lo = jnp.where(jnp.logical_and(lower_half, eye), 1.0, 0.0).astype(F32)
    lower_half_b = lane.astype(F32).astype(BF16) < CHUNK
    zero_b = jnp.zeros((CHUNK, LANES), BF16)
    norm_w = nw_ref[...]
    state = [state_ref[...]]

    def recurrence_chunk(c):
        rows = slice(c * CHUNK, (c + 1) * CHUNK)
        s_b = state[0].astype(BF16)
        qms = _dot(qm_ref[slot_b, c], _block_diag(s_b[:, :HEAD_DIM], s_b[:, HEAD_DIM:]))
        o_n = on_ref[slot_b, c]
        o_all = qms[:CHUNK] + o_n[:CHUNK]
        state[0] = state[0] * cd_ref[slot_b, c][:1, :] + o_n[CHUNK:] - qms[CHUNK:]
        for e in range(2):
            o = o_all[:, e * HEAD_DIM:(e + 1) * HEAD_DIM]
            o = o * lax.rsqrt(jnp.mean(o * o, axis=-1, keepdims=True) + RMS_EPS) * norm_w
            ze = z_ref[rows, e * HEAD_DIM:(e + 1) * HEAD_DIM].astype(F32)
            o_ref[rows, e * HEAD_DIM:(e + 1) * HEAD_DIM] = (o * _silu(ze)).astype(o_ref.dtype)

    gsel_all = pltpu.roll(gc_ref[...], (LANES - SUBLANES * kh) % LANES, axis=1)
    kf, qf, k_t, qk2, kk2 = {}, {}, {}, {}, {}
    beta_col, g_col, g_last, decay, x_mat, w_mat, sol, exp_g = {}, {}, {}, {}, {}, {}, {}, {}

    def load_chunk(c):
        rows = slice(c * CHUNK, (c + 1) * CHUNK)
        qc, kc = q_ref[rows, :], k_ref[rows, :]
        qf[c] = qc.astype(F32)
        kf[c] = kc.astype(F32)
        k_t[c] = jnp.transpose(kf[c]).astype(BF16)
        qkk = _dot_nt(jnp.concatenate([qc, kc], axis=0), jnp.concatenate([kc, kc], axis=0))
        qk2[c] = qkk[:CHUNK]
        kk2[c] = qkk[CHUNK:]

    def gate_terms(c, e):
        gsel = gsel_all[c * CHUNK:(c + 1) * CHUNK]
        g_tok = gt_ref[2 + e:3 + e, c * CHUNK:(c + 1) * CHUNK]
        g_row = jnp.concatenate([g_tok, g_tok], axis=1)
        beta_col[c, e] = jnp.broadcast_to(gsel[:, e:e + 1], (CHUNK, LANES))
        g_col[c, e] = jnp.broadcast_to(gsel[:, 2 + e:3 + e], (CHUNK, LANES))
        g_last[c, e] = g_row[:, CHUNK - 1:CHUNK]
        decay[c, e] = jnp.exp(jnp.where(causal, g_col[c, e] - g_row, NEG_BIG))
        x_mat[c, e] = jnp.where(strict, -(kk2[c] * beta_col[c, e] * decay[c, e]), 0.0)

    def square(c):
        xa, xb = x_mat[c, 0], x_mat[c, 1]
        sq = _dot(jnp.where(lower_half, xa, xb).astype(BF16), _block_diag(xa.astype(BF16), xb.astype(BF16)))
        w_mat[c, 0] = jnp.where(upper_half, xa + eye_hi, sq[:, :LANES]).astype(BF16)
        w_mat[c, 1] = jnp.where(upper_half, sq[:, LANES:], xb + eye_lo).astype(BF16)

    def advance(c):
        wa, wb = w_mat[c, 0], w_mat[c, 1]
        pw = _dot(jnp.where(lower_half_b, wa, wb), _block_diag(wa, wb))
        w_mat[c, 0] = pw[:, :LANES].astype(BF16) + jnp.where(lower_half_b, zero_b, wa)
        w_mat[c, 1] = pw[:, LANES:].astype(BF16) + jnp.where(lower_half_b, wb, zero_b)

    def solve(c, e):
        t_mat = w_mat[c, e][:, CHUNK:] if e == 0 else w_mat[c, e][:, :CHUNK]
        ve = v_ref[c * CHUNK:(c + 1) * CHUNK, e * HEAD_DIM:(e + 1) * HEAD_DIM].astype(F32)
        exp_g[c, e] = jnp.exp(g_col[c, e])
        rhs = jnp.concatenate([kf[c] * (beta_col[c, e] * exp_g[c, e]), ve * beta_col[c, e]], axis=1)
        sol[c, e] = _dot(t_mat, rhs.astype(BF16))

    def finish(c):
        qm, o_n, cd = [], [], []
        for e in range(2):
            a_qk = (qk2[c] * decay[c, e])[:, :CHUNK].astype(BF16)
            qo = _dot(a_qk, sol[c, e].astype(BF16))
            q_eff = qf[c] * exp_g[c, e] - qo[:, :HEAD_DIM]
            k_tail = jnp.exp(g_last[c, e] - g_col[c, e])[:, :1]
            mn = _dot(k_t[c], (sol[c, e] * k_tail).astype(BF16))
            qm.append(jnp.concatenate([q_eff, mn[:, :HEAD_DIM]], axis=0).astype(BF16))
            o_n.append(jnp.concatenate([qo[:, HEAD_DIM:], mn[:, HEAD_DIM:]], axis=0))
            cd.append(jnp.broadcast_to(jnp.exp(g_last[c, e]), (SUBLANES, LANES)))
        qm_ref[slot_a, c] = jnp.concatenate(qm, axis=1)
        on_ref[slot_a, c] = jnp.concatenate(o_n, axis=1)
        cd_ref[slot_a, c] = jnp.concatenate(cd, axis=1)

    chunks = range(n_chunks)
    items = []
    for g0 in range(0, n_chunks, PREP_GROUP):
        group = range(g0, min(g0 + PREP_GROUP, n_chunks))
        chains = [(c, e) for c in group for e in range(2)]
        items += [functools.partial(load_chunk, c) for c in group]
        items += [functools.partial(gate_terms, c, e) for (c, e) in chains]
        items += [functools.partial(square, c) for c in group]
        items += [functools.partial(advance, c) for _ in range(5) for c in group]
        items += [functools.partial(solve, c, e) for (c, e) in chains]
        items += [functools.partial(finish, c) for c in group]
    for idx, item in enumerate(items):
        for c in chunks:
            if idx == c * len(items) // n_chunks:
                recurrence_chunk(c)
        item()

    state_ref[...] = state[0]


def _dn_chunk(qk, v, z, gate_c, gate_t, norm_w, batch, seq, n_kh, tc):
    m = qk.shape[0]
    nt = seq // tc
    cpt = tc // CHUNK

    n_tiles = batch * n_kh * nt

    def split(t):
        return (t // (n_kh * nt)) * nt + t % nt, (t // nt) % n_kh

    def prep(g):
        return split(jnp.minimum(g, n_tiles - 1))

    def rec(g):
        return split(jnp.maximum(g - 1, 0))

    return pl.pallas_call(
        functools.partial(_dn_chunk_kernel, nt=nt, n_kh=n_kh),
        grid=(n_tiles + 1,),
        in_specs=[
            pl.BlockSpec((tc, HEAD_DIM), lambda g: prep(g)),
            pl.BlockSpec((tc, HEAD_DIM), lambda g: (prep(g)[0], n_kh + prep(g)[1])),
            pl.BlockSpec((tc, 2 * HEAD_DIM), lambda g: prep(g)),
            pl.BlockSpec((tc, 2 * HEAD_DIM), lambda g: rec(g)),
            pl.BlockSpec((tc, gate_c.shape[1]), lambda g: (prep(g)[0], 0)),
            pl.BlockSpec((None, SUBLANES, tc), lambda g: (prep(g)[1], 0, prep(g)[0])),
            pl.BlockSpec((1, HEAD_DIM), lambda g: (0, 0)),
        ],
        out_specs=pl.BlockSpec((tc, 2 * HEAD_DIM), lambda g: rec(g)),
        out_shape=jax.ShapeDtypeStruct((m, 2 * n_kh * HEAD_DIM), BF16),
        scratch_shapes=[pltpu.VMEM((HEAD_DIM, 2 * HEAD_DIM), F32),
                        pltpu.VMEM((2, cpt, CHUNK + HEAD_DIM, 2 * HEAD_DIM), BF16),
                        pltpu.VMEM((2, cpt, CHUNK + HEAD_DIM, 2 * HEAD_DIM), F32),
                        pltpu.VMEM((2, cpt, SUBLANES, 2 * LANES), F32)],
        compiler_params=_cparams(1),
        name="dn_chunk",
    )(qk, qk, v, z, gate_c, gate_t, norm_w)


def _tile(n, want):
    if n <= want:
        return n
    t = want - want % LANES
    while n % t:
        t -= LANES
    return t


def _gate_slots(w_beta_a, a_log, dt_bias, n_vh):
    n_kh = n_vh // 2
    d = w_beta_a.shape[0]
    wb = w_beta_a[:, :n_vh].reshape(d, n_kh, 2)
    wa = w_beta_a[:, n_vh:].reshape(d, n_kh, 2)
    w_slot = jnp.concatenate([wb, wa, jnp.zeros((d, n_kh, SUBLANES - 4), w_beta_a.dtype)], axis=2)
    zeros2 = jnp.zeros((n_kh, 2), F32)
    zeros4 = jnp.zeros((n_kh, SUBLANES - 4), F32)
    al = jnp.concatenate([zeros2, a_log.astype(F32).reshape(n_kh, 2), zeros4], axis=1)
    dt = jnp.concatenate([zeros2, dt_bias.astype(F32).reshape(n_kh, 2), zeros4], axis=1)
    pad = (-n_kh * SUBLANES) % LANES
    w_slot = jnp.pad(w_slot.reshape(d, n_kh * SUBLANES), ((0, 0), (0, pad)))
    return w_slot, jnp.pad(al.reshape(-1), (0, pad)), jnp.pad(dt.reshape(-1), (0, pad))


def kernel(x, sc_w_in, sc_conv_w, sc_w_out, dn_w_in, dn_conv_w, dn_a_log, dn_dt_bias, dn_norm_w,
           dn_w_out, ffn_w_gate_up, ffn_w_down, ln_gain, ln_bias):
    batch, seq, d = x.shape
    depth = ln_gain.shape[0]
    alpha = float((2 * depth) ** 0.25)
    m = batch * seq
    n_vh = dn_a_log.shape[1]
    n_kh = n_vh // 2
    key_dim = n_kh * HEAD_DIM
    val_dim = n_vh * HEAD_DIM
    qkv_dim = 2 * key_dim + val_dim
    hidden = ffn_w_down.shape[1]
    assert seq % CHUNK == 0 and d % LANES == 0 and dn_norm_w.shape[1] == HEAD_DIM

    tm_in = _tile(seq, 1024)
    tm_dn = _tile(seq, 2048)
    tm_out = _tile(seq, 512)
    tn = _tile(d, 512)
    th = _tile(hidden, 512)
    tc = _tile(seq, 1024)

    sc_w_in_b, sc_w_out_b = sc_w_in.astype(BF16), sc_w_out.astype(BF16)
    dn_w_in_b, dn_w_out_b = dn_w_in.astype(BF16), dn_w_out.astype(BF16)
    ffn_w_gu_b, ffn_w_down_b = ffn_w_gate_up.astype(BF16), ffn_w_down.astype(BF16)

    h = x.reshape(m, d)
    hb = h
    for i in range(depth):
        l = i // 2
        gb0 = jnp.stack([ln_gain[i, 0], ln_bias[i, 0]])
        gb1 = jnp.stack([ln_gain[i, 1], ln_bias[i, 1]])
        if i % 2 == 0:
            a = _sc_in(hb, sc_w_in_b, l, sc_conv_w[l], seq, tm_in, tn)
            h = _proj_ln(a, sc_w_out_b, l, h, gb0, alpha, tm_in)
        else:
            cw = dn_conv_w[l]
            qk = _dn_in(hb, dn_w_in_b, l, cw, 0, 2 * key_dim, seq, tm_dn, tn, conv=True, l2norm=True,
                        n_scaled=key_dim // tn, name="dn_in_qk")
            v = _dn_in(hb, dn_w_in_b, l, cw, 2 * key_dim, val_dim, seq, tm_dn, tn, conv=True, name="dn_in_v")
            z = _dn_in(hb, dn_w_in_b, l, cw, qkv_dim, val_dim, seq, tm_dn, tn, conv=False, name="dn_in_z")
            w_slot, al_slot, dt_slot = _gate_slots(dn_w_in[l][:, qkv_dim + val_dim:], dn_a_log[l],
                                                   dn_dt_bias[l], n_vh)
            gate_c, gate_t = _dn_gates(hb, w_slot, al_slot, dt_slot, tm_dn)
            gate_t = gate_t[:n_kh * SUBLANES].reshape(n_kh, SUBLANES, m)
            o = _dn_chunk(qk, v, z, gate_c, gate_t, dn_norm_w[l][None, :], batch, seq, n_kh, tc)
            h = _proj_ln(o, dn_w_out_b, l, h, gb0, alpha, tm_out)
        h, hb = _ffn(h, ffn_w_gu_b, ffn_w_down_b, i, gb1, alpha, tm_out, th, emit_bf16=i + 1 < depth)
    return h.reshape(batch, seq, d)
```

```python
import functools

import jax
import jax.numpy as jnp
from jax import lax
from jax.experimental import pallas as pl
from jax.experimental.pallas import tpu as pltpu

HEAD_DIM = 128
CHUNK = 64
SUBLANES = 8
LANES = 128
LN_EPS = 1e-5
RMS_EPS = 1e-6
NEG_BIG = -1e30
PREP_GROUP = 16
VMEM_LIMIT_BYTES = 60 * 1024 * 1024

F32 = jnp.float32
BF16 = jnp.bfloat16


def _cparams(n_axes):
    return pltpu.CompilerParams(
        dimension_semantics=("arbitrary",) * n_axes,
        vmem_limit_bytes=VMEM_LIMIT_BYTES)


def _dot(a, b):
    return jnp.dot(a, b, preferred_element_type=F32)


def _dot_nt(a, b):
    return lax.dot_general(a, b, (((1,), (1,)), ((), ())), preferred_element_type=F32)


def _dot_tn(a, b):
    return lax.dot_general(a, b, (((0,), (0,)), ((), ())), preferred_element_type=F32)


def _layer_norm(y, gain, bias):
    mu = jnp.mean(y, axis=-1, keepdims=True)
    d = y - mu
    var = jnp.mean(d * d, axis=-1, keepdims=True)
    return d * lax.rsqrt(var + LN_EPS) * gain + bias


def _silu(t):
    return t * jax.nn.sigmoid(t)


def _causal_taps(ext, w_rows):
    k = len(w_rows)
    assert k <= SUBLANES
    s1 = pltpu.roll(ext, 1, axis=0)
    out = None
    for d0 in range(0, k, 2):
        term = ext * w_rows[k - 1 - d0]
        if d0 + 1 < k:
            term = term + s1 * w_rows[k - 2 - d0]
        if d0:
            term = pltpu.roll(term, d0, axis=0)
        out = term if out is None else out + term
    return out[SUBLANES:]


def _sc_in_kernel(x_ref, wb_ref, wc_ref, wh_ref, cw_ref, o_ref, *scratch, tiles_per_seq, n_sub):
    i, j = pl.program_id(0), pl.program_id(1)
    if x_ref.dtype == BF16:
        xb_ref = x_ref
        u0_ref, u1_ref, g0_ref, g1_ref, carry_ref = scratch
    else:
        xb_ref, u0_ref, u1_ref, g0_ref, g1_ref, carry_ref = scratch

        @pl.when(j == 0)
        def _():
            xb_ref[...] = x_ref[...].astype(BF16)

    ubuf, gbuf = (u0_ref, u1_ref), (g0_ref, g1_ref)
    tm = o_ref.shape[0]
    rs = tm // n_sub
    ksize = cw_ref.shape[0]

    @pl.when(i % tiles_per_seq == 0)
    def _():
        u0_ref[0:SUBLANES, :] = jnp.zeros((SUBLANES, u0_ref.shape[1]), F32)

    @pl.when(i % tiles_per_seq != 0)
    def _():
        u0_ref[0:SUBLANES, :] = carry_ref[j]

    def epilogue(b):
        for s in range(o_ref.shape[1] // LANES):
            lanes = slice(s * LANES, (s + 1) * LANES)
            y = _causal_taps(ubuf[b % 2][:, lanes], [cw_ref[k:k + 1, lanes] for k in range(ksize)])
            o_ref[b * rs:(b + 1) * rs, lanes] = (gbuf[b % 2][:, lanes] * y).astype(o_ref.dtype)

    tail = None
    for b in range(n_sub):
        xblk = xb_ref[b * rs:(b + 1) * rs, :]
        u = _dot(xblk, wc_ref[...]) * _dot(xblk, wh_ref[...])
        ubuf[b % 2][SUBLANES:SUBLANES + rs, :] = u
        gbuf[b % 2][...] = _dot(xblk, wb_ref[...])
        if b > 0:
            epilogue(b - 1)
            ubuf[b % 2][0:SUBLANES, :] = tail
        tail = u[rs - SUBLANES:, :]
    epilogue(n_sub - 1)
    carry_ref[j] = tail


def _sc_in(x, w_in, layer, conv_w, seq, tm, tn):
    m, d = x.shape
    nd = d // tn
    n_sub = max(1, tm // 128)
    rs = tm // n_sub
    ublk, gblk = pltpu.VMEM((rs + SUBLANES, tn), F32), pltpu.VMEM((rs, tn), F32)
    scratch = [ublk, ublk, gblk, gblk, pltpu.VMEM((nd, SUBLANES, tn), F32)]
    if x.dtype != BF16:
        scratch = [pltpu.VMEM((tm, d), BF16)] + scratch
    return pl.pallas_call(
        functools.partial(_sc_in_kernel, tiles_per_seq=seq // tm, n_sub=n_sub),
        grid=(m // tm, nd),
        in_specs=[
            pl.BlockSpec((tm, d), lambda i, j: (i, 0)),
            pl.BlockSpec((None, d, tn), lambda i, j: (layer, 0, j)),
            pl.BlockSpec((None, d, tn), lambda i, j: (layer, 0, nd + j)),
            pl.BlockSpec((None, d, tn), lambda i, j: (layer, 0, 2 * nd + j)),
            pl.BlockSpec((conv_w.shape[0], tn), lambda i, j: (0, j)),
        ],
        out_specs=pl.BlockSpec((tm, tn), lambda i, j: (i, j)),
        out_shape=jax.ShapeDtypeStruct((m, d), BF16),
        scratch_shapes=scratch,
        compiler_params=_cparams(2),
        name="sc_in",
    )(x, w_in, w_in, w_in, conv_w)


def _residual_ln_blocks(n_sub, rs, pre_norm, gb_ref, out_refs):
    gain, bias = gb_ref[0:1, :], gb_ref[1:2, :]

    def finish(b, y):
        out = _layer_norm(y, gain, bias)
        for r in out_refs:
            r[b * rs:(b + 1) * rs, :] = out.astype(r.dtype)

    pending = pre_norm(0)
    for b in range(1, n_sub):
        nxt = pre_norm(b)
        finish(b - 1, pending)
        pending = nxt
    finish(n_sub - 1, pending)


def _proj_ln_kernel(a_ref, w_ref, x_ref, gb_ref, o_ref, *, alpha, n_sub):
    rs = o_ref.shape[0] // n_sub

    def pre_norm(b):
        rows = slice(b * rs, (b + 1) * rs)
        return alpha * x_ref[rows, :] + _dot(a_ref[rows, :], w_ref[...])

    _residual_ln_blocks(n_sub, rs, pre_norm, gb_ref, [o_ref])


def _proj_ln(a, w, layer, x, gain_bias, alpha, tm):
    m, d = x.shape
    k = a.shape[1]
    return pl.pallas_call(
        functools.partial(_proj_ln_kernel, alpha=alpha, n_sub=max(1, tm // 128)),
        grid=(m // tm,),
        in_specs=[
            pl.BlockSpec((tm, k), lambda i: (i, 0)),
            pl.BlockSpec((None, k, d), lambda i: (layer, 0, 0), pipeline_mode=pl.Buffered(1)),
            pl.BlockSpec((tm, d), lambda i: (i, 0)),
            pl.BlockSpec((2, d), lambda i: (0, 0)),
        ],
        out_specs=pl.BlockSpec((tm, d), lambda i: (i, 0)),
        out_shape=jax.ShapeDtypeStruct((m, d), F32),
        compiler_params=_cparams(1),
        name="proj_ln",
    )(a, w, x, gain_bias)


def _ffn_kernel(x_ref, wg_ref, wu_ref, wd_ref, gb_ref, o_ref, *rest, alpha):
    *ob_ref, xb_ref, acc_ref = rest
    j = pl.program_id(1)

    @pl.when(j == 0)
    def _():
        xb_ref[...] = x_ref[...].astype(BF16)
        acc_ref[...] = jnp.zeros_like(acc_ref)

    xb = xb_ref[...]
    act = _silu(_dot(xb, wg_ref[...])) * _dot(xb, wu_ref[...])
    acc_ref[...] += _dot(act.astype(BF16), wd_ref[...])

    @pl.when(j == pl.num_programs(1) - 1)
    def _():
        y = alpha * x_ref[...] + acc_ref[...]
        out = _layer_norm(y, gb_ref[0:1, :], gb_ref[1:2, :])
        o_ref[...] = out
        for r in ob_ref:
            r[...] = out.astype(BF16)


def _ffn(x, w_gate_up, w_down, layer, gain_bias, alpha, tm, th, emit_bf16):
    m, d = x.shape
    hidden = w_down.shape[1]
    nh = hidden // th
    row_block = pl.BlockSpec((tm, d), lambda i, j: (i, 0))
    outs = pl.pallas_call(
        functools.partial(_ffn_kernel, alpha=alpha),
        grid=(m // tm, nh),
        in_specs=[
            row_block,
            pl.BlockSpec((None, d, th), lambda i, j: (layer, 0, j)),
            pl.BlockSpec((None, d, th), lambda i, j: (layer, 0, nh + j)),
            pl.BlockSpec((None, th, d), lambda i, j: (layer, j, 0)),
            pl.BlockSpec((2, d), lambda i, j: (0, 0)),
        ],
        out_specs=[row_block] * (2 if emit_bf16 else 1),
        out_shape=[jax.ShapeDtypeStruct((m, d), F32)] + [jax.ShapeDtypeStruct((m, d), BF16)] * emit_bf16,
        scratch_shapes=[pltpu.VMEM((tm, d), BF16), pltpu.VMEM((tm, d), F32)],
        compiler_params=_cparams(2),
        name="ffn",
    )(x, w_gate_up, w_gate_up, w_down, gain_bias)
    return outs if emit_bf16 else (outs[0], None)


def _dn_in_kernel(xb_ref, w_ref, cw_ref, o_ref, *scratch, tiles_per_seq, conv, l2norm, n_scaled, n_sub):
    i, j = pl.program_id(0), pl.program_id(1)
    tm = o_ref.shape[0]
    rs = tm // n_sub

    def block_dot(b):
        return _dot(xb_ref[b * rs:(b + 1) * rs, :], w_ref[...])

    if not conv:
        for b in range(n_sub):
            o_ref[b * rs:(b + 1) * rs, :] = block_dot(b).astype(o_ref.dtype)
        return

    acc0_ref, acc1_ref, carry_ref = scratch
    acc = (acc0_ref, acc1_ref)
    ksize = cw_ref.shape[0]

    @pl.when(i % tiles_per_seq == 0)
    def _():
        acc0_ref[0:SUBLANES, :] = jnp.zeros((SUBLANES, acc0_ref.shape[1]), F32)

    @pl.when(i % tiles_per_seq != 0)
    def _():
        acc0_ref[0:SUBLANES, :] = carry_ref[j]

    scale = jnp.where(j < n_scaled, HEAD_DIM ** -0.5, 1.0).astype(F32)

    def epilogue(b):
        src = acc[b % 2]
        for s in range(o_ref.shape[1] // HEAD_DIM):
            lanes = slice(s * HEAD_DIM, (s + 1) * HEAD_DIM)
            t = _silu(_causal_taps(src[:, lanes], [cw_ref[k:k + 1, lanes] for k in range(ksize)]))
            if l2norm:
                t = t * (lax.rsqrt(jnp.sum(t * t, axis=-1, keepdims=True) + RMS_EPS) * scale)
            o_ref[b * rs:(b + 1) * rs, lanes] = t.astype(o_ref.dtype)

    tail = None
    for b in range(n_sub):
        res = block_dot(b)
        acc[b % 2][SUBLANES:SUBLANES + rs, :] = res
        if b > 0:
            epilogue(b - 1)
            acc[b % 2][0:SUBLANES, :] = tail
        tail = res[rs - SUBLANES:, :]
    epilogue(n_sub - 1)
    carry_ref[j] = tail


def _dn_in(xb, w_in, layer, conv_w, col0, width, seq, tm, tn, *, conv, l2norm=False, n_scaled=0, name):
    m, d = xb.shape
    nj = width // tn
    j0 = col0 // tn
    n_sub = max(1, tm // 128)
    blk = pltpu.VMEM((tm // n_sub + SUBLANES, tn), F32)
    scratch = [blk, blk, pltpu.VMEM((nj, SUBLANES, tn), F32)] if conv else []
    cw_j0 = j0 if conv else 0
    return pl.pallas_call(
        functools.partial(_dn_in_kernel, tiles_per_seq=seq // tm, conv=conv, l2norm=l2norm,
                          n_scaled=n_scaled, n_sub=n_sub),
        grid=(m // tm, nj),
        in_specs=[
            pl.BlockSpec((tm, d), lambda i, j: (i, 0)),
            pl.BlockSpec((None, d, tn), lambda i, j: (layer, 0, j0 + j)),
            pl.BlockSpec((conv_w.shape[0], tn), lambda i, j: (0, cw_j0 + j if conv else 0)),
        ],
        out_specs=pl.BlockSpec((tm, tn), lambda i, j: (i, j)),
        out_shape=jax.ShapeDtypeStruct((m, width), BF16),
        scratch_shapes=scratch,
        compiler_params=_cparams(2),
        name=name,
    )(xb, w_in, conv_w)


def _softplus(t):
    return jnp.maximum(t, 0.0) + jnp.log1p(jnp.exp(-jnp.abs(t)))


def _chunk_cumsum(g, axis):
    pos = lax.broadcasted_iota(jnp.int32, g.shape, axis) % CHUNK
    s = 1
    while s < CHUNK:
        g = g + jnp.where(pos >= s, pltpu.roll(g, s, axis=axis), 0.0)
        s *= 2
    return g


def _gates(raw, a_log, dt_bias, slot_axis, token_axis):
    c = lax.broadcasted_iota(jnp.int32, raw.shape, slot_axis) % SUBLANES
    beta = jax.nn.sigmoid(raw)
    g = -jnp.exp(a_log) * _softplus(raw + dt_bias)
    g = _chunk_cumsum(jnp.where(jnp.logical_and(c >= 2, c < 4), g, 0.0), token_axis)
    return jnp.where(c < 2, beta, g)


def _dn_gate_kernel(x_ref, w_ref, wt_ref, al_ref, dt_ref, alt_ref, dtt_ref, gc_ref, gt_ref):
    xb = x_ref[...]
    gc_ref[...] = _gates(_dot(xb, w_ref[...]), al_ref[...], dt_ref[...], 1, 0)
    gt_ref[...] = _gates(_dot_nt(wt_ref[...], xb), alt_ref[...], dtt_ref[...], 0, 1)


def _dn_gates(x, w_slot, a_log_slot, dt_slot, tm):
    m, d = x.shape
    ns = w_slot.shape[1]
    return pl.pallas_call(
        _dn_gate_kernel,
        grid=(m // tm,),
        in_specs=[
            pl.BlockSpec((tm, d), lambda i: (i, 0)),
            pl.BlockSpec((d, ns), lambda i: (0, 0)),
            pl.BlockSpec((ns, d), lambda i: (0, 0)),
            pl.BlockSpec((1, ns), lambda i: (0, 0)),
            pl.BlockSpec((1, ns), lambda i: (0, 0)),
            pl.BlockSpec((ns, 1), lambda i: (0, 0)),
            pl.BlockSpec((ns, 1), lambda i: (0, 0)),
        ],
        out_specs=[pl.BlockSpec((tm, ns), lambda i: (i, 0)),
                   pl.BlockSpec((ns, tm), lambda i: (0, i))],
        out_shape=[jax.ShapeDtypeStruct((m, ns), F32), jax.ShapeDtypeStruct((ns, m), F32)],
        compiler_params=_cparams(1),
        name="dn_gates",
    )(x, w_slot, w_slot.T, a_log_slot[None, :], dt_slot[None, :], a_log_slot[:, None], dt_slot[:, None])


def _block_diag(a, b):
    zero = jnp.zeros_like(a)
    return jnp.concatenate([jnp.concatenate([a, zero], axis=1), jnp.concatenate([zero, b], axis=1)], axis=0)


def _dn_chunk_kernel(q_ref, k_ref, v_ref, z_ref, gc_ref, gt_ref, nw_ref, o_ref,
                     state_ref, qm_ref, on_ref, cd_ref, *, nt, n_kh):
    g = pl.program_id(0)
    kh = (jnp.minimum(g, pl.num_programs(0) - 2) // nt) % n_kh
    n_chunks = q_ref.shape[0] // CHUNK
    slot_a = g % 2
    slot_b = 1 - slot_a

    @pl.when(g == 0)
    def _():
        qm_ref[1] = jnp.zeros(qm_ref.shape[1:], qm_ref.dtype)
        on_ref[1] = jnp.zeros(on_ref.shape[1:], on_ref.dtype)
        cd_ref[1] = jnp.zeros(cd_ref.shape[1:], cd_ref.dtype)

    @pl.when(jnp.maximum(g - 1, 0) % nt == 0)
    def _():
        state_ref[...] = jnp.zeros_like(state_ref)

    row = lax.broadcasted_iota(jnp.int32, (CHUNK, LANES), 0)
    lane = lax.broadcasted_iota(jnp.int32, (CHUNK, LANES), 1)
    col = lane % CHUNK
    causal = col <= row
    strict = col < row
    upper_half = lane >= CHUNK
    lower_half = lane < CHUNK
    eye = col == row
    eye_hi = jnp.where(jnp.logical_and(upper_half, eye), 1.0, 0.0).astype(F32)
    eye_lo = jnp.where(jnp.logical_and(lower_half, eye), 1.0, 0.0).astype(F32)
    lower_half_b = lane.astype(F32).astype(BF16) < CHUNK
    zero_b = jnp.zeros((CHUNK, LANES), BF16)
    norm_w = nw_ref[...]
    state = [state_ref[...]]

    def recurrence_chunk(c):
        rows = slice(c * CHUNK, (c + 1) * CHUNK)
        s_b = state[0].astype(BF16)
        qms = _dot(qm_ref[slot_b, c], _block_diag(s_b[:, :HEAD_DIM], s_b[:, HEAD_DIM:]))
        o_n = on_ref[slot_b, c]
        o_all = qms[:CHUNK] + o_n[:CHUNK]
        state[0] = state[0] * cd_ref[slot_b, c][:1, :] + o_n[CHUNK:] - qms[CHUNK:]
        for e in range(2):
            o = o_all[:, e * HEAD_DIM:(e + 1) * HEAD_DIM]
            o = o * lax.rsqrt(jnp.mean(o * o, axis=-1, keepdims=True) + RMS_EPS) * norm_w
            ze = z_ref[rows, e * HEAD_DIM:(e + 1) * HEAD_DIM].astype(F32)
            o_ref[rows, e * HEAD_DIM:(e + 1) * HEAD_DIM] = (o * _silu(ze)).astype(o_ref.dtype)

    gsel_all = pltpu.roll(gc_ref[...], (LANES - SUBLANES * kh) % LANES, axis=1)
    kf, qf, k_t, qk2, kk2 = {}, {}, {}, {}, {}
    beta_col, g_col, g_last, decay, x_mat, w_mat, sol, exp_g = {}, {}, {}, {}, {}, {}, {}, {}

    def load_chunk(c):
        rows = slice(c * CHUNK, (c + 1) * CHUNK)
        qc, kc = q_ref[rows, :], k_ref[rows, :]
        qf[c] = qc.astype(F32)
        kf[c] = kc.astype(F32)
        k_t[c] = jnp.transpose(kf[c]).astype(BF16)
        qkk = _dot_nt(jnp.concatenate([qc, kc], axis=0), jnp.concatenate([kc, kc], axis=0))
        qk2[c] = qkk[:CHUNK]
        kk2[c] = qkk[CHUNK:]

    def gate_terms(c, e):
        gsel = gsel_all[c * CHUNK:(c + 1) * CHUNK]
        g_tok = gt_ref[2 + e:3 + e, c * CHUNK:(c + 1) * CHUNK]
        g_row = jnp.concatenate([g_tok, g_tok], axis=1)
        beta_col[c, e] = jnp.broadcast_to(gsel[:, e:e + 1], (CHUNK, LANES))
        g_col[c, e] = jnp.broadcast_to(gsel[:, 2 + e:3 + e], (CHUNK, LANES))
        g_last[c, e] = g_row[:, CHUNK - 1:CHUNK]
        decay[c, e] = jnp.exp(jnp.where(causal, g_col[c, e] - g_row, NEG_BIG))
        x_mat[c, e] = jnp.where(strict, -(kk2[c] * beta_col[c, e] * decay[c, e]), 0.0)

    def square(c):
        xa, xb = x_mat[c, 0], x_mat[c, 1]
        sq = _dot(jnp.where(lower_half, xa, xb).astype(BF16), _block_diag(xa.astype(BF16), xb.astype(BF16)))
        w_mat[c, 0] = jnp.where(upper_half, xa + eye_hi, sq[:, :LANES]).astype(BF16)
        w_mat[c, 1] = jnp.where(upper_half, sq[:, LANES:], xb + eye_lo).astype(BF16)

    def advance(c):
        wa, wb = w_mat[c, 0], w_mat[c, 1]
        pw = _dot(jnp.where(lower_half_b, wa, wb), _block_diag(wa, wb))
        w_mat[c, 0] = pw[:, :LANES].astype(BF16) + jnp.where(lower_half_b, zero_b, wa)
        w_mat[c, 1] = pw[:, LANES:].astype(BF16) + jnp.where(lower_half_b, wb, zero_b)

    def solve(c, e):
        t_mat = w_mat[c, e][:, CHUNK:] if e == 0 else w_mat[c, e][:, :CHUNK]
        ve = v_ref[c * CHUNK:(c + 1) * CHUNK, e * HEAD_DIM:(e + 1) * HEAD_DIM].astype(F32)
        exp_g[c, e] = jnp.exp(g_col[c, e])
        rhs = jnp.concatenate([kf[c] * (beta_col[c, e] * exp_g[c, e]), ve * beta_col[c, e]], axis=1)
        sol[c, e] = _dot(t_mat, rhs.astype(BF16))

    def finish(c):
        qm, o_n, cd = [], [], []
        for e in range(2):
            a_qk = (qk2[c] * decay[c, e])[:, :CHUNK].astype(BF16)
            qo = _dot(a_qk, sol[c, e].astype(BF16))
            q_eff = qf[c] * exp_g[c, e] - qo[:, :HEAD_DIM]
            k_tail = jnp.exp(g_last[c, e] - g_col[c, e])[:, :1]
            mn = _dot(k_t[c], (sol[c, e] * k_tail).astype(BF16))
            qm.append(jnp.concatenate([q_eff, mn[:, :HEAD_DIM]], axis=0).astype(BF16))
            o_n.append(jnp.concatenate([qo[:, HEAD_DIM:], mn[:, HEAD_DIM:]], axis=0))
            cd.append(jnp.broadcast_to(jnp.exp(g_last[c, e]), (SUBLANES, LANES)))
        qm_ref[slot_a, c] = jnp.concatenate(qm, axis=1)
        on_ref[slot_a, c] = jnp.concatenate(o_n, axis=1)
        cd_ref[slot_a, c] = jnp.concatenate(cd, axis=1)

    chunks = range(n_chunks)
    items = []
    for g0 in range(0, n_chunks, PREP_GROUP):
        group = range(g0, min(g0 + PREP_GROUP, n_chunks))
        chains = [(c, e) for c in group for e in range(2)]
        items += [functools.partial(load_chunk, c) for c in group]
        items += [functools.partial(gate_terms, c, e) for (c, e) in chains]
        items += [functools.partial(square, c) for c in group]
        items += [functools.partial(advance, c) for _ in range(5) for c in group]
        items += [functools.partial(solve, c, e) for (c, e) in chains]
        items += [functools.partial(finish, c) for c in group]
    for idx, item in enumerate(items):
        for c in chunks:
            if idx == c * len(items) // n_chunks:
                recurrence_chunk(c)
        item()

    state_ref[...] = state[0]


def _dn_chunk(qk, v, z, gate_c, gate_t, norm_w, batch, seq, n_kh, tc):
    m = qk.shape[0]
    nt = seq // tc
    cpt = tc // CHUNK

    n_tiles = batch * n_kh * nt

    def split(t):
        return (t // (n_kh * nt)) * nt + t % nt, (t // nt) % n_kh

    def prep(g):
        return split(jnp.minimum(g, n_tiles - 1))

    def rec(g):
        return split(jnp.maximum(g - 1, 0))

    return pl.pallas_call(
        functools.partial(_dn_chunk_kernel, nt=nt, n_kh=n_kh),
        grid=(n_tiles + 1,),
        in_specs=[
            pl.BlockSpec((tc, HEAD_DIM), lambda g: prep(g)),
            pl.BlockSpec((tc, HEAD_DIM), lambda g: (prep(g)[0], n_kh + prep(g)[1])),
            pl.BlockSpec((tc, 2 * HEAD_DIM), lambda g: prep(g)),
            pl.BlockSpec((tc, 2 * HEAD_DIM), lambda g: rec(g)),
            pl.BlockSpec((tc, gate_c.shape[1]), lambda g: (prep(g)[0], 0)),
            pl.BlockSpec((None, SUBLANES, tc), lambda g: (prep(g)[1], 0, prep(g)[0])),
            pl.BlockSpec((1, HEAD_DIM), lambda g: (0, 0)),
        ],
        out_specs=pl.BlockSpec((tc, 2 * HEAD_DIM), lambda g: rec(g)),
        out_shape=jax.ShapeDtypeStruct((m, 2 * n_kh * HEAD_DIM), BF16),
        scratch_shapes=[pltpu.VMEM((HEAD_DIM, 2 * HEAD_DIM), F32),
                        pltpu.VMEM((2, cpt, CHUNK + HEAD_DIM, 2 * HEAD_DIM), BF16),
                        pltpu.VMEM((2, cpt, CHUNK + HEAD_DIM, 2 * HEAD_DIM), F32),
                        pltpu.VMEM((2, cpt, SUBLANES, 2 * LANES), F32)],
        compiler_params=_cparams(1),
        name="dn_chunk",
    )(qk, qk, v, z, gate_c, gate_t, norm_w)


def _tile(n, want):
    if n <= want:
        return n
    t = want - want % LANES
    while n % t:
        t -= LANES
    return t


def _gate_slots(w_beta_a, a_log, dt_bias, n_vh):
    n_kh = n_vh // 2
    d = w_beta_a.shape[0]
    wb = w_beta_a[:, :n_vh].reshape(d, n_kh, 2)
    wa = w_beta_a[:, n_vh:].reshape(d, n_kh, 2)
    w_slot = jnp.concatenate([wb, wa, jnp.zeros((d, n_kh, SUBLANES - 4), w_beta_a.dtype)], axis=2)
    zeros2 = jnp.zeros((n_kh, 2), F32)
    zeros4 = jnp.zeros((n_kh, SUBLANES - 4), F32)
    al = jnp.concatenate([zeros2, a_log.astype(F32).reshape(n_kh, 2), zeros4], axis=1)
    dt = jnp.concatenate([zeros2, dt_bias.astype(F32).reshape(n_kh, 2), zeros4], axis=1)
    pad = (-n_kh * SUBLANES) % LANES
    w_slot = jnp.pad(w_slot.reshape(d, n_kh * SUBLANES), ((0, 0), (0, pad)))
    return w_slot, jnp.pad(al.reshape(-1), (0, pad)), jnp.pad(dt.reshape(-1), (0, pad))


def kernel(x, sc_w_in, sc_conv_w, sc_w_out, dn_w_in, dn_conv_w, dn_a_log, dn_dt_bias, dn_norm_w,
           dn_w_out, ffn_w_gate_up, ffn_w_down, ln_gain, ln_bias):
    batch, seq, d = x.shape
    depth = ln_gain.shape[0]
    alpha = float((2 * depth) ** 0.25)
    m = batch * seq
    n_vh = dn_a_log.shape[1]
    n_kh = n_vh // 2
    key_dim = n_kh * HEAD_DIM
    val_dim = n_vh * HEAD_DIM
    qkv_dim = 2 * key_dim + val_dim
    hidden = ffn_w_down.shape[1]
    assert seq % CHUNK == 0 and d % LANES == 0 and dn_norm_w.shape[1] == HEAD_DIM

    tm_in = _tile(seq, 1024)
    tm_dn = _tile(seq, 2048)
    tm_out = _tile(seq, 512)
    tn = _tile(d, 512)
    th = _tile(hidden, 512)
    tc = _tile(seq, 1024)

    sc_w_in_b, sc_w_out_b = sc_w_in.astype(BF16), sc_w_out.astype(BF16)
    dn_w_in_b, dn_w_out_b = dn_w_in.astype(BF16), dn_w_out.astype(BF16)
    ffn_w_gu_b, ffn_w_down_b = ffn_w_gate_up.astype(BF16), ffn_w_down.astype(BF16)

    h = x.reshape(m, d)
    hb = h
    for i in range(depth):
        l = i // 2
        gb0 = jnp.stack([ln_gain[i, 0], ln_bias[i, 0]])
        gb1 = jnp.stack([ln_gain[i, 1], ln_bias[i, 1]])
        if i % 2 == 0:
            a = _sc_in(hb, sc_w_in_b, l, sc_conv_w[l], seq, tm_in, tn)
            h = _proj_ln(a, sc_w_out_b, l, h, gb0, alpha, tm_out)
        else:
            cw = dn_conv_w[l]
            qk = _dn_in(hb, dn_w_in_b, l, cw, 0, 2 * key_dim, seq, tm_dn, tn, conv=True, l2norm=True,
                        n_scaled=key_dim // tn, name="dn_in_qk")
            v = _dn_in(hb, dn_w_in_b, l, cw, 2 * key_dim, val_dim, seq, tm_dn, tn, conv=True, name="dn_in_v")
            z = _dn_in(hb, dn_w_in_b, l, cw, qkv_dim, val_dim, seq, tm_dn, tn, conv=False, name="dn_in_z")
            w_slot, al_slot, dt_slot = _gate_slots(dn_w_in_b[l][:, qkv_dim + val_dim:], dn_a_log[l],
                                                   dn_dt_bias[l], n_vh)
            gate_c, gate_t = _dn_gates(hb, w_slot, al_slot, dt_slot, tm_dn)
            gate_t = gate_t[:n_kh * SUBLANES].reshape(n_kh, SUBLANES, m)
            o = _dn_chunk(qk, v, z, gate_c, gate_t, dn_norm_w[l][None, :], batch, seq, n_kh, tc)
            h = _proj_ln(o, dn_w_out_b, l, h, gb0, alpha, tm_out)
        h, hb = _ffn(h, ffn_w_gu_b, ffn_w_down_b, i, gb1, alpha, tm_out, th, emit_bf16=i + 1 < depth)
    return h.reshape(batch, seq, d)
```

```python
import functools

import jax
import jax.numpy as jnp
from jax import lax
from jax.experimental import pallas as pl
from jax.experimental.pallas import tpu as pltpu

HEAD_DIM = 128
CHUNK = 64
SUBLANES = 8
LANES = 128
LN_EPS = 1e-5
RMS_EPS = 1e-6
NEG_BIG = -1e30
PREP_GROUP = 16
VMEM_LIMIT_BYTES = 60 * 1024 * 1024

F32 = jnp.float32
BF16 = jnp.bfloat16


def _cparams(n_axes):
    return pltpu.CompilerParams(
        dimension_semantics=("arbitrary",) * n_axes,
        vmem_limit_bytes=VMEM_LIMIT_BYTES)


def _dot(a, b):
    return jnp.dot(a, b, preferred_element_type=F32)


def _dot_nt(a, b):
    return lax.dot_general(a, b, (((1,), (1,)), ((), ())), preferred_element_type=F32)


def _dot_tn(a, b):
    return lax.dot_general(a, b, (((0,), (0,)), ((), ())), preferred_element_type=F32)


def _layer_norm(y, gain, bias):
    mu = jnp.mean(y, axis=-1, keepdims=True)
    d = y - mu
    var = jnp.mean(d * d, axis=-1, keepdims=True)
    return d * lax.rsqrt(var + LN_EPS) * gain + bias


def _silu(t):
    return t * jax.nn.sigmoid(t)


def _causal_taps(ext, w_rows):
    k = len(w_rows)
    assert k <= SUBLANES
    s1 = pltpu.roll(ext, 1, axis=0)
    out = None
    for d0 in range(0, k, 2):
        term = ext * w_rows[k - 1 - d0]
        if d0 + 1 < k:
            term = term + s1 * w_rows[k - 2 - d0]
        if d0:
            term = pltpu.roll(term, d0, axis=0)
        out = term if out is None else out + term
    return out[SUBLANES:]


def _sc_in_kernel(x_ref, wb_ref, wc_ref, wh_ref, cw_ref, o_ref, *scratch, tiles_per_seq, n_sub):
    i, j = pl.program_id(0), pl.program_id(1)
    if x_ref.dtype == BF16:
        xb_ref = x_ref
        u0_ref, u1_ref, g0_ref, g1_ref, carry_ref = scratch
    else:
        xb_ref, u0_ref, u1_ref, g0_ref, g1_ref, carry_ref = scratch

        @pl.when(j == 0)
        def _():
            xb_ref[...] = x_ref[...].astype(BF16)

    ubuf, gbuf = (u0_ref, u1_ref), (g0_ref, g1_ref)
    tm = o_ref.shape[0]
    rs = tm // n_sub
    ksize = cw_ref.shape[0]

    @pl.when(i % tiles_per_seq == 0)
    def _():
        u0_ref[0:SUBLANES, :] = jnp.zeros((SUBLANES, u0_ref.shape[1]), F32)

    @pl.when(i % tiles_per_seq != 0)
    def _():
        u0_ref[0:SUBLANES, :] = carry_ref[j]

    def epilogue(b):
        for s in range(o_ref.shape[1] // LANES):
            lanes = slice(s * LANES, (s + 1) * LANES)
            y = _causal_taps(ubuf[b % 2][:, lanes], [cw_ref[k:k + 1, lanes] for k in range(ksize)])
            o_ref[b * rs:(b + 1) * rs, lanes] = (gbuf[b % 2][:, lanes] * y).astype(o_ref.dtype)

    tail = None
    for b in range(n_sub):
        xblk = xb_ref[b * rs:(b + 1) * rs, :]
        u = _dot(xblk, wc_ref[...]) * _dot(xblk, wh_ref[...])
        ubuf[b % 2][SUBLANES:SUBLANES + rs, :] = u
        gbuf[b % 2][...] = _dot(xblk, wb_ref[...])
        if b > 0:
            epilogue(b - 1)
            ubuf[b % 2][0:SUBLANES, :] = tail
        tail = u[rs - SUBLANES:, :]
    epilogue(n_sub - 1)
    carry_ref[j] = tail


def _sc_in(x, w_in, layer, conv_w, seq, tm, tn):
    m, d = x.shape
    nd = d // tn
    n_sub = max(1, tm // 128)
    rs = tm // n_sub
    ublk, gblk = pltpu.VMEM((rs + SUBLANES, tn), F32), pltpu.VMEM((rs, tn), F32)
    scratch = [ublk, ublk, gblk, gblk, pltpu.VMEM((nd, SUBLANES, tn), F32)]
    if x.dtype != BF16:
        scratch = [pltpu.VMEM((tm, d), BF16)] + scratch
    return pl.pallas_call(
        functools.partial(_sc_in_kernel, tiles_per_seq=seq // tm, n_sub=n_sub),
        grid=(m // tm, nd),
        in_specs=[
            pl.BlockSpec((tm, d), lambda i, j: (i, 0)),
            pl.BlockSpec((None, d, tn), lambda i, j: (layer, 0, j)),
            pl.BlockSpec((None, d, tn), lambda i, j: (layer, 0, nd + j)),
            pl.BlockSpec((None, d, tn), lambda i, j: (layer, 0, 2 * nd + j)),
            pl.BlockSpec((conv_w.shape[0], tn), lambda i, j: (0, j)),
        ],
        out_specs=pl.BlockSpec((tm, tn), lambda i, j: (i, j)),
        out_shape=jax.ShapeDtypeStruct((m, d), BF16),
        scratch_shapes=scratch,
        compiler_params=_cparams(2),
        name="sc_in",
    )(x, w_in, w_in, w_in, conv_w)


def _residual_ln_blocks(n_sub, rs, pre_norm, gb_ref, out_refs):
    gain, bias = gb_ref[0:1, :], gb_ref[1:2, :]

    def finish(b, y):
        out = _layer_norm(y, gain, bias)
        for r in out_refs:
            r[b * rs:(b + 1) * rs, :] = out.astype(r.dtype)

    pending = pre_norm(0)
    for b in range(1, n_sub):
        nxt = pre_norm(b)
        finish(b - 1, pending)
        pending = nxt
    finish(n_sub - 1, pending)


def _proj_ln_kernel(a_ref, w_ref, x_ref, gb_ref, o_ref, *, alpha, n_sub):
    rs = o_ref.shape[0] // n_sub

    def pre_norm(b):
        rows = slice(b * rs, (b + 1) * rs)
        return alpha * x_ref[rows, :] + _dot(a_ref[rows, :], w_ref[...])

    _residual_ln_blocks(n_sub, rs, pre_norm, gb_ref, [o_ref])


def _proj_ln(a, w, layer, x, gain_bias, alpha, tm):
    m, d = x.shape
    k = a.shape[1]
    return pl.pallas_call(
        functools.partial(_proj_ln_kernel, alpha=alpha, n_sub=max(1, tm // 128)),
        grid=(m // tm,),
        in_specs=[
            pl.BlockSpec((tm, k), lambda i: (i, 0)),
            pl.BlockSpec((None, k, d), lambda i: (layer, 0, 0), pipeline_mode=pl.Buffered(1)),
            pl.BlockSpec((tm, d), lambda i: (i, 0)),
            pl.BlockSpec((2, d), lambda i: (0, 0)),
        ],
        out_specs=pl.BlockSpec((tm, d), lambda i: (i, 0)),
        out_shape=jax.ShapeDtypeStruct((m, d), F32),
        compiler_params=_cparams(1),
        name="proj_ln",
    )(a, w, x, gain_bias)


def _ffn_kernel(x_ref, wg_ref, wu_ref, wd_ref, gb_ref, o_ref, *rest, alpha):
    *ob_ref, xb_ref, acc_ref = rest
    j = pl.program_id(1)

    @pl.when(j == 0)
    def _():
        xb_ref[...] = x_ref[...].astype(BF16)
        acc_ref[...] = jnp.zeros_like(acc_ref)

    xb = xb_ref[...]
    act = _silu(_dot(xb, wg_ref[...])) * _dot(xb, wu_ref[...])
    acc_ref[...] += _dot(act.astype(BF16), wd_ref[...])

    @pl.when(j == pl.num_programs(1) - 1)
    def _():
        y = alpha * x_ref[...] + acc_ref[...]
        out = _layer_norm(y, gb_ref[0:1, :], gb_ref[1:2, :])
        o_ref[...] = out
        for r in ob_ref:
            r[...] = out.astype(BF16)


def _ffn(x, w_gate_up, w_down, layer, gain_bias, alpha, tm, th, emit_bf16):
    m, d = x.shape
    hidden = w_down.shape[1]
    nh = hidden // th
    row_block = pl.BlockSpec((tm, d), lambda i, j: (i, 0))
    outs = pl.pallas_call(
        functools.partial(_ffn_kernel, alpha=alpha),
        grid=(m // tm, nh),
        in_specs=[
            row_block,
            pl.BlockSpec((None, d, th), lambda i, j: (layer, 0, j)),
            pl.BlockSpec((None, d, th), lambda i, j: (layer, 0, nh + j)),
            pl.BlockSpec((None, th, d), lambda i, j: (layer, j, 0)),
            pl.BlockSpec((2, d), lambda i, j: (0, 0)),
        ],
        out_specs=[row_block] * (2 if emit_bf16 else 1),
        out_shape=[jax.ShapeDtypeStruct((m, d), F32)] + [jax.ShapeDtypeStruct((m, d), BF16)] * emit_bf16,
        scratch_shapes=[pltpu.VMEM((tm, d), BF16), pltpu.VMEM((tm, d), F32)],
        compiler_params=_cparams(2),
        name="ffn",
    )(x, w_gate_up, w_gate_up, w_down, gain_bias)
    return outs if emit_bf16 else (outs[0], None)


def _dn_in_kernel(xb_ref, w_ref, cw_ref, o_ref, *scratch, tiles_per_seq, conv, l2norm, n_scaled, n_sub):
    i, j = pl.program_id(0), pl.program_id(1)
    tm = o_ref.shape[0]
    rs = tm // n_sub

    def block_dot(b):
        return _dot(xb_ref[b * rs:(b + 1) * rs, :], w_ref[...])

    if not conv:
        for b in range(n_sub):
            o_ref[b * rs:(b + 1) * rs, :] = block_dot(b).astype(o_ref.dtype)
        return

    acc0_ref, acc1_ref, carry_ref = scratch
    acc = (acc0_ref, acc1_ref)
    ksize = cw_ref.shape[0]

    @pl.when(i % tiles_per_seq == 0)
    def _():
        acc0_ref[0:SUBLANES, :] = jnp.zeros((SUBLANES, acc0_ref.shape[1]), F32)

    @pl.when(i % tiles_per_seq != 0)
    def _():
        acc0_ref[0:SUBLANES, :] = carry_ref[j]

    scale = jnp.where(j < n_scaled, HEAD_DIM ** -0.5, 1.0).astype(F32)

    def epilogue(b):
        src = acc[b % 2]
        for s in range(o_ref.shape[1] // HEAD_DIM):
            lanes = slice(s * HEAD_DIM, (s + 1) * HEAD_DIM)
            t = _silu(_causal_taps(src[:, lanes], [cw_ref[k:k + 1, lanes] for k in range(ksize)]))
            if l2norm:
                t = t * (lax.rsqrt(jnp.sum(t * t, axis=-1, keepdims=True) + RMS_EPS) * scale)
            o_ref[b * rs:(b + 1) * rs, lanes] = t.astype(o_ref.dtype)

    tail = None
    for b in range(n_sub):
        res = block_dot(b)
        acc[b % 2][SUBLANES:SUBLANES + rs, :] = res
        if b > 0:
            epilogue(b - 1)
            acc[b % 2][0:SUBLANES, :] = tail
        tail = res[rs - SUBLANES:, :]
    epilogue(n_sub - 1)
    carry_ref[j] = tail


def _dn_in(xb, w_in, layer, conv_w, col0, width, seq, tm, tn, *, conv, l2norm=False, n_scaled=0,
           block_rows=128, name):
    m, d = xb.shape
    nj = width // tn
    j0 = col0 // tn
    n_sub = max(1, tm // block_rows)
    blk = pltpu.VMEM((tm // n_sub + SUBLANES, tn), F32)
    scratch = [blk, blk, pltpu.VMEM((nj, SUBLANES, tn), F32)] if conv else []
    cw_j0 = j0 if conv else 0
    return pl.pallas_call(
        functools.partial(_dn_in_kernel, tiles_per_seq=seq // tm, conv=conv, l2norm=l2norm,
                          n_scaled=n_scaled, n_sub=n_sub),
        grid=(m // tm, nj),
        in_specs=[
            pl.BlockSpec((tm, d), lambda i, j: (i, 0)),
            pl.BlockSpec((None, d, tn), lambda i, j: (layer, 0, j0 + j)),
            pl.BlockSpec((conv_w.shape[0], tn), lambda i, j: (0, cw_j0 + j if conv else 0)),
        ],
        out_specs=pl.BlockSpec((tm, tn), lambda i, j: (i, j)),
        out_shape=jax.ShapeDtypeStruct((m, width), BF16),
        scratch_shapes=scratch,
        compiler_params=_cparams(2),
        name=name,
    )(xb, w_in, conv_w)


def _softplus(t):
    return jnp.maximum(t, 0.0) + jnp.log1p(jnp.exp(-jnp.abs(t)))


def _chunk_cumsum(g, axis):
    pos = lax.broadcasted_iota(jnp.int32, g.shape, axis) % CHUNK
    s = 1
    while s < CHUNK:
        g = g + jnp.where(pos >= s, pltpu.roll(g, s, axis=axis), 0.0)
        s *= 2
    return g


def _gates(raw, a_log, dt_bias, slot_axis, token_axis):
    c = lax.broadcasted_iota(jnp.int32, raw.shape, slot_axis) % SUBLANES
    beta = jax.nn.sigmoid(raw)
    g = -jnp.exp(a_log) * _softplus(raw + dt_bias)
    g = _chunk_cumsum(jnp.where(jnp.logical_and(c >= 2, c < 4), g, 0.0), token_axis)
    return jnp.where(c < 2, beta, g)


def _dn_gate_kernel(x_ref, w_ref, wt_ref, al_ref, dt_ref, alt_ref, dtt_ref, gc_ref, gt_ref):
    xb = x_ref[...]
    gc_ref[...] = _gates(_dot(xb, w_ref[...]), al_ref[...], dt_ref[...], 1, 0)
    gt_ref[...] = _gates(_dot_nt(wt_ref[...], xb), alt_ref[...], dtt_ref[...], 0, 1)


def _dn_gates(x, w_slot, a_log_slot, dt_slot, tm):
    m, d = x.shape
    ns = w_slot.shape[1]
    return pl.pallas_call(
        _dn_gate_kernel,
        grid=(m // tm,),
        in_specs=[
            pl.BlockSpec((tm, d), lambda i: (i, 0)),
            pl.BlockSpec((d, ns), lambda i: (0, 0)),
            pl.BlockSpec((ns, d), lambda i: (0, 0)),
            pl.BlockSpec((1, ns), lambda i: (0, 0)),
            pl.BlockSpec((1, ns), lambda i: (0, 0)),
            pl.BlockSpec((ns, 1), lambda i: (0, 0)),
            pl.BlockSpec((ns, 1), lambda i: (0, 0)),
        ],
        out_specs=[pl.BlockSpec((tm, ns), lambda i: (i, 0)),
                   pl.BlockSpec((ns, tm), lambda i: (0, i))],
        out_shape=[jax.ShapeDtypeStruct((m, ns), F32), jax.ShapeDtypeStruct((ns, m), F32)],
        compiler_params=_cparams(1),
        name="dn_gates",
    )(x, w_slot, w_slot.T, a_log_slot[None, :], dt_slot[None, :], a_log_slot[:, None], dt_slot[:, None])


def _block_diag(a, b):
    zero = jnp.zeros_like(a)
    return jnp.concatenate([jnp.concatenate([a, zero], axis=1), jnp.concatenate([zero, b], axis=1)], axis=0)


def _dn_chunk_kernel(q_ref, k_ref, v_ref, z_ref, gc_ref, gt_ref, nw_ref, o_ref,
                     state_ref, qm_ref, on_ref, cd_ref, *, nt, n_kh):
    g = pl.program_id(0)
    kh = (jnp.minimum(g, pl.num_programs(0) - 2) // nt) % n_kh
    n_chunks = q_ref.shape[0] // CHUNK
    slot_a = g % 2
    slot_b = 1 - slot_a

    @pl.when(g == 0)
    def _():
        qm_ref[1] = jnp.zeros(qm_ref.shape[1:], qm_ref.dtype)
        on_ref[1] = jnp.zeros(on_ref.shape[1:], on_ref.dtype)
        cd_ref[1] = jnp.zeros(cd_ref.shape[1:], cd_ref.dtype)

    @pl.when(jnp.maximum(g - 1, 0) % nt == 0)
    def _():
        state_ref[...] = jnp.zeros_like(state_ref)

    row = lax.broadcasted_iota(jnp.int32, (CHUNK, LANES), 0)
    lane = lax.broadcasted_iota(jnp.int32, (CHUNK, LANES), 1)
    col = lane % CHUNK
    causal = col <= row
    strict = col < row
    upper_half = lane >= CHUNK
    lower_half = lane < CHUNK
    eye = col == row
    eye_hi = jnp.where(jnp.logical_and(upper_half, eye), 1.0, 0.0).astype(F32)
    eye_lo = jnp.where(jnp.logical_and(lower_half, eye), 1.0, 0.0).astype(F32)
    lower_half_b = lane.astype(F32).astype(BF16) < CHUNK
    zero_b = jnp.zeros((CHUNK, LANES), BF16)
    norm_w = nw_ref[...]
    state = [state_ref[...]]

    def recurrence_chunk(c):
        rows = slice(c * CHUNK, (c + 1) * CHUNK)
        s_b = state[0].astype(BF16)
        qms = _dot(qm_ref[slot_b, c], _block_diag(s_b[:, :HEAD_DIM], s_b[:, HEAD_DIM:]))
        o_n = on_ref[slot_b, c]
        o_all = qms[:CHUNK] + o_n[:CHUNK]
        state[0] = state[0] * cd_ref[slot_b, c][:1, :] + o_n[CHUNK:] - qms[CHUNK:]
        for e in range(2):
            o = o_all[:, e * HEAD_DIM:(e + 1) * HEAD_DIM]
            o = o * lax.rsqrt(jnp.mean(o * o, axis=-1, keepdims=True) + RMS_EPS) * norm_w
            ze = z_ref[rows, e * HEAD_DIM:(e + 1) * HEAD_DIM].astype(F32)
            o_ref[rows, e * HEAD_DIM:(e + 1) * HEAD_DIM] = (o * _silu(ze)).astype(o_ref.dtype)

    gsel_all = pltpu.roll(gc_ref[...], (LANES - SUBLANES * kh) % LANES, axis=1)
    kf, qf, k_t, qk2, kk2 = {}, {}, {}, {}, {}
    beta_col, g_col, g_last, decay, x_mat, w_mat, sol, exp_g = {}, {}, {}, {}, {}, {}, {}, {}

    def load_chunk(c):
        rows = slice(c * CHUNK, (c + 1) * CHUNK)
        qc, kc = q_ref[rows, :], k_ref[rows, :]
        qf[c] = qc.astype(F32)
        kf[c] = kc.astype(F32)
        k_t[c] = jnp.transpose(kf[c]).astype(BF16)
        qkk = _dot_nt(jnp.concatenate([qc, kc], axis=0), jnp.concatenate([kc, kc], axis=0))
        qk2[c] = qkk[:CHUNK]
        kk2[c] = qkk[CHUNK:]

    def gate_terms(c, e):
        gsel = gsel_all[c * CHUNK:(c + 1) * CHUNK]
        g_tok = gt_ref[2 + e:3 + e, c * CHUNK:(c + 1) * CHUNK]
        g_row = jnp.concatenate([g_tok, g_tok], axis=1)
        beta_col[c, e] = jnp.broadcast_to(gsel[:, e:e + 1], (CHUNK, LANES))
        g_col[c, e] = jnp.broadcast_to(gsel[:, 2 + e:3 + e], (CHUNK, LANES))
        g_last[c, e] = g_row[:, CHUNK - 1:CHUNK]
        decay[c, e] = jnp.exp(jnp.where(causal, g_col[c, e] - g_row, NEG_BIG))
        x_mat[c, e] = jnp.where(strict, -(kk2[c] * beta_col[c, e] * decay[c, e]), 0.0)

    def square(c):
        xa, xb = x_mat[c, 0], x_mat[c, 1]
        sq = _dot(jnp.where(lower_half, xa, xb).astype(BF16), _block_diag(xa.astype(BF16), xb.astype(BF16)))
        w_mat[c, 0] = jnp.where(upper_half, xa + eye_hi, sq[:, :LANES]).astype(BF16)
        w_mat[c, 1] = jnp.where(upper_half, sq[:, LANES:], xb + eye_lo).astype(BF16)

    def advance(c):
        wa, wb = w_mat[c, 0], w_mat[c, 1]
        pw = _dot(jnp.where(lower_half_b, wa, wb), _block_diag(wa, wb))
        w_mat[c, 0] = pw[:, :LANES].astype(BF16) + jnp.where(lower_half_b, zero_b, wa)
        w_mat[c, 1] = pw[:, LANES:].astype(BF16) + jnp.where(lower_half_b, wb, zero_b)

    def solve(c, e):
        t_mat = w_mat[c, e][:, CHUNK:] if e == 0 else w_mat[c, e][:, :CHUNK]
        ve = v_ref[c * CHUNK:(c + 1) * CHUNK, e * HEAD_DIM:(e + 1) * HEAD_DIM].astype(F32)
        exp_g[c, e] = jnp.exp(g_col[c, e])
        rhs = jnp.concatenate([kf[c] * (beta_col[c, e] * exp_g[c, e]), ve * beta_col[c, e]], axis=1)
        sol[c, e] = _dot(t_mat, rhs.astype(BF16))

    def finish(c):
        qm, o_n, cd = [], [], []
        for e in range(2):
            a_qk = (qk2[c] * decay[c, e])[:, :CHUNK].astype(BF16)
            qo = _dot(a_qk, sol[c, e].astype(BF16))
            q_eff = qf[c] * exp_g[c, e] - qo[:, :HEAD_DIM]
            k_tail = jnp.exp(g_last[c, e] - g_col[c, e])[:, :1]
            mn = _dot(k_t[c], (sol[c, e] * k_tail).astype(BF16))
            qm.append(jnp.concatenate([q_eff, mn[:, :HEAD_DIM]], axis=0).astype(BF16))
            o_n.append(jnp.concatenate([qo[:, HEAD_DIM:], mn[:, HEAD_DIM:]], axis=0))
            cd.append(jnp.broadcast_to(jnp.exp(g_last[c, e]), (SUBLANES, LANES)))
        qm_ref[slot_a, c] = jnp.concatenate(qm, axis=1)
        on_ref[slot_a, c] = jnp.concatenate(o_n, axis=1)
        cd_ref[slot_a, c] = jnp.concatenate(cd, axis=1)

    chunks = range(n_chunks)
    items = []
    for g0 in range(0, n_chunks, PREP_GROUP):
        group = range(g0, min(g0 + PREP_GROUP, n_chunks))
        chains = [(c, e) for c in group for e in range(2)]
        items += [functools.partial(load_chunk, c) for c in group]
        items += [functools.partial(gate_terms, c, e) for (c, e) in chains]
        items += [functools.partial(square, c) for c in group]
        items += [functools.partial(advance, c) for _ in range(5) for c in group]
        items += [functools.partial(solve, c, e) for (c, e) in chains]
        items += [functools.partial(finish, c) for c in group]
    for idx, item in enumerate(items):
        for c in chunks:
            if idx == c * len(items) // n_chunks:
                recurrence_chunk(c)
        item()

    state_ref[...] = state[0]


def _dn_chunk(qk, v, z, gate_c, gate_t, norm_w, batch, seq, n_kh, tc):
    m = qk.shape[0]
    nt = seq // tc
    cpt = tc // CHUNK

    n_tiles = batch * n_kh * nt

    def split(t):
        return (t // (n_kh * nt)) * nt + t % nt, (t // nt) % n_kh

    def prep(g):
        return split(jnp.minimum(g, n_tiles - 1))

    def rec(g):
        return split(jnp.maximum(g - 1, 0))

    return pl.pallas_call(
        functools.partial(_dn_chunk_kernel, nt=nt, n_kh=n_kh),
        grid=(n_tiles + 1,),
        in_specs=[
            pl.BlockSpec((tc, HEAD_DIM), lambda g: prep(g)),
            pl.BlockSpec((tc, HEAD_DIM), lambda g: (prep(g)[0], n_kh + prep(g)[1])),
            pl.BlockSpec((tc, 2 * HEAD_DIM), lambda g: prep(g)),
            pl.BlockSpec((tc, 2 * HEAD_DIM), lambda g: rec(g)),
            pl.BlockSpec((tc, gate_c.shape[1]), lambda g: (prep(g)[0], 0)),
            pl.BlockSpec((None, SUBLANES, tc), lambda g: (prep(g)[1], 0, prep(g)[0])),
            pl.BlockSpec((1, HEAD_DIM), lambda g: (0, 0)),
        ],
        out_specs=pl.BlockSpec((tc, 2 * HEAD_DIM), lambda g: rec(g)),
        out_shape=jax.ShapeDtypeStruct((m, 2 * n_kh * HEAD_DIM), BF16),
        scratch_shapes=[pltpu.VMEM((HEAD_DIM, 2 * HEAD_DIM), F32),
                        pltpu.VMEM((2, cpt, CHUNK + HEAD_DIM, 2 * HEAD_DIM), BF16),
                        pltpu.VMEM((2, cpt, CHUNK + HEAD_DIM, 2 * HEAD_DIM), F32),
                        pltpu.VMEM((2, cpt, SUBLANES, 2 * LANES), F32)],
        compiler_params=_cparams(1),
        name="dn_chunk",
    )(qk, qk, v, z, gate_c, gate_t, norm_w)


def _tile(n, want):
    if n <= want:
        return n
    t = want - want % LANES
    while n % t:
        t -= LANES
    return t


def _gate_slots(w_beta_a, a_log, dt_bias, n_vh):
    n_kh = n_vh // 2
    d = w_beta_a.shape[0]
    wb = w_beta_a[:, :n_vh].reshape(d, n_kh, 2)
    wa = w_beta_a[:, n_vh:].reshape(d, n_kh, 2)
    w_slot = jnp.concatenate([wb, wa, jnp.zeros((d, n_kh, SUBLANES - 4), w_beta_a.dtype)], axis=2)
    zeros2 = jnp.zeros((n_kh, 2), F32)
    zeros4 = jnp.zeros((n_kh, SUBLANES - 4), F32)
    al = jnp.concatenate([zeros2, a_log.astype(F32).reshape(n_kh, 2), zeros4], axis=1)
    dt = jnp.concatenate([zeros2, dt_bias.astype(F32).reshape(n_kh, 2), zeros4], axis=1)
    pad = (-n_kh * SUBLANES) % LANES
    w_slot = jnp.pad(w_slot.reshape(d, n_kh * SUBLANES), ((0, 0), (0, pad)))
    return w_slot, jnp.pad(al.reshape(-1), (0, pad)), jnp.pad(dt.reshape(-1), (0, pad))


def kernel(x, sc_w_in, sc_conv_w, sc_w_out, dn_w_in, dn_conv_w, dn_a_log, dn_dt_bias, dn_norm_w,
           dn_w_out, ffn_w_gate_up, ffn_w_down, ln_gain, ln_bias):
    batch, seq, d = x.shape
    depth = ln_gain.shape[0]
    alpha = float((2 * depth) ** 0.25)
    m = batch * seq
    n_vh = dn_a_log.shape[1]
    n_kh = n_vh // 2
    key_dim = n_kh * HEAD_DIM
    val_dim = n_vh * HEAD_DIM
    qkv_dim = 2 * key_dim + val_dim
    hidden = ffn_w_down.shape[1]
    assert seq % CHUNK == 0 and d % LANES == 0 and dn_norm_w.shape[1] == HEAD_DIM

    tm_in = _tile(seq, 1024)
    tm_dn = _tile(seq, 2048)
    tm_out = _tile(seq, 512)
    tn = _tile(d, 512)
    th = _tile(hidden, 512)
    tc = _tile(seq, 1024)

    sc_w_in_b, sc_w_out_b = sc_w_in.astype(BF16), sc_w_out.astype(BF16)
    dn_w_in_b, dn_w_out_b = dn_w_in.astype(BF16), dn_w_out.astype(BF16)
    ffn_w_gu_b, ffn_w_down_b = ffn_w_gate_up.astype(BF16), ffn_w_down.astype(BF16)

    h = x.reshape(m, d)
    hb = h
    for i in range(depth):
        l = i // 2
        gb0 = jnp.stack([ln_gain[i, 0], ln_bias[i, 0]])
        gb1 = jnp.stack([ln_gain[i, 1], ln_bias[i, 1]])
        if i % 2 == 0:
            a = _sc_in(hb, sc_w_in_b, l, sc_conv_w[l], seq, tm_in, tn)
            h = _proj_ln(a, sc_w_out_b, l, h, gb0, alpha, tm_out)
        else:
            cw = dn_conv_w[l]
            qk = _dn_in(hb, dn_w_in_b, l, cw, 0, 2 * key_dim, seq, tm_dn, tn, conv=True, l2norm=True,
                        n_scaled=key_dim // tn, name="dn_in_qk")
            v = _dn_in(hb, dn_w_in_b, l, cw, 2 * key_dim, val_dim, seq, tm_dn, tn, conv=True,
                       block_rows=256, name="dn_in_v")
            z = _dn_in(hb, dn_w_in_b, l, cw, qkv_dim, val_dim, seq, tm_dn, tn, conv=False, name="dn_in_z")
            w_slot, al_slot, dt_slot = _gate_slots(dn_w_in_b[l][:, qkv_dim + val_dim:], dn_a_log[l],
                                                   dn_dt_bias[l], n_vh)
            gate_c, gate_t = _dn_gates(hb, w_slot, al_slot, dt_slot, tm_dn)
            gate_t = gate_t[:n_kh * SUBLANES].reshape(n_kh, SUBLANES, m)
            o = _dn_chunk(qk, v, z, gate_c, gate_t, dn_norm_w[l][None, :], batch, seq, n_kh, tc)
            h = _proj_ln(o, dn_w_out_b, l, h, gb0, alpha, tm_out)
        h, hb = _ffn(h, ffn_w_gu_b, ffn_w_down_b, i, gb1, alpha, tm_out, th, emit_bf16=i + 1 < depth)
    return h.reshape(batch, seq, d)
```

```python
import functools

import jax
import jax.numpy as jnp
from jax import lax
from jax.experimental import pallas as pl
from jax.experimental.pallas import tpu as pltpu

HEAD_DIM = 128
CHUNK = 64
SUBLANES = 8
LANES = 128
LN_EPS = 1e-5
RMS_EPS = 1e-6
NEG_BIG = -1e30
FFN_BLOCK_ROWS = 256
PREP_GROUP = 16
VMEM_LIMIT_BYTES = 60 * 1024 * 1024

F32 = jnp.float32
BF16 = jnp.bfloat16


def _cparams(n_axes):
    return pltpu.CompilerParams(
        dimension_semantics=("arbitrary",) * n_axes,
        vmem_limit_bytes=VMEM_LIMIT_BYTES)


def _dot(a, b):
    return jnp.dot(a, b, preferred_element_type=F32)


def _dot_nt(a, b):
    return lax.dot_general(a, b, (((1,), (1,)), ((), ())), preferred_element_type=F32)


def _dot_tn(a, b):
    return lax.dot_general(a, b, (((0,), (0,)), ((), ())), preferred_element_type=F32)


def _layer_norm(y, gain, bias):
    mu = jnp.mean(y, axis=-1, keepdims=True)
    d = y - mu
    var = jnp.mean(d * d, axis=-1, keepdims=True)
    return d * lax.rsqrt(var + LN_EPS) * gain + bias


def _silu(t):
    return t * jax.nn.sigmoid(t)


def _causal_taps(ext, w_rows):
    k = len(w_rows)
    assert k <= SUBLANES
    s1 = pltpu.roll(ext, 1, axis=0)
    out = None
    for d0 in range(0, k, 2):
        term = ext * w_rows[k - 1 - d0]
        if d0 + 1 < k:
            term = term + s1 * w_rows[k - 2 - d0]
        if d0:
            term = pltpu.roll(term, d0, axis=0)
        out = term if out is None else out + term
    return out[SUBLANES:]


def _sc_in_kernel(x_ref, wb_ref, wc_ref, wh_ref, cw_ref, o_ref, *scratch, tiles_per_seq, n_sub):
    i, j = pl.program_id(0), pl.program_id(1)
    if x_ref.dtype == BF16:
        xb_ref = x_ref
        u0_ref, u1_ref, g0_ref, g1_ref, carry_ref = scratch
    else:
        xb_ref, u0_ref, u1_ref, g0_ref, g1_ref, carry_ref = scratch

        @pl.when(j == 0)
        def _():
            xb_ref[...] = x_ref[...].astype(BF16)

    ubuf, gbuf = (u0_ref, u1_ref), (g0_ref, g1_ref)
    tm = o_ref.shape[0]
    rs = tm // n_sub
    ksize = cw_ref.shape[0]

    @pl.when(i % tiles_per_seq == 0)
    def _():
        u0_ref[0:SUBLANES, :] = jnp.zeros((SUBLANES, u0_ref.shape[1]), F32)

    @pl.when(i % tiles_per_seq != 0)
    def _():
        u0_ref[0:SUBLANES, :] = carry_ref[j]

    def epilogue(b):
        for s in range(o_ref.shape[1] // LANES):
            lanes = slice(s * LANES, (s + 1) * LANES)
            y = _causal_taps(ubuf[b % 2][:, lanes], [cw_ref[k:k + 1, lanes] for k in range(ksize)])
            o_ref[b * rs:(b + 1) * rs, lanes] = (gbuf[b % 2][:, lanes] * y).astype(o_ref.dtype)

    tail = None
    for b in range(n_sub):
        xblk = xb_ref[b * rs:(b + 1) * rs, :]
        u = _dot(xblk, wc_ref[...]) * _dot(xblk, wh_ref[...])
        ubuf[b % 2][SUBLANES:SUBLANES + rs, :] = u
        gbuf[b % 2][...] = _dot(xblk, wb_ref[...])
        if b > 0:
            epilogue(b - 1)
            ubuf[b % 2][0:SUBLANES, :] = tail
        tail = u[rs - SUBLANES:, :]
    epilogue(n_sub - 1)
    carry_ref[j] = tail


def _sc_in(x, w_in, layer, conv_w, seq, tm, tn):
    m, d = x.shape
    nd = d // tn
    n_sub = max(1, tm // 128)
    rs = tm // n_sub
    ublk, gblk = pltpu.VMEM((rs + SUBLANES, tn), F32), pltpu.VMEM((rs, tn), F32)
    scratch = [ublk, ublk, gblk, gblk, pltpu.VMEM((nd, SUBLANES, tn), F32)]
    if x.dtype != BF16:
        scratch = [pltpu.VMEM((tm, d), BF16)] + scratch
    return pl.pallas_call(
        functools.partial(_sc_in_kernel, tiles_per_seq=seq // tm, n_sub=n_sub),
        grid=(m // tm, nd),
        in_specs=[
            pl.BlockSpec((tm, d), lambda i, j: (i, 0)),
            pl.BlockSpec((None, d, tn), lambda i, j: (layer, 0, j)),
            pl.BlockSpec((None, d, tn), lambda i, j: (layer, 0, nd + j)),
            pl.BlockSpec((None, d, tn), lambda i, j: (layer, 0, 2 * nd + j)),
            pl.BlockSpec((conv_w.shape[0], tn), lambda i, j: (0, j)),
        ],
        out_specs=pl.BlockSpec((tm, tn), lambda i, j: (i, j)),
        out_shape=jax.ShapeDtypeStruct((m, d), BF16),
        scratch_shapes=scratch,
        compiler_params=_cparams(2),
        name="sc_in",
    )(x, w_in, w_in, w_in, conv_w)


def _residual_ln_blocks(n_sub, rs, pre_norm, gb_ref, out_refs):
    gain, bias = gb_ref[0:1, :], gb_ref[1:2, :]

    def finish(b, y):
        out = _layer_norm(y, gain, bias)
        for r in out_refs:
            r[b * rs:(b + 1) * rs, :] = out.astype(r.dtype)

    pending = pre_norm(0)
    for b in range(1, n_sub):
        nxt = pre_norm(b)
        finish(b - 1, pending)
        pending = nxt
    finish(n_sub - 1, pending)


def _proj_ln_kernel(a_ref, w_ref, x_ref, gb_ref, o_ref, *, alpha, n_sub):
    rs = o_ref.shape[0] // n_sub

    def pre_norm(b):
        rows = slice(b * rs, (b + 1) * rs)
        return alpha * x_ref[rows, :] + _dot(a_ref[rows, :], w_ref[...])

    _residual_ln_blocks(n_sub, rs, pre_norm, gb_ref, [o_ref])


def _proj_ln(a, w, layer, x, gain_bias, alpha, tm):
    m, d = x.shape
    k = a.shape[1]
    return pl.pallas_call(
        functools.partial(_proj_ln_kernel, alpha=alpha, n_sub=max(1, tm // 128)),
        grid=(m // tm,),
        in_specs=[
            pl.BlockSpec((tm, k), lambda i: (i, 0)),
            pl.BlockSpec((None, k, d), lambda i: (layer, 0, 0), pipeline_mode=pl.Buffered(1)),
            pl.BlockSpec((tm, d), lambda i: (i, 0)),
            pl.BlockSpec((2, d), lambda i: (0, 0)),
        ],
        out_specs=pl.BlockSpec((tm, d), lambda i: (i, 0)),
        out_shape=jax.ShapeDtypeStruct((m, d), F32),
        compiler_params=_cparams(1),
        name="proj_ln",
    )(a, w, x, gain_bias)


def _ffn_kernel(x_ref, wg_ref, wu_ref, wd_ref, gb_ref, o_ref, *rest, alpha):
    *ob_ref, xb_ref, acc_ref = rest
    j = pl.program_id(1)

    @pl.when(j == 0)
    def _():
        xb_ref[...] = x_ref[...].astype(BF16)
        acc_ref[...] = jnp.zeros_like(acc_ref)

    for r0 in range(0, xb_ref.shape[0], FFN_BLOCK_ROWS):
        rows = slice(r0, r0 + FFN_BLOCK_ROWS)
        xb = xb_ref[rows, :]
        act = _silu(_dot(xb, wg_ref[...])) * _dot(xb, wu_ref[...])
        acc_ref[rows, :] += _dot(act.astype(BF16), wd_ref[...])

    @pl.when(j == pl.num_programs(1) - 1)
    def _():
        y = alpha * x_ref[...] + acc_ref[...]
        out = _layer_norm(y, gb_ref[0:1, :], gb_ref[1:2, :])
        o_ref[...] = out
        for r in ob_ref:
            r[...] = out.astype(BF16)


def _ffn(x, w_gate_up, w_down, layer, gain_bias, alpha, tm, th, emit_bf16):
    m, d = x.shape
    hidden = w_down.shape[1]
    nh = hidden // th
    row_block = pl.BlockSpec((tm, d), lambda i, j: (i, 0))
    outs = pl.pallas_call(
        functools.partial(_ffn_kernel, alpha=alpha),
        grid=(m // tm, nh),
        in_specs=[
            row_block,
            pl.BlockSpec((None, d, th), lambda i, j: (layer, 0, j)),
            pl.BlockSpec((None, d, th), lambda i, j: (layer, 0, nh + j)),
            pl.BlockSpec((None, th, d), lambda i, j: (layer, j, 0)),
            pl.BlockSpec((2, d), lambda i, j: (0, 0)),
        ],
        out_specs=[row_block] * (2 if emit_bf16 else 1),
        out_shape=[jax.ShapeDtypeStruct((m, d), F32)] + [jax.ShapeDtypeStruct((m, d), BF16)] * emit_bf16,
        scratch_shapes=[pltpu.VMEM((tm, d), BF16), pltpu.VMEM((tm, d), F32)],
        compiler_params=_cparams(2),
        name="ffn",
    )(x, w_gate_up, w_gate_up, w_down, gain_bias)
    return outs if emit_bf16 else (outs[0], None)


def _dn_in_kernel(xb_ref, w_ref, cw_ref, o_ref, *scratch, tiles_per_seq, conv, l2norm, n_scaled, n_sub):
    i, j = pl.program_id(0), pl.program_id(1)
    tm = o_ref.shape[0]
    rs = tm // n_sub

    def block_dot(b):
        return _dot(xb_ref[b * rs:(b + 1) * rs, :], w_ref[...])

    if not conv:
        for b in range(n_sub):
            o_ref[b * rs:(b + 1) * rs, :] = block_dot(b).astype(o_ref.dtype)
        return

    acc0_ref, acc1_ref, carry_ref = scratch
    acc = (acc0_ref, acc1_ref)
    ksize = cw_ref.shape[0]

    @pl.when(i % tiles_per_seq == 0)
    def _():
        acc0_ref[0:SUBLANES, :] = jnp.zeros((SUBLANES, acc0_ref.shape[1]), F32)

    @pl.when(i % tiles_per_seq != 0)
    def _():
        acc0_ref[0:SUBLANES, :] = carry_ref[j]

    scale = jnp.where(j < n_scaled, HEAD_DIM ** -0.5, 1.0).astype(F32)

    def epilogue(b):
        src = acc[b % 2]
        for s in range(o_ref.shape[1] // HEAD_DIM):
            lanes = slice(s * HEAD_DIM, (s + 1) * HEAD_DIM)
            t = _silu(_causal_taps(src[:, lanes], [cw_ref[k:k + 1, lanes] for k in range(ksize)]))
            if l2norm:
                t = t * (lax.rsqrt(jnp.sum(t * t, axis=-1, keepdims=True) + RMS_EPS) * scale)
            o_ref[b * rs:(b + 1) * rs, lanes] = t.astype(o_ref.dtype)

    tail = None
    for b in range(n_sub):
        res = block_dot(b)
        acc[b % 2][SUBLANES:SUBLANES + rs, :] = res
        if b > 0:
            epilogue(b - 1)
            acc[b % 2][0:SUBLANES, :] = tail
        tail = res[rs - SUBLANES:, :]
    epilogue(n_sub - 1)
    carry_ref[j] = tail


def _dn_in(xb, w_in, layer, conv_w, col0, width, seq, tm, tn, *, conv, l2norm=False, n_scaled=0,
           block_rows=128, name):
    m, d = xb.shape
    nj = width // tn
    j0 = col0 // tn
    n_sub = max(1, tm // block_rows)
    blk = pltpu.VMEM((tm // n_sub + SUBLANES, tn), F32)
    scratch = [blk, blk, pltpu.VMEM((nj, SUBLANES, tn), F32)] if conv else []
    cw_j0 = j0 if conv else 0
    return pl.pallas_call(
        functools.partial(_dn_in_kernel, tiles_per_seq=seq // tm, conv=conv, l2norm=l2norm,
                          n_scaled=n_scaled, n_sub=n_sub),
        grid=(m // tm, nj),
        in_specs=[
            pl.BlockSpec((tm, d), lambda i, j: (i, 0)),
            pl.BlockSpec((None, d, tn), lambda i, j: (layer, 0, j0 + j)),
            pl.BlockSpec((conv_w.shape[0], tn), lambda i, j: (0, cw_j0 + j if conv else 0)),
        ],
        out_specs=pl.BlockSpec((tm, tn), lambda i, j: (i, j)),
        out_shape=jax.ShapeDtypeStruct((m, width), BF16),
        scratch_shapes=scratch,
        compiler_params=_cparams(2),
        name=name,
    )(xb, w_in, conv_w)


def _softplus(t):
    return jnp.maximum(t, 0.0) + jnp.log1p(jnp.exp(-jnp.abs(t)))


def _chunk_cumsum(g, axis):
    pos = lax.broadcasted_iota(jnp.int32, g.shape, axis) % CHUNK
    s = 1
    while s < CHUNK:
        g = g + jnp.where(pos >= s, pltpu.roll(g, s, axis=axis), 0.0)
        s *= 2
    return g


def _gates(raw, a_log, dt_bias, slot_axis, token_axis):
    c = lax.broadcasted_iota(jnp.int32, raw.shape, slot_axis) % SUBLANES
    beta = jax.nn.sigmoid(raw)
    g = -jnp.exp(a_log) * _softplus(raw + dt_bias)
    g = _chunk_cumsum(jnp.where(jnp.logical_and(c >= 2, c < 4), g, 0.0), token_axis)
    return jnp.where(c < 2, beta, g)


def _dn_gate_kernel(x_ref, w_ref, wt_ref, al_ref, dt_ref, alt_ref, dtt_ref, gc_ref, gt_ref):
    xb = x_ref[...]
    gc_ref[...] = _gates(_dot(xb, w_ref[...]), al_ref[...], dt_ref[...], 1, 0)
    gt_ref[...] = _gates(_dot_nt(wt_ref[...], xb), alt_ref[...], dtt_ref[...], 0, 1)


def _dn_gates(x, w_slot, a_log_slot, dt_slot, tm):
    m, d = x.shape
    ns = w_slot.shape[1]
    return pl.pallas_call(
        _dn_gate_kernel,
        grid=(m // tm,),
        in_specs=[
            pl.BlockSpec((tm, d), lambda i: (i, 0)),
            pl.BlockSpec((d, ns), lambda i: (0, 0)),
            pl.BlockSpec((ns, d), lambda i: (0, 0)),
            pl.BlockSpec((1, ns), lambda i: (0, 0)),
            pl.BlockSpec((1, ns), lambda i: (0, 0)),
            pl.BlockSpec((ns, 1), lambda i: (0, 0)),
            pl.BlockSpec((ns, 1), lambda i: (0, 0)),
        ],
        out_specs=[pl.BlockSpec((tm, ns), lambda i: (i, 0)),
                   pl.BlockSpec((ns, tm), lambda i: (0, i))],
        out_shape=[jax.ShapeDtypeStruct((m, ns), F32), jax.ShapeDtypeStruct((ns, m), F32)],
        compiler_params=_cparams(1),
        name="dn_gates",
    )(x, w_slot, w_slot.T, a_log_slot[None, :], dt_slot[None, :], a_log_slot[:, None], dt_slot[:, None])


def _block_diag(a, b):
    zero = jnp.zeros_like(a)
    return jnp.concatenate([jnp.concatenate([a, zero], axis=1), jnp.concatenate([zero, b], axis=1)], axis=0)


def _dn_chunk_kernel(q_ref, k_ref, v_ref, z_ref, gc_ref, gt_ref, nw_ref, o_ref,
                     state_ref, qm_ref, on_ref, cd_ref, *, nt, n_kh):
    g = pl.program_id(0)
    kh = (jnp.minimum(g, pl.num_programs(0) - 2) // nt) % n_kh
    n_chunks = q_ref.shape[0] // CHUNK
    slot_a = g % 2
    slot_b = 1 - slot_a

    @pl.when(g == 0)
    def _():
        qm_ref[1] = jnp.zeros(qm_ref.shape[1:], qm_ref.dtype)
        on_ref[1] = jnp.zeros(on_ref.shape[1:], on_ref.dtype)
        cd_ref[1] = jnp.zeros(cd_ref.shape[1:], cd_ref.dtype)

    @pl.when(jnp.maximum(g - 1, 0) % nt == 0)
    def _():
        state_ref[...] = jnp.zeros_like(state_ref)

    row = lax.broadcasted_iota(jnp.int32, (CHUNK, LANES), 0)
    lane = lax.broadcasted_iota(jnp.int32, (CHUNK, LANES), 1)
    col = lane % CHUNK
    causal = col <= row
    strict = col < row
    upper_half = lane >= CHUNK
    lower_half = lane < CHUNK
    eye = col == row
    eye_hi = jnp.where(jnp.logical_and(upper_half, eye), 1.0, 0.0).astype(F32)
    eye_lo = jnp.where(jnp.logical_and(lower_half, eye), 1.0, 0.0).astype(F32)
    lower_half_b = lane.astype(F32).astype(BF16) < CHUNK
    zero_b = jnp.zeros((CHUNK, LANES), BF16)
    norm_w = nw_ref[...]
    state = [state_ref[...]]

    def recurrence_chunk(c):
        rows = slice(c * CHUNK, (c + 1) * CHUNK)
        s_b = state[0].astype(BF16)
        qms = _dot(qm_ref[slot_b, c], _block_diag(s_b[:, :HEAD_DIM], s_b[:, HEAD_DIM:]))
        o_n = on_ref[slot_b, c]
        o_all = qms[:CHUNK] + o_n[:CHUNK]
        state[0] = state[0] * cd_ref[slot_b, c][:1, :] + o_n[CHUNK:] - qms[CHUNK:]
        for e in range(2):
            o = o_all[:, e * HEAD_DIM:(e + 1) * HEAD_DIM]
            o = o * lax.rsqrt(jnp.mean(o * o, axis=-1, keepdims=True) + RMS_EPS) * norm_w
            ze = z_ref[rows, e * HEAD_DIM:(e + 1) * HEAD_DIM].astype(F32)
            o_ref[rows, e * HEAD_DIM:(e + 1) * HEAD_DIM] = (o * _silu(ze)).astype(o_ref.dtype)

    gsel_all = pltpu.roll(gc_ref[...], (LANES - SUBLANES * kh) % LANES, axis=1)
    kf, qf, k_t, qk2, kk2 = {}, {}, {}, {}, {}
    beta_col, g_col, g_last, decay, x_mat, w_mat, sol, exp_g = {}, {}, {}, {}, {}, {}, {}, {}

    def load_chunk(c):
        rows = slice(c * CHUNK, (c + 1) * CHUNK)
        qc, kc = q_ref[rows, :], k_ref[rows, :]
        qf[c] = qc.astype(F32)
        kf[c] = kc.astype(F32)
        k_t[c] = jnp.transpose(kf[c]).astype(BF16)
        qkk = _dot_nt(jnp.concatenate([qc, kc], axis=0), jnp.concatenate([kc, kc], axis=0))
        qk2[c] = qkk[:CHUNK]
        kk2[c] = qkk[CHUNK:]

    def gate_terms(c, e):
        gsel = gsel_all[c * CHUNK:(c + 1) * CHUNK]
        g_tok = gt_ref[2 + e:3 + e, c * CHUNK:(c + 1) * CHUNK]
        g_row = jnp.concatenate([g_tok, g_tok], axis=1)
        beta_col[c, e] = jnp.broadcast_to(gsel[:, e:e + 1], (CHUNK, LANES))
        g_col[c, e] = jnp.broadcast_to(gsel[:, 2 + e:3 + e], (CHUNK, LANES))
        g_last[c, e] = g_row[:, CHUNK - 1:CHUNK]
        decay[c, e] = jnp.exp(jnp.where(causal, g_col[c, e] - g_row, NEG_BIG))
        x_mat[c, e] = jnp.where(strict, -(kk2[c] * beta_col[c, e] * decay[c, e]), 0.0)

    def square(c):
        xa, xb = x_mat[c, 0], x_mat[c, 1]
        sq = _dot(jnp.where(lower_half, xa, xb).astype(BF16), _block_diag(xa.astype(BF16), xb.astype(BF16)))
        w_mat[c, 0] = jnp.where(upper_half, xa + eye_hi, sq[:, :LANES]).astype(BF16)
        w_mat[c, 1] = jnp.where(upper_half, sq[:, LANES:], xb + eye_lo).astype(BF16)

    def advance(c):
        wa, wb = w_mat[c, 0], w_mat[c, 1]
        pw = _dot(jnp.where(lower_half_b, wa, wb), _block_diag(wa, wb))
        w_mat[c, 0] = pw[:, :LANES].astype(BF16) + jnp.where(lower_half_b, zero_b, wa)
        w_mat[c, 1] = pw[:, LANES:].astype(BF16) + jnp.where(lower_half_b, wb, zero_b)

    def solve(c, e):
        t_mat = w_mat[c, e][:, CHUNK:] if e == 0 else w_mat[c, e][:, :CHUNK]
        ve = v_ref[c * CHUNK:(c + 1) * CHUNK, e * HEAD_DIM:(e + 1) * HEAD_DIM].astype(F32)
        exp_g[c, e] = jnp.exp(g_col[c, e])
        rhs = jnp.concatenate([kf[c] * (beta_col[c, e] * exp_g[c, e]), ve * beta_col[c, e]], axis=1)
        sol[c, e] = _dot(t_mat, rhs.astype(BF16))

    def finish(c):
        qm, o_n, cd = [], [], []
        for e in range(2):
            a_qk = (qk2[c] * decay[c, e])[:, :CHUNK].astype(BF16)
            qo = _dot(a_qk, sol[c, e].astype(BF16))
            q_eff = qf[c] * exp_g[c, e] - qo[:, :HEAD_DIM]
            k_tail = jnp.exp(g_last[c, e] - g_col[c, e])[:, :1]
            mn = _dot(k_t[c], (sol[c, e] * k_tail).astype(BF16))
            qm.append(jnp.concatenate([q_eff, mn[:, :HEAD_DIM]], axis=0).astype(BF16))
            o_n.append(jnp.concatenate([qo[:, HEAD_DIM:], mn[:, HEAD_DIM:]], axis=0))
            cd.append(jnp.broadcast_to(jnp.exp(g_last[c, e]), (SUBLANES, LANES)))
        qm_ref[slot_a, c] = jnp.concatenate(qm, axis=1)
        on_ref[slot_a, c] = jnp.concatenate(o_n, axis=1)
        cd_ref[slot_a, c] = jnp.concatenate(cd, axis=1)

    chunks = range(n_chunks)
    items = []
    for g0 in range(0, n_chunks, PREP_GROUP):
        group = range(g0, min(g0 + PREP_GROUP, n_chunks))
        chains = [(c, e) for c in group for e in range(2)]
        items += [functools.partial(load_chunk, c) for c in group]
        items += [functools.partial(gate_terms, c, e) for (c, e) in chains]
        items += [functools.partial(square, c) for c in group]
        items += [functools.partial(advance, c) for _ in range(5) for c in group]
        items += [functools.partial(solve, c, e) for (c, e) in chains]
        items += [functools.partial(finish, c) for c in group]
    for idx, item in enumerate(items):
        for c in chunks:
            if idx == c * len(items) // n_chunks:
                recurrence_chunk(c)
        item()

    state_ref[...] = state[0]


def _dn_chunk(qk, v, z, gate_c, gate_t, norm_w, batch, seq, n_kh, tc):
    m = qk.shape[0]
    nt = seq // tc
    cpt = tc // CHUNK

    n_tiles = batch * n_kh * nt

    def split(t):
        return (t // (n_kh * nt)) * nt + t % nt, (t // nt) % n_kh

    def prep(g):
        return split(jnp.minimum(g, n_tiles - 1))

    def rec(g):
        return split(jnp.maximum(g - 1, 0))

    return pl.pallas_call(
        functools.partial(_dn_chunk_kernel, nt=nt, n_kh=n_kh),
        grid=(n_tiles + 1,),
        in_specs=[
            pl.BlockSpec((tc, HEAD_DIM), lambda g: prep(g)),
            pl.BlockSpec((tc, HEAD_DIM), lambda g: (prep(g)[0], n_kh + prep(g)[1])),
            pl.BlockSpec((tc, 2 * HEAD_DIM), lambda g: prep(g)),
            pl.BlockSpec((tc, 2 * HEAD_DIM), lambda g: rec(g)),
            pl.BlockSpec((tc, gate_c.shape[1]), lambda g: (prep(g)[0], 0)),
            pl.BlockSpec((None, SUBLANES, tc), lambda g: (prep(g)[1], 0, prep(g)[0])),
            pl.BlockSpec((1, HEAD_DIM), lambda g: (0, 0)),
        ],
        out_specs=pl.BlockSpec((tc, 2 * HEAD_DIM), lambda g: rec(g)),
        out_shape=jax.ShapeDtypeStruct((m, 2 * n_kh * HEAD_DIM), BF16),
        scratch_shapes=[pltpu.VMEM((HEAD_DIM, 2 * HEAD_DIM), F32),
                        pltpu.VMEM((2, cpt, CHUNK + HEAD_DIM, 2 * HEAD_DIM), BF16),
                        pltpu.VMEM((2, cpt, CHUNK + HEAD_DIM, 2 * HEAD_DIM), F32),
                        pltpu.VMEM((2, cpt, SUBLANES, 2 * LANES), F32)],
        compiler_params=_cparams(1),
        name="dn_chunk",
    )(qk, qk, v, z, gate_c, gate_t, norm_w)


def _tile(n, want):
    if n <= want:
        return n
    t = want - want % LANES
    while n % t:
        t -= LANES
    return t


def _gate_slots(w_beta_a, a_log, dt_bias, n_vh):
    n_kh = n_vh // 2
    d = w_beta_a.shape[0]
    wb = w_beta_a[:, :n_vh].reshape(d, n_kh, 2)
    wa = w_beta_a[:, n_vh:].reshape(d, n_kh, 2)
    w_slot = jnp.concatenate([wb, wa, jnp.zeros((d, n_kh, SUBLANES - 4), w_beta_a.dtype)], axis=2)
    zeros2 = jnp.zeros((n_kh, 2), F32)
    zeros4 = jnp.zeros((n_kh, SUBLANES - 4), F32)
    al = jnp.concatenate([zeros2, a_log.astype(F32).reshape(n_kh, 2), zeros4], axis=1)
    dt = jnp.concatenate([zeros2, dt_bias.astype(F32).reshape(n_kh, 2), zeros4], axis=1)
    pad = (-n_kh * SUBLANES) % LANES
    w_slot = jnp.pad(w_slot.reshape(d, n_kh * SUBLANES), ((0, 0), (0, pad)))
    return w_slot, jnp.pad(al.reshape(-1), (0, pad)), jnp.pad(dt.reshape(-1), (0, pad))


def kernel(x, sc_w_in, sc_conv_w, sc_w_out, dn_w_in, dn_conv_w, dn_a_log, dn_dt_bias, dn_norm_w,
           dn_w_out, ffn_w_gate_up, ffn_w_down, ln_gain, ln_bias):
    batch, seq, d = x.shape
    depth = ln_gain.shape[0]
    alpha = float((2 * depth) ** 0.25)
    m = batch * seq
    n_vh = dn_a_log.shape[1]
    n_kh = n_vh // 2
    key_dim = n_kh * HEAD_DIM
    val_dim = n_vh * HEAD_DIM
    qkv_dim = 2 * key_dim + val_dim
    hidden = ffn_w_down.shape[1]
    assert seq % CHUNK == 0 and d % LANES == 0 and dn_norm_w.shape[1] == HEAD_DIM

    tm_in = _tile(seq, 1024)
    tm_dn = _tile(seq, 2048)
    tm_out = _tile(seq, 512)
    tn = _tile(d, 512)
    th = _tile(hidden, 512)
    tc = _tile(seq, 1024)

    sc_w_in_b, sc_w_out_b = sc_w_in.astype(BF16), sc_w_out.astype(BF16)
    dn_w_in_b, dn_w_out_b = dn_w_in.astype(BF16), dn_w_out.astype(BF16)
    ffn_w_gu_b, ffn_w_down_b = ffn_w_gate_up.astype(BF16), ffn_w_down.astype(BF16)

    h = x.reshape(m, d)
    hb = h
    for i in range(depth):
        l = i // 2
        gb0 = jnp.stack([ln_gain[i, 0], ln_bias[i, 0]])
        gb1 = jnp.stack([ln_gain[i, 1], ln_bias[i, 1]])
        if i % 2 == 0:
            a = _sc_in(hb, sc_w_in_b, l, sc_conv_w[l], seq, tm_in, tn)
            h = _proj_ln(a, sc_w_out_b, l, h, gb0, alpha, tm_out)
        else:
            cw = dn_conv_w[l]
            qk = _dn_in(hb, dn_w_in_b, l, cw, 0, 2 * key_dim, seq, tm_dn, tn, conv=True, l2norm=True,
                        n_scaled=key_dim // tn, name="dn_in_qk")
            v = _dn_in(hb, dn_w_in_b, l, cw, 2 * key_dim, val_dim, seq, tm_dn, tn, conv=True,
                       block_rows=256, name="dn_in_v")
            z = _dn_in(hb, dn_w_in_b, l, cw, qkv_dim, val_dim, seq, tm_dn, tn, conv=False, name="dn_in_z")
            w_slot, al_slot, dt_slot = _gate_slots(dn_w_in_b[l][:, qkv_dim + val_dim:], dn_a_log[l],
                                                   dn_dt_bias[l], n_vh)
            gate_c, gate_t = _dn_gates(hb, w_slot, al_slot, dt_slot, tm_dn)
            gate_t = gate_t[:n_kh * SUBLANES].reshape(n_kh, SUBLANES, m)
            o = _dn_chunk(qk, v, z, gate_c, gate_t, dn_norm_w[l][None, :], batch, seq, n_kh, tc)
            h = _proj_ln(o, dn_w_out_b, l, h, gb0, alpha, tm_out)
        h, hb = _ffn(h, ffn_w_gu_b, ffn_w_down_b, i, gb1, alpha, tm_out, th, emit_bf16=i + 1 < depth)
    return h.reshape(batch, seq, d)
```
